```python
import math
import jax, jax.numpy as jnp
from jax import lax
import numpy as np

D_MODEL = 1024
BATCH = 1
SEQ = 16384
DEPTH = 1

D_MIX = D_MODEL
POOL_WIDTH = D_MIX // 4
POOL_WINDOWS = (2, 4, 8, 16)
POOL_GROUP = POOL_WIDTH // len(POOL_WINDOWS)
ATT_WIDTH = D_MIX - POOL_WIDTH
HEAD_DIM = 64
N_ATT_HEADS = ATT_WIDTH // HEAD_DIM
DILATED_GROUPS = ((128, 1), (512, 4), (2048, 16))
ATT_BLOCK = 128
N_BUCKETS = 32
MAX_DISTANCE = 2048
N_EXPERTS = 32
TOP_K = 4
D_FF = D_MODEL
SWIGLU_LIMIT = 7.0
SWIGLU_ALPHA = 1.702
MOE_BLOCK = 128
RMS_EPS = 1e-5
NEG_INF = -1e30

kernel_name = 'hybrid_pool_dilated_attn_moe'


def rmsnorm(x, g):
    xf = x.astype(jnp.float32)
    y = xf * lax.rsqrt(jnp.mean(xf * xf, axis=-1, keepdims=True) + RMS_EPS)
    return (y * g.astype(jnp.float32)).astype(x.dtype)


def pool_mixer(u, pool_w, pool_scale):
    B, S, _ = u.shape
    maxw = max(POOL_WINDOWS)
    uf = u.astype(jnp.float32)
    c = jnp.cumsum(uf, axis=1)
    c_pad = jnp.pad(c, ((0, 0), (maxw, 0), (0, 0)))
    t = jnp.arange(S)
    outs = []
    for g, w in enumerate(POOL_WINDOWS):
        lo, hi = g * POOL_GROUP, (g + 1) * POOL_GROUP
        win_sum = c_pad[:, maxw:, lo:hi] - c_pad[:, maxw - w:maxw - w + S, lo:hi]
        cnt = jnp.minimum(t + 1, w).astype(jnp.float32)[None, :, None]
        outs.append(win_sum / cnt - uf[:, :, lo:hi])
    pooled = jnp.stack(outs, axis=2).astype(u.dtype)
    mixed = jnp.einsum('bsgc,gcd->bsgd', pooled, pool_w)
    return mixed.reshape(B, S, POOL_WIDTH) * pool_scale


def t5_bucket(dist):
    max_exact = N_BUCKETS // 2
    is_small = dist < max_exact
    nf = jnp.maximum(dist, 1).astype(jnp.float32)
    large = max_exact + (jnp.log(nf / max_exact) / math.log(MAX_DISTANCE / max_exact)
                         * (N_BUCKETS - max_exact)).astype(jnp.int32)
    large = jnp.minimum(large, N_BUCKETS - 1)
    return jnp.where(is_small, dist, large)


def dilated_branch(q, k, v, rel_bias, window, dil):
    B, S, H, Dh = q.shape
    w_sub = window // dil
    assert w_sub <= ATT_BLOCK
    span = dil * ATT_BLOCK
    S_pad = -(-S // span) * span
    M = S_pad // dil
    nb = M // ATT_BLOCK

    def to_blocks(a):
        a = jnp.pad(a, ((0, 0), (0, S_pad - S), (0, 0), (0, 0)))
        a = a.reshape(B, M, dil, H, Dh).transpose(0, 2, 1, 3, 4)
        return a.reshape(B, dil, nb, ATT_BLOCK, H, Dh)

    def band(a):
        prev = jnp.pad(a[:, :, :-1], ((0, 0), (0, 0), (1, 0), (0, 0), (0, 0), (0, 0)))
        return jnp.concatenate([prev, a], axis=3)

    qb = to_blocks(q)
    kb = band(to_blocks(k))
    vb = band(to_blocks(v))

    qi = jnp.arange(ATT_BLOCK)[:, None]
    kj = jnp.arange(2 * ATT_BLOCK)[None, :]
    dist = qi + ATT_BLOCK - kj
    in_win = (dist >= 0) & (dist <= w_sub)
    bias = rel_bias[t5_bucket(jnp.clip(dist, 0) * dil)]
    bias = bias.transpose(2, 0, 1).astype(jnp.float32)
    has_prev = (jnp.arange(nb)[:, None, None] > 0) | (kj >= ATT_BLOCK)[None]
    mask = in_win[None] & has_prev

    s = jnp.einsum('brnqhd,brnkhd->brnhqk', qb, kb).astype(jnp.float32) * (HEAD_DIM ** -0.5)
    s = jnp.where(mask[None, None, :, None], s + bias[None, None, None], NEG_INF)
    m = jnp.max(s, axis=-1, keepdims=True)
    p = jnp.exp(s - m)
    l = jnp.sum(p, axis=-1, keepdims=True)
    o = jnp.einsum('brnhqk,brnkhd->brnqhd', (p / l).astype(v.dtype), vb)
    lse = (m + jnp.log(l))[..., 0]

    o = o.reshape(B, dil, M, H, Dh).transpose(0, 2, 1, 3, 4).reshape(B, S_pad, H, Dh)[:, :S]
    lse = lse.transpose(0, 1, 2, 4, 3).reshape(B, dil, M, H).transpose(0, 2, 1, 3)
    lse = lse.reshape(B, S_pad, H)[:, :S]
    return o, lse


def dilated_attention(q, k, v, rel_bias):
    outs, lses = [], []
    for window, dil in DILATED_GROUPS:
        o, lse = dilated_branch(q, k, v, rel_bias, window, dil)
        outs.append(o)
        lses.append(lse)
    wts = jax.nn.softmax(jnp.stack(lses, axis=0), axis=0)
    o = jnp.stack(outs, axis=0).astype(jnp.float32)
    return jnp.sum(wts[..., None] * o, axis=0).astype(q.dtype)


def clamped_swiglu(gu):
    gate, up = gu[..., :D_FF], gu[..., D_FF:]
    gate = jnp.minimum(gate, SWIGLU_LIMIT)
    up = jnp.clip(up, -SWIGLU_LIMIT, SWIGLU_LIMIT)
    return (up + 1.0) * (gate * jax.nn.sigmoid(gate * SWIGLU_ALPHA))


def moe(h, router_w, router_b, w_gate_up, b_gate_up, w_down, b_down):
    N, D = h.shape
    logits = (h @ router_w + router_b).astype(jnp.float32)
    top_v, top_i = lax.top_k(logits, TOP_K)
    gates = jax.nn.softmax(top_v, axis=-1)
    A = N * TOP_K
    e_flat = top_i.reshape(A)
    g_flat = gates.reshape(A)
    tok_flat = jnp.arange(A, dtype=jnp.int32) // TOP_K
    order = jnp.argsort(e_flat)
    e_s, tok_s, g_s = e_flat[order], tok_flat[order], g_flat[order]
    counts = jnp.zeros((N_EXPERTS,), jnp.int32).at[e_flat].add(1)
    starts = jnp.cumsum(counts) - counts
    blocks_per = (counts + MOE_BLOCK - 1) // MOE_BLOCK
    block_end = jnp.cumsum(blocks_per)
    pstarts = (block_end - blocks_per) * MOE_BLOCK
    dest = pstarts[e_s] + (jnp.arange(A, dtype=jnp.int32) - starts[e_s])
    nb = -(-A // MOE_BLOCK) + N_EXPERTS
    P = nb * MOE_BLOCK
    slot_tok = jnp.full((P,), N, jnp.int32).at[dest].set(tok_s)
    slot_gate = jnp.zeros((P,), h.dtype).at[dest].set(g_s.astype(h.dtype))
    block_exp = jnp.minimum(jnp.searchsorted(block_end, jnp.arange(nb, dtype=jnp.int32), side='right'),
                            N_EXPERTS - 1)
    h_pad = jnp.concatenate([h, jnp.zeros((1, D), h.dtype)], axis=0)
    xb = h_pad[slot_tok].reshape(nb, MOE_BLOCK, D)

    def expert_block(args):
        xblk, e = args
        act = clamped_swiglu(xblk @ w_gate_up[e] + b_gate_up[e])
        return act @ w_down[e] + b_down[e]

    yb = lax.map(expert_block, (xb, block_exp)).reshape(P, D)
    out = jnp.zeros((N + 1, D), h.dtype).at[slot_tok].add(yb * slot_gate[:, None])
    return out[:N]


def setup_inputs(seed: int = 0) -> dict:
    key = jax.random.key(seed)
    ks = jax.random.split(key, 16)
    f32 = jnp.float32
    n_in = POOL_WIDTH + 3 * ATT_WIDTH
    nrm = lambda k, shape, scale: jax.random.normal(k, shape, f32) * scale
    return {
        'x': jax.random.normal(ks[0], (BATCH, SEQ, D_MODEL), f32),
        'norm1_g': 1.0 + nrm(ks[1], (DEPTH, D_MODEL), 0.05),
        'w_in': nrm(ks[2], (DEPTH, D_MODEL, n_in), D_MODEL ** -0.5),
        'pool_w': nrm(ks[3], (DEPTH, len(POOL_WINDOWS), POOL_GROUP, POOL_GROUP), POOL_GROUP ** -0.5),
        'pool_scale': 1.0 + nrm(ks[4], (DEPTH, POOL_WIDTH), 0.1),
        'rel_bias': nrm(ks[5], (N_BUCKETS, N_ATT_HEADS), 0.5),
        'w_out': nrm(ks[6], (DEPTH, D_MIX, D_MODEL), D_MIX ** -0.5),
        'norm2_g': 1.0 + nrm(ks[7], (DEPTH, D_MODEL), 0.05),
        'router_w': nrm(ks[8], (DEPTH, D_MODEL, N_EXPERTS), D_MODEL ** -0.5),
        'router_b': nrm(ks[9], (DEPTH, N_EXPERTS), 0.01),
        'w_gate_up': nrm(ks[10], (DEPTH, N_EXPERTS, D_MODEL, 2 * D_FF), D_MODEL ** -0.5),
        'b_gate_up': nrm(ks[11], (DEPTH, N_EXPERTS, 2 * D_FF), 0.02),
        'w_down': nrm(ks[12], (DEPTH, N_EXPERTS, D_FF, D_MODEL), D_FF ** -0.5),
        'b_down': nrm(ks[13], (DEPTH, N_EXPERTS, D_MODEL), 0.02),
        'final_g': 1.0 + nrm(ks[14], (D_MODEL,), 0.05),
    }


def reference(x, norm1_g, w_in, pool_w, pool_scale, rel_bias, w_out, norm2_g,
              router_w, router_b, w_gate_up, b_gate_up, w_down, b_down, final_g):
    B, S, D = x.shape
    for l in range(DEPTH):
        h = rmsnorm(x, norm1_g[l])
        proj = h @ w_in[l]
        u = proj[..., :POOL_WIDTH]
        qkv = proj[..., POOL_WIDTH:].reshape(B, S, 3, N_ATT_HEADS, HEAD_DIM)
        q, k, v = qkv[:, :, 0], qkv[:, :, 1], qkv[:, :, 2]
        pool_out = pool_mixer(u, pool_w[l], pool_scale[l])
        att_out = dilated_attention(q, k, v, rel_bias).reshape(B, S, ATT_WIDTH)
        mix = jnp.concatenate([pool_out.astype(x.dtype), att_out], axis=-1)
        x = x + mix @ w_out[l]
        h2 = rmsnorm(x, norm2_g[l]).reshape(B * S, D)
        x = x + moe(h2, router_w[l], router_b[l], w_gate_up[l], b_gate_up[l],
                    w_down[l], b_down[l]).reshape(B, S, D)
    return rmsnorm(x, final_g)
```

```python
import math

import jax
import jax.numpy as jnp
from jax import lax
from jax.experimental import pallas as pl
from jax.experimental.pallas import tpu as pltpu

F32 = jnp.float32
BF16 = jnp.bfloat16

D_MODEL = 1024
POOL_WIDTH = 256
POOL_WINDOWS = (2, 4, 8, 16)
POOL_GROUP = 64
ATT_WIDTH = 768
HEAD_DIM = 64
N_HEADS = 12
N_PAIRS = N_HEADS // 2
DILATIONS = (1, 4, 16)
ATT_BLOCK = 128
N_BUCKETS = 32
MAX_DISTANCE = 2048
N_EXPERTS = 32
TOP_K = 4
D_FF = 1024
SWIGLU_LIMIT = 7.0
SWIGLU_ALPHA = 1.702
RMS_EPS = 1e-5
NEG_INF = -1e30

LANES = 128
ATT_TILE = 16 * ATT_BLOCK
N_RES = 16
QKV_SLABS = 3 * N_PAIRS
IN_TM = 512
POST_TM = 256
MOE_BM = 256
DISPATCH_TB = 256
COMBINE_TB = 256
VMEM_LIMIT = 56 * 1024 * 1024


def _rms(x, g):
    return x * lax.rsqrt(jnp.mean(x * x, axis=-1, keepdims=True) + RMS_EPS) * g


def _inproj_kernel(x_ref, g_ref, w_ref, u_ref, nat_ref, p16_ref, acc_ref):
    h = _rms(x_ref[...], g_ref[...])
    acc = jnp.dot(h.astype(BF16), w_ref[...], preferred_element_type=F32)
    u_ref[...] = acc[:, :POOL_WIDTH]
    for j in range(QKV_SLABS):
        slab = acc[:, POOL_WIDTH + j * LANES:POOL_WIDTH + (j + 1) * LANES]
        if j < N_PAIRS:
            slab = slab * (HEAD_DIM ** -0.5)
        acc_ref[j] = slab
        nat_ref[j] = slab.astype(BF16)
    for j in range(QKV_SLABS):
        for r in range(N_RES):
            p16_ref[j, r] = acc_ref[j, pl.ds(r, IN_TM // N_RES, stride=N_RES), :].astype(BF16)


def _inproj(x, g, w_bf, S):
    sub = ATT_TILE // IN_TM
    mrows = IN_TM // N_RES
    return pl.pallas_call(
        _inproj_kernel,
        grid=(S // IN_TM,),
        in_specs=[
            pl.BlockSpec((IN_TM, D_MODEL), lambda i: (i, 0)),
            pl.BlockSpec((1, D_MODEL), lambda i: (0, 0)),
            pl.BlockSpec((D_MODEL, POOL_WIDTH + 3 * ATT_WIDTH), lambda i: (0, 0)),
        ],
        out_specs=[
            pl.BlockSpec((IN_TM, POOL_WIDTH), lambda i: (i, 0)),
            pl.BlockSpec((QKV_SLABS, IN_TM, LANES), lambda i: (0, i, 0)),
            pl.BlockSpec((QKV_SLABS, None, N_RES, mrows, LANES),
                         lambda i: (0, i // sub, 0, i % sub, 0)),
        ],
        out_shape=[
            jax.ShapeDtypeStruct((S, POOL_WIDTH), F32),
            jax.ShapeDtypeStruct((QKV_SLABS, S, LANES), BF16),
            jax.ShapeDtypeStruct((QKV_SLABS, S // ATT_TILE, N_RES, ATT_BLOCK, LANES), BF16),
        ],
        scratch_shapes=[pltpu.VMEM((QKV_SLABS, IN_TM, LANES), F32)],
        compiler_params=pltpu.CompilerParams(
            dimension_semantics=("arbitrary",), vmem_limit_bytes=VMEM_LIMIT),
        name="inproj",
    )(x, g, w_bf)


def _t5_bucket(dist):
    max_exact = N_BUCKETS // 2
    is_small = dist < max_exact
    nf = jnp.maximum(dist, 1).astype(jnp.float32)
    large = max_exact + (jnp.log(nf / max_exact) / math.log(MAX_DISTANCE / max_exact)
                         * (N_BUCKETS - max_exact)).astype(jnp.int32)
    large = jnp.minimum(large, N_BUCKETS - 1)
    return jnp.where(is_small, dist, large)


def _bias_tables(rel_bias):
    qi = jnp.arange(ATT_BLOCK)[:, None]
    kj = jnp.arange(2 * ATT_BLOCK)[None, :]
    dist_std = qi + ATT_BLOCK - kj
    prev_std = jnp.broadcast_to(kj < ATT_BLOCK, dist_std.shape)
    qc, qm = qi // 32, qi % 32
    kc, km = kj // 64, kj % 64
    dist_p4 = 4 * (qm - km + 32) + (qc - kc)
    prev_p4 = jnp.broadcast_to(km < 32, dist_p4.shape)
    tabs = []
    for first in (False, True):
        per_branch = []
        for dist, prev, dil in ((dist_std, prev_std, 1), (dist_p4, prev_p4, 4),
                                (dist_std, prev_std, 16)):
            ok = (dist >= 0) & (dist <= ATT_BLOCK)
            if first:
                ok = ok & ~prev
            b = rel_bias[_t5_bucket(jnp.clip(dist, 0) * dil)]
            b = b.transpose(2, 0, 1).astype(F32)
            per_branch.append(jnp.where(ok[None], b, NEG_INF))
        tabs.append(jnp.stack(per_branch, axis=0))
    return jnp.stack(tabs, axis=0)


def _attn_kernel(qn_ref, knp_ref, kn_ref, vnp_ref, vn_ref,
                 qp_ref, kpp_ref, kp_ref, vpp_ref, vp_ref, bias_ref, o_ref,
                 knat, vnat, kext, vext,
                 acc1, m1, l1, acc4, m4, l4, acc16, m16, l16, tacc, tm, tl):
    i = pl.program_id(0)
    at_start = i == 0
    lane = lax.broadcasted_iota(jnp.int32, (ATT_BLOCK, LANES), 1)
    lo = lane < HEAD_DIM
    head_mask = (jnp.where(lo, 1.0, 0.0).astype(BF16), jnp.where(lo, 0.0, 1.0).astype(BF16))

    knat[0:ATT_BLOCK] = knp_ref[...]
    knat[ATT_BLOCK:] = kn_ref[...]
    vnat[0:ATT_BLOCK] = vnp_ref[...]
    vnat[ATT_BLOCK:] = vn_ref[...]
    kext[:, 0:32] = kpp_ref[:, ATT_BLOCK - 32:ATT_BLOCK]
    kext[:, 32:] = kp_ref[...]
    vext[:, 0:32] = vpp_ref[:, ATT_BLOCK - 32:ATT_BLOCK]
    vext[:, 32:] = vp_ref[...]

    def unit(q, k, v, variant, branch):
        res = []
        for a in (0, 1):
            qa = q * head_mask[a]
            s = lax.dot_general(qa, k, (((1,), (1,)), ((), ())), preferred_element_type=F32)
            s = s + bias_ref[variant, branch, a]
            m = jnp.max(s, axis=-1, keepdims=True)
            p = jnp.exp(s - m)
            l = jnp.sum(p, axis=-1, keepdims=True)
            o = jnp.dot(p.astype(BF16), v, preferred_element_type=F32)
            res.append((o, m, l))
        (o0, m0, l0), (o1, m1_, l1_) = res
        return (jnp.where(lo, o0, o1),
                jnp.where(lo, jnp.broadcast_to(m0, o0.shape), jnp.broadcast_to(m1_, o0.shape)),
                jnp.where(lo, jnp.broadcast_to(l0, o0.shape), jnp.broadcast_to(l1_, o0.shape)))

    def branch1(b, carry):
        r0 = pl.multiple_of(b * ATT_BLOCK, ATT_BLOCK)
        q = qn_ref[pl.ds(r0, ATT_BLOCK), :]
        k = knat[pl.ds(r0, 2 * ATT_BLOCK), :]
        v = vnat[pl.ds(r0, 2 * ATT_BLOCK), :]
        variant = jnp.logical_and(at_start, b == 0).astype(jnp.int32)
        o, m, l = unit(q, k, v, variant, 0)
        acc1[pl.ds(r0, ATT_BLOCK), :] = o
        m1[pl.ds(r0, ATT_BLOCK), :] = m
        l1[pl.ds(r0, ATT_BLOCK), :] = l
        return carry

    def branch16(r, carry):
        q = qp_ref[r]
        k = jnp.concatenate([kpp_ref[r], kp_ref[r]], axis=0)
        v = jnp.concatenate([vpp_ref[r], vp_ref[r]], axis=0)
        o, m, l = unit(q, k, v, at_start.astype(jnp.int32), 2)
        r0 = pl.multiple_of(r * ATT_BLOCK, ATT_BLOCK)
        acc16[pl.ds(r0, ATT_BLOCK), :] = o
        m16[pl.ds(r0, ATT_BLOCK), :] = m
        l16[pl.ds(r0, ATT_BLOCK), :] = l
        return carry

    def branch4(u, carry):
        r4 = u // 4
        a = u % 4
        m0 = pl.multiple_of(a * 32, 32)
        q = jnp.concatenate([qp_ref[r4 + 4 * c, pl.ds(m0, 32), :] for c in range(4)], axis=0)
        k = jnp.concatenate([kext[r4 + 4 * c, pl.ds(m0, 64), :] for c in range(4)], axis=0)
        v = jnp.concatenate([vext[r4 + 4 * c, pl.ds(m0, 64), :] for c in range(4)], axis=0)
        variant = jnp.logical_and(at_start, a == 0).astype(jnp.int32)
        o, m, l = unit(q, k, v, variant, 1)
        for c in range(4):
            d0 = pl.multiple_of((r4 + 4 * c) * ATT_BLOCK + m0, 32)
            acc4[pl.ds(d0, 32), :] = o[32 * c:32 * (c + 1)]
            m4[pl.ds(d0, 32), :] = m[32 * c:32 * (c + 1)]
            l4[pl.ds(d0, 32), :] = l[32 * c:32 * (c + 1)]
        return carry

    lax.fori_loop(0, ATT_TILE // ATT_BLOCK, branch1, 0)
    lax.fori_loop(0, N_RES, branch16, 0)
    lax.fori_loop(0, N_RES, branch4, 0)

    for r in range(N_RES):
        sl = slice(r * ATT_BLOCK, (r + 1) * ATT_BLOCK)
        ma, mb = m4[sl], m16[sl]
        mm = jnp.maximum(ma, mb)
        wa, wb = jnp.exp(ma - mm), jnp.exp(mb - mm)
        dst = pl.ds(r, ATT_BLOCK, stride=N_RES)
        tacc[dst, :] = wa * acc4[sl] + wb * acc16[sl]
        tl[dst, :] = wa * l4[sl] + wb * l16[sl]
        tm[dst, :] = mm

    def final(b, carry):
        r0 = pl.multiple_of(b * ATT_BLOCK, ATT_BLOCK)
        sl = pl.ds(r0, ATT_BLOCK)
        ma, mb = m1[sl, :], tm[sl, :]
        mm = jnp.maximum(ma, mb)
        wa, wb = jnp.exp(ma - mm), jnp.exp(mb - mm)
        num = wa * acc1[sl, :] + wb * tacc[sl, :]
        den = wa * l1[sl, :] + wb * tl[sl, :]
        o_ref[sl, :] = (num / den).astype(o_ref.dtype)
        return carry

    lax.fori_loop(0, ATT_TILE // ATT_BLOCK, final, 0)


def _attention(nat, p16, bias, S):
    n_tiles = S // ATT_TILE
    blocks_per_tile = ATT_TILE // ATT_BLOCK
    nat_cur = lambda off: pl.BlockSpec((None, ATT_TILE, LANES), lambda i, p: (off + p, i, 0))
    nat_prev = lambda off: pl.BlockSpec(
        (None, ATT_BLOCK, LANES), lambda i, p: (off + p, jnp.maximum(i * blocks_per_tile - 1, 0), 0))
    p16_cur = lambda off: pl.BlockSpec(
        (None, None, N_RES, ATT_BLOCK, LANES), lambda i, p: (off + p, i, 0, 0, 0))
    p16_prev = lambda off: pl.BlockSpec(
        (None, None, N_RES, ATT_BLOCK, LANES), lambda i, p: (off + p, jnp.maximum(i - 1, 0), 0, 0, 0))
    big = lambda: pltpu.VMEM((ATT_TILE, LANES), F32)
    return pl.pallas_call(
        _attn_kernel,
        grid=(n_tiles, N_PAIRS),
        in_specs=[
            nat_cur(0), nat_prev(N_PAIRS), nat_cur(N_PAIRS), nat_prev(2 * N_PAIRS), nat_cur(2 * N_PAIRS),
            p16_cur(0), p16_prev(N_PAIRS), p16_cur(N_PAIRS), p16_prev(2 * N_PAIRS), p16_cur(2 * N_PAIRS),
            pl.BlockSpec((2, 3, 2, ATT_BLOCK, 2 * ATT_BLOCK), lambda i, p: (0, 0, p, 0, 0)),
        ],
        out_specs=pl.BlockSpec((None, ATT_TILE, LANES), lambda i, p: (p, i, 0)),
        out_shape=jax.ShapeDtypeStruct((N_PAIRS, S, LANES), BF16),
        scratch_shapes=[
            pltpu.VMEM((ATT_TILE + ATT_BLOCK, LANES), BF16),
            pltpu.VMEM((ATT_TILE + ATT_BLOCK, LANES), BF16),
            pltpu.VMEM((N_RES, ATT_BLOCK + 32, LANES), BF16),
            pltpu.VMEM((N_RES, ATT_BLOCK + 32, LANES), BF16),
        ] + [big() for _ in range(12)],
        compiler_params=pltpu.CompilerParams(
            dimension_semantics=("arbitrary", "arbitrary"), vmem_limit_bytes=VMEM_LIMIT),
        name="attention",
    )(nat, nat, nat, nat, nat, p16, p16, p16, p16, p16, bias)


def _post_kernel(x_ref, u_ref, up_ref, att_ref, pw_ref, ps_ref, wo_ref, g2_ref, rw_ref, rb_ref,
                 x1_ref, h2_ref, topi_ref, gate_ref, rank_ref, cnt_ref, ue, carry):
    i = pl.program_id(0)
    tm = POST_TM
    maxw = max(POOL_WINDOWS)

    @pl.when(i == 0)
    def _():
        carry[...] = jnp.zeros_like(carry)

    hist = up_ref[...]
    ue[0:maxw] = jnp.where(i == 0, jnp.zeros_like(hist), hist)
    ue[maxw:] = u_ref[...]
    t_glob = i * tm + lax.broadcasted_iota(jnp.int32, (tm, LANES), 0)
    lane = lax.broadcasted_iota(jnp.int32, (tm, LANES), 1)
    halves = []
    for half, (w_lo, w_hi) in enumerate(((2, 4), (8, 16))):
        cols = slice(half * LANES, (half + 1) * LANES)
        s = ue[maxw:, cols]
        tok = s
        s_lo = None
        for j in range(1, w_hi):
            s = s + ue[maxw - j:maxw - j + tm, cols]
            if j == w_lo - 1:
                s_lo = s
        cnt_lo = jnp.minimum(t_glob + 1, w_lo).astype(F32)
        cnt_hi = jnp.minimum(t_glob + 1, w_hi).astype(F32)
        halves.append(jnp.where(lane < POOL_GROUP, s_lo / cnt_lo, s / cnt_hi) - tok)
    pooled = jnp.concatenate(halves, axis=-1)
    mixed = jnp.dot(pooled.astype(BF16), pw_ref[...], preferred_element_type=F32) * ps_ref[...]

    mix = jnp.concatenate([mixed.astype(BF16)] + [att_ref[j] for j in range(N_PAIRS)], axis=-1)
    x1 = x_ref[...] + jnp.dot(mix, wo_ref[...], preferred_element_type=F32)
    x1_ref[...] = x1
    h2 = _rms(x1, g2_ref[...])
    h2_ref[...] = h2

    logits = jnp.dot(h2, rw_ref[...], preferred_element_type=F32,
                     precision=lax.Precision.HIGHEST) + rb_ref[...]
    eidx = lax.broadcasted_iota(jnp.int32, (tm, N_EXPERTS), 1)
    vals = logits
    top_v, top_i = [], []
    for _ in range(TOP_K):
        m = jnp.max(vals, axis=-1, keepdims=True)
        idx = jnp.min(jnp.where(vals == m, eidx, N_EXPERTS), axis=-1, keepdims=True)
        top_v.append(m)
        top_i.append(idx)
        vals = jnp.where(eidx == idx, -jnp.inf, vals)
    ex = [jnp.exp(v - top_v[0]) for v in top_v]
    den = ex[0] + ex[1] + ex[2] + ex[3]

    hot = [(eidx == idx) for idx in top_i]
    cnt = sum(h.astype(F32) for h in hot)
    row = lax.broadcasted_iota(jnp.int32, (tm, tm), 0)
    col = lax.broadcasted_iota(jnp.int32, (tm, tm), 1)
    strict = jnp.where(col < row, 1.0, 0.0).astype(BF16)
    before = jnp.dot(strict, cnt.astype(BF16), preferred_element_type=F32) + carry[...]
    ranks = [jnp.sum(jnp.where(h, before, 0.0), axis=-1, keepdims=True) for h in hot]
    carry[...] = carry[...] + jnp.sum(cnt, axis=0, keepdims=True)
    cnt_ref[...] = carry[...].astype(jnp.int32)

    k4 = lax.broadcasted_iota(jnp.int32, (tm, TOP_K), 1)

    def pack(cols):
        out = jnp.broadcast_to(cols[0], (tm, TOP_K))
        for k in range(1, TOP_K):
            out = jnp.where(k4 == k, jnp.broadcast_to(cols[k], (tm, TOP_K)), out)
        return out

    topi_ref[...] = pack(top_i)
    gate_ref[...] = pack([e / den for e in ex])
    rank_ref[...] = pack(ranks).astype(jnp.int32)


def _post(x, u, att, pw_bd, pool_scale, wo_bf, g2, router_w, router_b, S):
    tm = POST_TM
    maxw = max(POOL_WINDOWS)
    row = lambda w: pl.BlockSpec((tm, w), lambda i: (i, 0))
    const = lambda a, b: pl.BlockSpec((a, b), lambda i: (0, 0))
    return pl.pallas_call(
        _post_kernel,
        grid=(S // tm,),
        in_specs=[
            row(D_MODEL),
            row(POOL_WIDTH),
            pl.BlockSpec((maxw, POOL_WIDTH), lambda i: (jnp.maximum(i * (tm // maxw) - 1, 0), 0)),
            pl.BlockSpec((N_PAIRS, tm, LANES), lambda i: (0, i, 0)),
            const(POOL_WIDTH, POOL_WIDTH),
            const(1, POOL_WIDTH),
            const(D_MODEL, D_MODEL),
            const(1, D_MODEL),
            const(D_MODEL, N_EXPERTS),
            const(1, N_EXPERTS),
        ],
        out_specs=[row(D_MODEL), row(D_MODEL), row(TOP_K), row(TOP_K), row(TOP_K),
                   const(1, N_EXPERTS)],
        out_shape=[
            jax.ShapeDtypeStruct((S, D_MODEL), F32),
            jax.ShapeDtypeStruct((S, D_MODEL), F32),
            jax.ShapeDtypeStruct((S, TOP_K), jnp.int32),
            jax.ShapeDtypeStruct((S, TOP_K), F32),
            jax.ShapeDtypeStruct((S, TOP_K), jnp.int32),
            jax.ShapeDtypeStruct((1, N_EXPERTS), jnp.int32),
        ],
        scratch_shapes=[pltpu.VMEM((tm + maxw, POOL_WIDTH), F32),
                        pltpu.VMEM((1, N_EXPERTS), F32)],
        compiler_params=pltpu.CompilerParams(
            dimension_semantics=("arbitrary",), vmem_limit_bytes=VMEM_LIMIT),
        name="post_attention",
    )(x, u, u, att, pw_bd, pool_scale, wo_bf, g2, router_w, router_b)


def _dispatch_kernel(topi_ref, rank_ref, pstart_ref, padlo_ref, padhi_ref, nblk_ref,
                     h2_hbm, xb_hbm, sem):
    i = pl.program_id(0)
    tb = DISPATCH_TB

    def row_copy(src_row, dst_row):
        return pltpu.make_async_copy(h2_hbm.at[pl.ds(src_row, 1)], xb_hbm.at[pl.ds(dst_row, 1)], sem)

    def issue(j, carry):
        t = i * tb + j
        for k in range(TOP_K):
            a = t * TOP_K + k
            row_copy(t, pstart_ref[topi_ref[a]] + rank_ref[a]).start()
        return carry

    lax.fori_loop(0, tb, issue, 0)

    def drain(j, carry):
        row_copy(0, 0).wait()
        return carry

    lax.fori_loop(0, tb * TOP_K, drain, 0)

    @pl.when(i == pl.num_programs(0) - 1)
    def _():
        def per_expert(e, carry):
            lo, hi = padlo_ref[e], padhi_ref[e]

            def fill(s, c):
                row_copy(0, s).start()
                return c

            lax.fori_loop(lo, hi, fill, 0)
            lax.fori_loop(lo, hi, drain, 0)
            return carry

        lax.fori_loop(0, N_EXPERTS, per_expert, 0)

        def block_copy(b):
            return pltpu.make_async_copy(h2_hbm.at[pl.ds(0, MOE_BM)],
                                         xb_hbm.at[pl.ds(b * MOE_BM, MOE_BM)], sem)

        def fill_block(b, c):
            block_copy(b).start()
            return c

        def drain_block(b, c):
            block_copy(b).wait()
            return c

        n_blocks = xb_hbm.shape[0] // MOE_BM
        lax.fori_loop(nblk_ref[0], n_blocks, fill_block, 0)
        lax.fori_loop(nblk_ref[0], n_blocks, drain_block, 0)


def _dispatch(topi_flat, rank_flat, pstart, padlo, padhi, nblk, h2, n_rows):
    N = h2.shape[0]
    return pl.pallas_call(
        _dispatch_kernel,
        grid_spec=pltpu.PrefetchScalarGridSpec(
            num_scalar_prefetch=6,
            grid=(N // DISPATCH_TB,),
            in_specs=[pl.BlockSpec(memory_space=pl.ANY)],
            out_specs=pl.BlockSpec(memory_space=pl.ANY),
            scratch_shapes=[pltpu.SemaphoreType.DMA(())],
        ),
        out_shape=jax.ShapeDtypeStruct((n_rows, D_MODEL), F32),
        compiler_params=pltpu.CompilerParams(
            dimension_semantics=("arbitrary",), has_side_effects=True),
        name="moe_dispatch",
    )(topi_flat, rank_flat, pstart, padlo, padhi, nblk, h2)


def _expert_kernel(bexp_ref, nblk_ref, xb_ref, wgu_ref, bgu_ref, wd_ref, bd_ref, y_ref,
                   wgu_bf, wd_bf):
    b = pl.program_id(0)
    prev = bexp_ref[jnp.maximum(b - 1, 0)]
    fresh = jnp.logical_or(b == 0, bexp_ref[b] != prev)

    @pl.when(jnp.logical_and(fresh, b < nblk_ref[0]))
    def _():
        wgu_bf[...] = wgu_ref[...].astype(BF16)
        wd_bf[...] = wd_ref[...].astype(BF16)

    @pl.when(b < nblk_ref[0])
    def _():
        x = xb_ref[...].astype(BF16)
        gu = jnp.dot(x, wgu_bf[...], preferred_element_type=F32) + bgu_ref[...]
        gate = jnp.minimum(gu[:, :D_FF], SWIGLU_LIMIT)
        up = jnp.clip(gu[:, D_FF:], -SWIGLU_LIMIT, SWIGLU_LIMIT)
        act = (up + 1.0) * (gate * jax.nn.sigmoid(gate * SWIGLU_ALPHA))
        y_ref[...] = jnp.dot(act.astype(BF16), wd_bf[...], preferred_element_type=F32) + bd_ref[...]

    @pl.when(b >= nblk_ref[0])
    def _():
        y_ref[...] = jnp.zeros_like(y_ref)


def _experts(block_exp, nblk, xb, w_gate_up, b_gate_up, w_down, b_down):
    n_rows = xb.shape[0]
    nb = n_rows // MOE_BM
    blk = lambda b, be, nk: (jnp.minimum(b, nk[0] - 1), 0)
    wsel = lambda b, be, nk: (be[jnp.minimum(b, nk[0] - 1)], 0, 0)
    return pl.pallas_call(
        _expert_kernel,
        grid_spec=pltpu.PrefetchScalarGridSpec(
            num_scalar_prefetch=2,
            grid=(nb,),
            in_specs=[
                pl.BlockSpec((MOE_BM, D_MODEL), blk),
                pl.BlockSpec((None, D_MODEL, 2 * D_FF), wsel),
                pl.BlockSpec((None, 1, 2 * D_FF), wsel),
                pl.BlockSpec((None, D_FF, D_MODEL), wsel),
                pl.BlockSpec((None, 1, D_MODEL), wsel),
            ],
            out_specs=pl.BlockSpec((MOE_BM, D_MODEL), lambda b, be, nk: (b, 0)),
            scratch_shapes=[pltpu.VMEM((D_MODEL, 2 * D_FF), BF16),
                            pltpu.VMEM((D_FF, D_MODEL), BF16)],
        ),
        out_shape=jax.ShapeDtypeStruct((n_rows, D_MODEL), F32),
        compiler_params=pltpu.CompilerParams(
            dimension_semantics=("arbitrary",), vmem_limit_bytes=VMEM_LIMIT),
        name="moe_experts",
    )(block_exp, nblk, xb, w_gate_up, b_gate_up, w_down, b_down)


def _combine_kernel(topi_ref, rank_ref, pstart_ref, x1_ref, gate_ref, g_ref, y_hbm, o_ref,
                    ybuf, sem):
    i = pl.program_id(0)
    tb = COMBINE_TB

    def row_copy(src_row, k, j):
        return pltpu.make_async_copy(y_hbm.at[pl.ds(src_row, 1)], ybuf.at[k, pl.ds(j, 1)], sem)

    def issue(j, carry):
        t = i * tb + j
        for k in range(TOP_K):
            a = t * TOP_K + k
            row_copy(pstart_ref[topi_ref[a]] + rank_ref[a], k, j).start()
        return carry

    lax.fori_loop(0, tb, issue, 0)

    def drain(j, carry):
        row_copy(0, 0, 0).wait()
        return carry

    lax.fori_loop(0, tb * TOP_K, drain, 0)

    gates = gate_ref[...]
    moe = gates[:, 0:1] * ybuf[0]
    for k in range(1, TOP_K):
        moe = moe + gates[:, k:k + 1] * ybuf[k]
    o_ref[...] = _rms(x1_ref[...] + moe, g_ref[...])


def _combine(topi_flat, rank_flat, pstart, x1, gates, final_g, y):
    N = x1.shape[0]
    tb = COMBINE_TB
    return pl.pallas_call(
        _combine_kernel,
        grid_spec=pltpu.PrefetchScalarGridSpec(
            num_scalar_prefetch=3,
            grid=(N // tb,),
            in_specs=[
                pl.BlockSpec((tb, D_MODEL), lambda i, *_: (i, 0)),
                pl.BlockSpec((tb, TOP_K), lambda i, *_: (i, 0)),
                pl.BlockSpec((1, D_MODEL), lambda i, *_: (0, 0)),
                pl.BlockSpec(memory_space=pl.ANY),
            ],
            out_specs=pl.BlockSpec((tb, D_MODEL), lambda i, *_: (i, 0)),
            scratch_shapes=[pltpu.VMEM((TOP_K, tb, D_MODEL), F32),
                            pltpu.SemaphoreType.DMA(())],
        ),
        out_shape=jax.ShapeDtypeStruct((N, D_MODEL), F32),
        compiler_params=pltpu.CompilerParams(
            dimension_semantics=("arbitrary",), vmem_limit_bytes=VMEM_LIMIT),
        name="moe_combine",
    )(topi_flat, rank_flat, pstart, x1, gates, final_g, y)


def _layer(x2, norm1_g, w_in, pool_w, pool_scale, rel_bias, w_out, norm2_g,
           router_w, router_b, w_gate_up, b_gate_up, w_down, b_down, final_g):
    S = x2.shape[0]
    u, nat, p16 = _inproj(x2, norm1_g.reshape(1, D_MODEL), w_in.astype(BF16), S)
    att = _attention(nat, p16, _bias_tables(rel_bias), S)

    n_g = len(POOL_WINDOWS)
    pw_bd = (jnp.eye(n_g, dtype=F32)[:, None, :, None] * pool_w[:, :, None, :]).reshape(
        POOL_WIDTH, POOL_WIDTH).astype(BF16)
    x1, h2, topi, gates, rank, counts = _post(
        x2, u, att, pw_bd, pool_scale.reshape(1, POOL_WIDTH), w_out.astype(BF16),
        norm2_g.reshape(1, D_MODEL), router_w, router_b.reshape(1, N_EXPERTS), S)

    counts = counts.reshape(N_EXPERTS)
    blocks_per = (counts + MOE_BM - 1) // MOE_BM
    block_end = jnp.cumsum(blocks_per)
    pstart = ((block_end - blocks_per) * MOE_BM).astype(jnp.int32)
    nb = (S * TOP_K) // MOE_BM + N_EXPERTS
    block_exp = jnp.minimum(
        jnp.searchsorted(block_end, jnp.arange(nb, dtype=jnp.int32), side='right'),
        N_EXPERTS - 1).astype(jnp.int32)
    nblk = block_end[-1:].astype(jnp.int32)
    padlo = (pstart + counts).astype(jnp.int32)
    padhi = (pstart + blocks_per * MOE_BM).astype(jnp.int32)

    topi_flat = topi.reshape(S * TOP_K)
    rank_flat = rank.reshape(S * TOP_K)
    xb = _dispatch(topi_flat, rank_flat, pstart, padlo, padhi, nblk, h2, nb * MOE_BM)
    y = _experts(block_exp, nblk, xb, w_gate_up, b_gate_up.reshape(N_EXPERTS, 1, 2 * D_FF),
                 w_down, b_down.reshape(N_EXPERTS, 1, D_MODEL))
    return _combine(topi_flat, rank_flat, pstart, x1, gates, final_g.reshape(1, D_MODEL), y)


def kernel(x, norm1_g, w_in, pool_w, pool_scale, rel_bias, w_out, norm2_g, router_w, router_b,
           w_gate_up, b_gate_up, w_down, b_down, final_g):
    B, S, D = x.shape
    assert B == 1 and D == D_MODEL and S % ATT_TILE == 0, x.shape
    out = _layer(x.reshape(B * S, D), norm1_g[0], w_in[0], pool_w[0], pool_scale[0], rel_bias,
                 w_out[0], norm2_g[0], router_w[0], router_b[0], w_gate_up[0], b_gate_up[0],
                 w_down[0], b_down[0], final_g)
    return out.reshape(B, S, D)
```

```python
import math

import jax
import jax.numpy as jnp
from jax import lax
from jax.experimental import pallas as pl
from jax.experimental.pallas import tpu as pltpu

F32 = jnp.float32
BF16 = jnp.bfloat16

D_MODEL = 1024
POOL_WIDTH = 256
POOL_WINDOWS = (2, 4, 8, 16)
POOL_GROUP = 64
ATT_WIDTH = 768
HEAD_DIM = 64
N_HEADS = 12
N_PAIRS = N_HEADS // 2
DILATIONS = (1, 4, 16)
ATT_BLOCK = 128
N_BUCKETS = 32
MAX_DISTANCE = 2048
N_EXPERTS = 32
TOP_K = 4
D_FF = 1024
SWIGLU_LIMIT = 7.0
SWIGLU_ALPHA = 1.702
RMS_EPS = 1e-5
NEG_INF = -1e30
LOG2E = math.log2(math.e)

LANES = 128
ATT_TILE = 16 * ATT_BLOCK
BLOCKS_PER_TILE = ATT_TILE // ATT_BLOCK
N_UNITS = len(DILATIONS) * BLOCKS_PER_TILE
ATT_GROUP = 4
QKV_SLABS = 3 * N_PAIRS
IN_TM = 512
POST_TM = 256
MOE_BM = 256
DISPATCH_TB = 512
COMBINE_TB = 256
ROW_UNROLL = 8
VMEM_LIMIT = 56 * 1024 * 1024

_CLASS_ROWS = tuple(ATT_TILE // d for d in DILATIONS)
_SEG_ROWS = tuple(ATT_BLOCK + r for r in _CLASS_ROWS)
_SEG_BASE = (0, _SEG_ROWS[0], _SEG_ROWS[0] + DILATIONS[1] * _SEG_ROWS[1])
KV_ROWS = _SEG_BASE[2] + DILATIONS[2] * _SEG_ROWS[2]


def _rms(x, g):
    return x * lax.rsqrt(jnp.mean(x * x, axis=-1, keepdims=True) + RMS_EPS) * g


def _inproj_kernel(x_ref, g_ref, w_ref, u_ref, nat_ref, p4_ref, p16_ref, acc_ref):
    h = _rms(x_ref[...], g_ref[...])
    acc = jnp.dot(h.astype(BF16), w_ref[...], preferred_element_type=F32)
    u_ref[...] = acc[:, :POOL_WIDTH]
    for j in range(QKV_SLABS):
        slab = acc[:, POOL_WIDTH + j * LANES:POOL_WIDTH + (j + 1) * LANES]
        if j < N_PAIRS:
            slab = slab * (HEAD_DIM ** -0.5 * LOG2E)
        acc_ref[j] = slab
        nat_ref[j] = slab.astype(BF16)
    for j in range(QKV_SLABS):
        for d, ref in ((DILATIONS[1], p4_ref), (DILATIONS[2], p16_ref)):
            for r in range(d):
                ref[j, r] = acc_ref[j, pl.ds(r, IN_TM // d, stride=d), :].astype(BF16)


def _inproj(x, g, w_bf, S):
    sub = ATT_TILE // IN_TM
    n_tiles = S // ATT_TILE

    def res_spec(d):
        return pl.BlockSpec((QKV_SLABS, None, d, IN_TM // d, LANES),
                            lambda i: (0, i // sub, 0, i % sub, 0))

    def res_shape(d):
        return jax.ShapeDtypeStruct((QKV_SLABS, n_tiles, d, ATT_TILE // d, LANES), BF16)

    return pl.pallas_call(
        _inproj_kernel,
        grid=(S // IN_TM,),
        in_specs=[
            pl.BlockSpec((IN_TM, D_MODEL), lambda i: (i, 0)),
            pl.BlockSpec((1, D_MODEL), lambda i: (0, 0)),
            pl.BlockSpec((D_MODEL, POOL_WIDTH + 3 * ATT_WIDTH), lambda i: (0, 0)),
        ],
        out_specs=[
            pl.BlockSpec((IN_TM, POOL_WIDTH), lambda i: (i, 0)),
            pl.BlockSpec((QKV_SLABS, IN_TM, LANES), lambda i: (0, i, 0)),
            res_spec(DILATIONS[1]),
            res_spec(DILATIONS[2]),
        ],
        out_shape=[
            jax.ShapeDtypeStruct((S, POOL_WIDTH), F32),
            jax.ShapeDtypeStruct((QKV_SLABS, S, LANES), BF16),
            res_shape(DILATIONS[1]),
            res_shape(DILATIONS[2]),
        ],
        scratch_shapes=[pltpu.VMEM((QKV_SLABS, IN_TM, LANES), F32)],
        compiler_params=pltpu.CompilerParams(
            dimension_semantics=("arbitrary",), vmem_limit_bytes=VMEM_LIMIT),
        name="inproj",
    )(x, g, w_bf)


def _t5_bucket(dist):
    max_exact = N_BUCKETS // 2
    is_small = dist < max_exact
    nf = jnp.maximum(dist, 1).astype(jnp.float32)
    large = max_exact + (jnp.log(nf / max_exact) / math.log(MAX_DISTANCE / max_exact)
                         * (N_BUCKETS - max_exact)).astype(jnp.int32)
    large = jnp.minimum(large, N_BUCKETS - 1)
    return jnp.where(is_small, dist, large)


def _bias_kernel(ids_ref, rb_ref, o_ref):
    h = pl.program_id(0)
    col = lax.broadcasted_iota(jnp.int32, (ATT_BLOCK, 2 * ATT_BLOCK), 1)
    for b in range(len(DILATIONS)):
        ids = ids_ref[b]
        tab = jnp.full(ids.shape, NEG_INF, F32)
        for k in range(N_BUCKETS):
            tab = jnp.where(ids == k, rb_ref[k, h] * LOG2E, tab)
        o_ref[0, b] = tab
        o_ref[1, b] = jnp.where(col < ATT_BLOCK, NEG_INF, tab)


def _bias_tables(rel_bias):
    qi = jnp.arange(ATT_BLOCK)[:, None]
    kj = jnp.arange(2 * ATT_BLOCK)[None, :]
    dist = qi + ATT_BLOCK - kj
    ok = (dist >= 0) & (dist <= ATT_BLOCK)
    ids = jnp.stack([jnp.where(ok, _t5_bucket(jnp.clip(dist, 0) * d), -1) for d in DILATIONS])
    return pl.pallas_call(
        _bias_kernel,
        grid=(N_HEADS,),
        in_specs=[
            pl.BlockSpec((len(DILATIONS), ATT_BLOCK, 2 * ATT_BLOCK), lambda h: (0, 0, 0)),
            pl.BlockSpec(memory_space=pltpu.SMEM),
        ],
        out_specs=pl.BlockSpec((2, len(DILATIONS), None, ATT_BLOCK, 2 * ATT_BLOCK),
                               lambda h: (0, 0, h, 0, 0)),
        out_shape=jax.ShapeDtypeStruct(
            (2, len(DILATIONS), N_HEADS, ATT_BLOCK, 2 * ATT_BLOCK), F32),
        compiler_params=pltpu.CompilerParams(dimension_semantics=("arbitrary",)),
        name="bias_tables",
    )(ids.astype(jnp.int32), rel_bias.astype(F32))


def _attn_kernel(qn, knp, kn, vnp, vn, q4, k4p, k4, v4p, v4, q16, k16p, k16, v16p, v16,
                 bias_ref, o_ref, qbuf, kbuf, vbuf, sbuf, pbuf, obuf, mbuf, lbuf, tbuf):
    i = pl.program_id(0)
    at_start = i == 0
    G = ATT_GROUP
    n_groups = N_UNITS // G
    lane = lax.broadcasted_iota(jnp.int32, (ATT_BLOCK, LANES), 1)
    lo = lane < HEAD_DIM
    head_mask = (jnp.where(lo, 1.0, 0.0).astype(BF16), jnp.where(lo, 0.0, 1.0).astype(BF16))

    @pl.when(jnp.logical_and(at_start, pl.program_id(1) == 0))
    def _():
        sbuf[...] = jnp.zeros_like(sbuf)
        pbuf[...] = jnp.zeros_like(pbuf)

    for b, q in enumerate((qn, q4, q16)):
        qbuf[b * ATT_TILE:(b + 1) * ATT_TILE] = q[...]
    for buf, prevs, curs in ((kbuf, (knp, k4p, k16p), (kn, k4, k16)),
                             (vbuf, (vnp, v4p, v16p), (vn, v4, v16))):
        buf[0:ATT_BLOCK] = prevs[0][...]
        buf[ATT_BLOCK:_SEG_ROWS[0]] = curs[0][...]
        for b in (1, 2):
            rows = _CLASS_ROWS[b]
            for c in range(DILATIONS[b]):
                base = _SEG_BASE[b] + c * _SEG_ROWS[b]
                buf[base:base + ATT_BLOCK] = prevs[b][c]
                buf[base + ATT_BLOCK:base + _SEG_ROWS[b]] = curs[b][c * rows:(c + 1) * rows]

    def geometry(u):
        b = u // BLOCKS_PER_TILE
        w = u % BLOCKS_PER_TILE
        sh = 4 - 2 * b
        c = lax.shift_right_logical(w, sh)
        a = w - lax.shift_left(c, sh)
        seg = ATT_BLOCK + lax.shift_right_logical(ATT_TILE, 2 * b)
        base = b * _SEG_BASE[1] + jnp.where(b == 2, _SEG_BASE[2] - 2 * _SEG_BASE[1], 0)
        krow = pl.multiple_of(base + c * seg + a * ATT_BLOCK, ATT_BLOCK)
        variant = jnp.logical_and(at_start, a == 0).astype(jnp.int32)
        return b, krow, variant

    def scores(g, slot):
        for j in range(G):
            u = g * G + j
            b, krow, variant = geometry(u)
            q = qbuf[pl.ds(pl.multiple_of(u * ATT_BLOCK, ATT_BLOCK), ATT_BLOCK), :]
            k = kbuf[pl.ds(krow, 2 * ATT_BLOCK), :]
            for a in (0, 1):
                s = lax.dot_general(q * head_mask[a], k, (((1,), (1,)), ((), ())),
                                    preferred_element_type=F32)
                sbuf[slot, j, a] = s + bias_ref[variant, b, a]

    def softmax(g, slot):
        for j in range(G):
            rows = pl.ds(pl.multiple_of((g * G + j) * ATT_BLOCK, ATT_BLOCK), ATT_BLOCK)
            stats = []
            for a in (0, 1):
                s = sbuf[slot, j, a]
                m = jnp.max(s, axis=-1, keepdims=True)
                p = jnp.exp2(s - m)
                pbuf[slot, j, a] = p.astype(BF16)
                stats.append((m, jnp.sum(p, axis=-1, keepdims=True)))
            shape = (ATT_BLOCK, LANES)
            mbuf[rows, :] = jnp.where(lo, jnp.broadcast_to(stats[0][0], shape),
                                      jnp.broadcast_to(stats[1][0], shape))
            lbuf[rows, :] = jnp.where(lo, jnp.broadcast_to(stats[0][1], shape),
                                      jnp.broadcast_to(stats[1][1], shape))

    def values(g, slot):
        for j in range(G):
            u = g * G + j
            _, krow, _ = geometry(u)
            v = vbuf[pl.ds(krow, 2 * ATT_BLOCK), :]
            o0 = jnp.dot(pbuf[slot, j, 0], v, preferred_element_type=F32)
            o1 = jnp.dot(pbuf[slot, j, 1], v, preferred_element_type=F32)
            obuf[pl.ds(pl.multiple_of(u * ATT_BLOCK, ATT_BLOCK), ATT_BLOCK), :] = jnp.where(lo, o0, o1)

    def trip(t, carry):
        clamp = lambda g: jnp.clip(g, 0, n_groups - 1)
        values(clamp(t - 2), t % 2)
        scores(clamp(t), t % 2)
        softmax(clamp(t - 1), (t + 1) % 2)
        return carry

    lax.fori_loop(0, n_groups + 2, trip, 0)

    for b in (1, 2):
        d, rows = DILATIONS[b], _CLASS_ROWS[b]
        for r in range(d):
            src = slice(b * ATT_TILE + r * rows, b * ATT_TILE + (r + 1) * rows)
            dst = pl.ds(r, rows, stride=d)
            for n, buf in enumerate((obuf, mbuf, lbuf)):
                tbuf[3 * (b - 1) + n, dst, :] = buf[src]

    def merge(c, carry):
        rows = pl.ds(pl.multiple_of(c * ATT_BLOCK, ATT_BLOCK), ATT_BLOCK)
        ms = (mbuf[rows, :], tbuf[1, rows, :], tbuf[4, rows, :])
        os_ = (obuf[rows, :], tbuf[0, rows, :], tbuf[3, rows, :])
        ls = (lbuf[rows, :], tbuf[2, rows, :], tbuf[5, rows, :])
        mm = jnp.maximum(jnp.maximum(ms[0], ms[1]), ms[2])
        ws = [jnp.exp2(m - mm) for m in ms]
        num = ws[0] * os_[0] + ws[1] * os_[1] + ws[2] * os_[2]
        den = ws[0] * ls[0] + ws[1] * ls[1] + ws[2] * ls[2]
        o_ref[rows, :] = (num / den).astype(o_ref.dtype)
        return carry

    lax.fori_loop(0, BLOCKS_PER_TILE, merge, 0)


def _attention(nat, p4, p16, bias, S):
    n_tiles = S // ATT_TILE
    d4, d16 = DILATIONS[1], DILATIONS[2]
    flat4 = p4.reshape(QKV_SLABS, S, LANES)
    flat16 = p16.reshape(QKV_SLABS, S, LANES)
    blk4 = p4.reshape(QKV_SLABS, n_tiles, d4, _CLASS_ROWS[1] // ATT_BLOCK, ATT_BLOCK, LANES)
    prev = lambda i: jnp.maximum(i - 1, 0)

    def cur(off):
        return pl.BlockSpec((None, ATT_TILE, LANES), lambda i, p: (off + p, i, 0))

    def prev_nat(off):
        return pl.BlockSpec((None, ATT_BLOCK, LANES),
                            lambda i, p: (off + p, jnp.maximum(i * BLOCKS_PER_TILE - 1, 0), 0))

    def prev4(off):
        last = _CLASS_ROWS[1] // ATT_BLOCK - 1
        return pl.BlockSpec((None, None, d4, None, ATT_BLOCK, LANES),
                            lambda i, p: (off + p, prev(i), 0, last, 0, 0))

    def prev16(off):
        return pl.BlockSpec((None, None, d16, ATT_BLOCK, LANES),
                            lambda i, p: (off + p, prev(i), 0, 0, 0))

    k_off, v_off = N_PAIRS, 2 * N_PAIRS
    return pl.pallas_call(
        _attn_kernel,
        grid=(n_tiles, N_PAIRS),
        in_specs=[
            cur(0), prev_nat(k_off), cur(k_off), prev_nat(v_off), cur(v_off),
            cur(0), prev4(k_off), cur(k_off), prev4(v_off), cur(v_off),
            cur(0), prev16(k_off), cur(k_off), prev16(v_off), cur(v_off),
            pl.BlockSpec((2, len(DILATIONS), 2, ATT_BLOCK, 2 * ATT_BLOCK),
                         lambda i, p: (0, 0, p, 0, 0)),
        ],
        out_specs=pl.BlockSpec((None, ATT_TILE, LANES), lambda i, p: (p, i, 0)),
        out_shape=jax.ShapeDtypeStruct((N_PAIRS, S, LANES), BF16),
        scratch_shapes=[
            pltpu.VMEM((N_UNITS * ATT_BLOCK, LANES), BF16),
            pltpu.VMEM((KV_ROWS, LANES), BF16),
            pltpu.VMEM((KV_ROWS, LANES), BF16),
            pltpu.VMEM((2, ATT_GROUP, 2, ATT_BLOCK, 2 * ATT_BLOCK), F32),
            pltpu.VMEM((2, ATT_GROUP, 2, ATT_BLOCK, 2 * ATT_BLOCK), BF16),
            pltpu.VMEM((N_UNITS * ATT_BLOCK, LANES), F32),
            pltpu.VMEM((N_UNITS * ATT_BLOCK, LANES), F32),
            pltpu.VMEM((N_UNITS * ATT_BLOCK, LANES), F32),
            pltpu.VMEM((6, ATT_TILE, LANES), F32),
        ],
        compiler_params=pltpu.CompilerParams(
            dimension_semantics=("arbitrary", "arbitrary"), vmem_limit_bytes=VMEM_LIMIT),
        name="attention",
    )(nat, nat, nat, nat, nat, flat4, blk4, flat4, blk4, flat4,
      flat16, p16, flat16, p16, flat16, bias)


def _post_kernel(x_ref, u_ref, up_ref, att_ref, pw_ref, ps_ref, wo_ref, g2_ref, rw_ref, rb_ref,
                 x1_ref, h2_ref, topi_ref, gate_ref, rank_ref, cnt_ref, ue, carry):
    i = pl.program_id(0)
    tm = POST_TM
    maxw = max(POOL_WINDOWS)

    @pl.when(i == 0)
    def _():
        carry[...] = jnp.zeros_like(carry)

    hist = up_ref[...]
    ue[0:maxw] = jnp.where(i == 0, jnp.zeros_like(hist), hist)
    ue[maxw:] = u_ref[...]
    t_glob = i * tm + lax.broadcasted_iota(jnp.int32, (tm, LANES), 0)
    lane = lax.broadcasted_iota(jnp.int32, (tm, LANES), 1)
    halves = []
    for half, (w_lo, w_hi) in enumerate(((2, 4), (8, 16))):
        cols = slice(half * LANES, (half + 1) * LANES)
        s = ue[maxw:, cols]
        tok = s
        s_lo = None
        for j in range(1, w_hi):
            s = s + ue[maxw - j:maxw - j + tm, cols]
            if j == w_lo - 1:
                s_lo = s
        cnt_lo = jnp.minimum(t_glob + 1, w_lo).astype(F32)
        cnt_hi = jnp.minimum(t_glob + 1, w_hi).astype(F32)
        halves.append(jnp.where(lane < POOL_GROUP, s_lo / cnt_lo, s / cnt_hi) - tok)
    pooled = jnp.concatenate(halves, axis=-1)
    mixed = jnp.dot(pooled.astype(BF16), pw_ref[...], preferred_element_type=F32) * ps_ref[...]

    mix = jnp.concatenate([mixed.astype(BF16)] + [att_ref[j] for j in range(N_PAIRS)], axis=-1)
    x1 = x_ref[...] + jnp.dot(mix, wo_ref[...], preferred_element_type=F32)
    x1_ref[...] = x1
    h2 = _rms(x1, g2_ref[...])
    h2_ref[...] = h2

    logits = jnp.dot(h2, rw_ref[...], preferred_element_type=F32,
                     precision=lax.Precision.HIGHEST) + rb_ref[...]
    eidx = lax.broadcasted_iota(jnp.int32, (tm, N_EXPERTS), 1)
    vals = logits
    top_v, top_i = [], []
    for _ in range(TOP_K):
        m = jnp.max(vals, axis=-1, keepdims=True)
        idx = jnp.min(jnp.where(vals == m, eidx, N_EXPERTS), axis=-1, keepdims=True)
        top_v.append(m)
        top_i.append(idx)
        vals = jnp.where(eidx == idx, -jnp.inf, vals)
    ex = [jnp.exp(v - top_v[0]) for v in top_v]
    den = ex[0] + ex[1] + ex[2] + ex[3]

    hot = [(eidx == idx) for idx in top_i]
    cnt = sum(h.astype(F32) for h in hot)
    row = lax.broadcasted_iota(jnp.int32, (tm, tm), 0)
    col = lax.broadcasted_iota(jnp.int32, (tm, tm), 1)
    strict = jnp.where(col < row, 1.0, 0.0).astype(BF16)
    before = jnp.dot(strict, cnt.astype(BF16), preferred_element_type=F32) + carry[...]
    ranks = [jnp.sum(jnp.where(h, before, 0.0), axis=-1, keepdims=True) for h in hot]
    carry[...] = carry[...] + jnp.sum(cnt, axis=0, keepdims=True)
    cnt_ref[...] = carry[...].astype(jnp.int32)

    k4 = lax.broadcasted_iota(jnp.int32, (tm, TOP_K), 1)

    def pack(cols):
        out = jnp.broadcast_to(cols[0], (tm, TOP_K))
        for k in range(1, TOP_K):
            out = jnp.where(k4 == k, jnp.broadcast_to(cols[k], (tm, TOP_K)), out)
        return out

    topi_ref[...] = pack(top_i)
    gate_ref[...] = pack([e / den for e in ex])
    rank_ref[...] = pack(ranks).astype(jnp.int32)


def _post(x, u, att, pw_bd, pool_scale, wo_bf, g2, router_w, router_b, S):
    tm = POST_TM
    maxw = max(POOL_WINDOWS)
    row = lambda w: pl.BlockSpec((tm, w), lambda i: (i, 0))
    const = lambda a, b: pl.BlockSpec((a, b), lambda i: (0, 0))
    return pl.pallas_call(
        _post_kernel,
        grid=(S // tm,),
        in_specs=[
            row(D_MODEL),
            row(POOL_WIDTH),
            pl.BlockSpec((maxw, POOL_WIDTH), lambda i: (jnp.maximum(i * (tm // maxw) - 1, 0), 0)),
            pl.BlockSpec((N_PAIRS, tm, LANES), lambda i: (0, i, 0)),
            const(POOL_WIDTH, POOL_WIDTH),
            const(1, POOL_WIDTH),
            const(D_MODEL, D_MODEL),
            const(1, D_MODEL),
            const(D_MODEL, N_EXPERTS),
            const(1, N_EXPERTS),
        ],
        out_specs=[row(D_MODEL), row(D_MODEL), row(TOP_K), row(TOP_K), row(TOP_K),
                   const(1, N_EXPERTS)],
        out_shape=[
            jax.ShapeDtypeStruct((S, D_MODEL), F32),
            jax.ShapeDtypeStruct((S, D_MODEL), F32),
            jax.ShapeDtypeStruct((S, TOP_K), jnp.int32),
            jax.ShapeDtypeStruct((S, TOP_K), F32),
            jax.ShapeDtypeStruct((S, TOP_K), jnp.int32),
            jax.ShapeDtypeStruct((1, N_EXPERTS), jnp.int32),
        ],
        scratch_shapes=[pltpu.VMEM((tm + maxw, POOL_WIDTH), F32),
                        pltpu.VMEM((1, N_EXPERTS), F32)],
        compiler_params=pltpu.CompilerParams(
            dimension_semantics=("arbitrary",), vmem_limit_bytes=VMEM_LIMIT),
        name="post_attention",
    )(x, u, u, att, pw_bd, pool_scale, wo_bf, g2, router_w, router_b)


def _expert_kernel(bexp_ref, nblk_ref, xb_ref, wgu_ref, bgu_ref, wd_ref, bd_ref, y_ref,
                   wgu_bf, wd_bf):
    b = pl.program_id(0)
    prev = bexp_ref[jnp.maximum(b - 1, 0)]
    fresh = jnp.logical_or(b == 0, bexp_ref[b] != prev)

    @pl.when(jnp.logical_and(fresh, b < nblk_ref[0]))
    def _():
        wgu_bf[...] = wgu_ref[...].astype(BF16)
        wd_bf[...] = wd_ref[...].astype(BF16)

    @pl.when(b < nblk_ref[0])
    def _():
        x = xb_ref[...].astype(BF16)
        gu = jnp.dot(x, wgu_bf[...], preferred_element_type=F32) + bgu_ref[...]
        gate = jnp.minimum(gu[:, :D_FF], SWIGLU_LIMIT)
        up = jnp.clip(gu[:, D_FF:], -SWIGLU_LIMIT, SWIGLU_LIMIT)
        act = (up + 1.0) * (gate * jax.nn.sigmoid(gate * SWIGLU_ALPHA))
        y_ref[...] = jnp.dot(act.astype(BF16), wd_bf[...], preferred_element_type=F32) + bd_ref[...]

    @pl.when(b >= nblk_ref[0])
    def _():
        y_ref[...] = jnp.zeros_like(y_ref)


def _experts(block_exp, nblk, xb, w_gate_up, b_gate_up, w_down, b_down):
    n_rows = xb.shape[0]
    nb = n_rows // MOE_BM
    blk = lambda b, be, nk: (jnp.minimum(b, nk[0] - 1), 0)
    wsel = lambda b, be, nk: (be[jnp.minimum(b, nk[0] - 1)], 0, 0)
    return pl.pallas_call(
        _expert_kernel,
        grid_spec=pltpu.PrefetchScalarGridSpec(
            num_scalar_prefetch=2,
            grid=(nb,),
            in_specs=[
                pl.BlockSpec((MOE_BM, D_MODEL), blk),
                pl.BlockSpec((None, D_MODEL, 2 * D_FF), wsel),
                pl.BlockSpec((None, 1, 2 * D_FF), wsel),
                pl.BlockSpec((None, D_FF, D_MODEL), wsel),
                pl.BlockSpec((None, 1, D_MODEL), wsel),
            ],
            out_specs=pl.BlockSpec((MOE_BM, D_MODEL), lambda b, be, nk: (b, 0)),
            scratch_shapes=[pltpu.VMEM((D_MODEL, 2 * D_FF), BF16),
                            pltpu.VMEM((D_FF, D_MODEL), BF16)],
        ),
        out_shape=jax.ShapeDtypeStruct((n_rows, D_MODEL), F32),
        compiler_params=pltpu.CompilerParams(
            dimension_semantics=("arbitrary",), vmem_limit_bytes=VMEM_LIMIT),
        name="moe_experts",
    )(block_exp, nblk, xb, w_gate_up, b_gate_up, w_down, b_down)


def _drain_rows(wait_one, n):
    def body(j, carry):
        for _ in range(ROW_UNROLL):
            wait_one()
        return carry

    lax.fori_loop(0, n // ROW_UNROLL, body, 0)


def _dispatch_kernel(dest_ref, padlo_ref, padhi_ref, nblk_ref, h2_ref, xb_hbm, sem):
    i = pl.program_id(0)
    tb = DISPATCH_TB

    def row_copy(j, dst_row):
        return pltpu.make_async_copy(h2_ref.at[pl.ds(j, 1)], xb_hbm.at[pl.ds(dst_row, 1)], sem)

    def issue(jo, carry):
        for ji in range(ROW_UNROLL):
            j = jo * ROW_UNROLL + ji
            for k in range(TOP_K):
                row_copy(j, dest_ref[(i * tb + j) * TOP_K + k]).start()
        return carry

    lax.fori_loop(0, tb // ROW_UNROLL, issue, 0)
    _drain_rows(lambda: row_copy(0, 0).wait(), tb * TOP_K)

    @pl.when(i == pl.num_programs(0) - 1)
    def _():
        def per_expert(e, carry):
            lo, hi = padlo_ref[e], padhi_ref[e]

            def fill(s, c):
                row_copy(0, s).start()
                return c

            def drain(s, c):
                row_copy(0, 0).wait()
                return c

            lax.fori_loop(lo, hi, fill, 0)
            lax.fori_loop(lo, hi, drain, 0)
            return carry

        lax.fori_loop(0, N_EXPERTS, per_expert, 0)

        def block_copy(b):
            return pltpu.make_async_copy(h2_ref.at[pl.ds(0, MOE_BM)],
                                         xb_hbm.at[pl.ds(b * MOE_BM, MOE_BM)], sem)

        def fill_block(b, c):
            block_copy(b).start()
            return c

        def drain_block(b, c):
            block_copy(b).wait()
            return c

        n_blocks = xb_hbm.shape[0] // MOE_BM
        lax.fori_loop(nblk_ref[0], n_blocks, fill_block, 0)
        lax.fori_loop(nblk_ref[0], n_blocks, drain_block, 0)


def _dispatch(dest_flat, padlo, padhi, nblk, h2, n_rows):
    N = h2.shape[0]
    assert DISPATCH_TB >= MOE_BM and DISPATCH_TB % ROW_UNROLL == 0
    return pl.pallas_call(
        _dispatch_kernel,
        grid_spec=pltpu.PrefetchScalarGridSpec(
            num_scalar_prefetch=4,
            grid=(N // DISPATCH_TB,),
            in_specs=[pl.BlockSpec((DISPATCH_TB, D_MODEL), lambda i, *_: (i, 0))],
            out_specs=pl.BlockSpec(memory_space=pl.ANY),
            scratch_shapes=[pltpu.SemaphoreType.DMA(())],
        ),
        out_shape=jax.ShapeDtypeStruct((n_rows, D_MODEL), F32),
        compiler_params=pltpu.CompilerParams(
            dimension_semantics=("arbitrary",), has_side_effects=True,
            vmem_limit_bytes=VMEM_LIMIT),
        name="moe_dispatch",
    )(dest_flat, padlo, padhi, nblk, h2)


def _combine_kernel(dest_ref, x1_ref, gate_ref, g_ref, y_hbm, o_ref, ybuf, sems):
    i = pl.program_id(0)
    n = pl.num_programs(0)
    tb = COMBINE_TB

    def row_copy(src_row, slot, k, j):
        return pltpu.make_async_copy(y_hbm.at[pl.ds(src_row, 1)],
                                     ybuf.at[slot, k, pl.ds(j, 1)], sems.at[slot])

    def gather(step, slot):
        def issue(jo, carry):
            for ji in range(ROW_UNROLL):
                j = jo * ROW_UNROLL + ji
                for k in range(TOP_K):
                    row_copy(dest_ref[(step * tb + j) * TOP_K + k], slot, k, j).start()
            return carry

        lax.fori_loop(0, tb // ROW_UNROLL, issue, 0)

    slot = i % 2

    @pl.when(i == 0)
    def _():
        gather(0, 0)

    @pl.when(i + 1 < n)
    def _():
        gather(i + 1, 1 - slot)

    _drain_rows(lambda: row_copy(0, slot, 0, 0).wait(), tb * TOP_K)

    gates = gate_ref[...]
    moe = gates[:, 0:1] * ybuf[slot, 0]
    for k in range(1, TOP_K):
        moe = moe + gates[:, k:k + 1] * ybuf[slot, k]
    o_ref[...] = _rms(x1_ref[...] + moe, g_ref[...])


def _combine(dest_flat, x1, gates, final_g, y):
    N = x1.shape[0]
    tb = COMBINE_TB
    return pl.pallas_call(
        _combine_kernel,
        grid_spec=pltpu.PrefetchScalarGridSpec(
            num_scalar_prefetch=1,
            grid=(N // tb,),
            in_specs=[
                pl.BlockSpec((tb, D_MODEL), lambda i, *_: (i, 0)),
                pl.BlockSpec((tb, TOP_K), lambda i, *_: (i, 0)),
                pl.BlockSpec((1, D_MODEL), lambda i, *_: (0, 0)),
                pl.BlockSpec(memory_space=pl.ANY),
            ],
            out_specs=pl.BlockSpec((tb, D_MODEL), lambda i, *_: (i, 0)),
            scratch_shapes=[pltpu.VMEM((2, TOP_K, tb, D_MODEL), F32),
                            pltpu.SemaphoreType.DMA((2,))],
        ),
        out_shape=jax.ShapeDtypeStruct((N, D_MODEL), F32),
        compiler_params=pltpu.CompilerParams(
            dimension_semantics=("arbitrary",), vmem_limit_bytes=VMEM_LIMIT),
        name="moe_combine",
    )(dest_flat, x1, gates, final_g, y)


def _layer(x2, norm1_g, w_in, pool_w, pool_scale, rel_bias, w_out, norm2_g,
           router_w, router_b, w_gate_up, b_gate_up, w_down, b_down, final_g):
    S = x2.shape[0]
    u, nat, p4, p16 = _inproj(x2, norm1_g.reshape(1, D_MODEL), w_in.astype(BF16), S)
    att = _attention(nat, p4, p16, _bias_tables(rel_bias), S)

    n_g = len(POOL_WINDOWS)
    pw_bd = (jnp.eye(n_g, dtype=F32)[:, None, :, None] * pool_w[:, :, None, :]).reshape(
        POOL_WIDTH, POOL_WIDTH).astype(BF16)
    x1, h2, topi, gates, rank, counts = _post(
        x2, u, att, pw_bd, pool_scale.reshape(1, POOL_WIDTH), w_out.astype(BF16),
        norm2_g.reshape(1, D_MODEL), router_w, router_b.reshape(1, N_EXPERTS), S)

    counts = counts.reshape(N_EXPERTS)
    blocks_per = (counts + MOE_BM - 1) // MOE_BM
    block_end = jnp.cumsum(blocks_per)
    pstart = ((block_end - blocks_per) * MOE_BM).astype(jnp.int32)
    nb = (S * TOP_K) // MOE_BM + N_EXPERTS
    block_exp = jnp.minimum(
        jnp.sum(block_end[None, :] <= jnp.arange(nb, dtype=jnp.int32)[:, None], axis=1),
        N_EXPERTS - 1).astype(jnp.int32)
    nblk = block_end[-1:].astype(jnp.int32)
    padlo = (pstart + counts).astype(jnp.int32)
    padhi = (pstart + blocks_per * MOE_BM).astype(jnp.int32)
    expert_ids = jnp.arange(N_EXPERTS, dtype=jnp.int32)
    dest = rank + jnp.sum(jnp.where(topi[:, :, None] == expert_ids, pstart, 0), axis=-1)
    dest_flat = dest.reshape(S * TOP_K).astype(jnp.int32)

    xb = _dispatch(dest_flat, padlo, padhi, nblk, h2, nb * MOE_BM)
    y = _experts(block_exp, nblk, xb, w_gate_up, b_gate_up.reshape(N_EXPERTS, 1, 2 * D_FF),
                 w_down, b_down.reshape(N_EXPERTS, 1, D_MODEL))
    return _combine(dest_flat, x1, gates, final_g.reshape(1, D_MODEL), y)


def kernel(x, norm1_g, w_in, pool_w, pool_scale, rel_bias, w_out, norm2_g, router_w, router_b,
           w_gate_up, b_gate_up, w_down, b_down, final_g):
    B, S, D = x.shape
    assert B == 1 and D == D_MODEL and S % ATT_TILE == 0, x.shape
    out = _layer(x.reshape(B * S, D), norm1_g[0], w_in[0], pool_w[0], pool_scale[0], rel_bias,
                 w_out[0], norm2_g[0], router_w[0], router_b[0], w_gate_up[0], b_gate_up[0],
                 w_down[0], b_down[0], final_g)
    return out.reshape(B, S, D)
```

```python
import math

import jax
import jax.numpy as jnp
from jax import lax
from jax.experimental import pallas as pl
from jax.experimental.pallas import tpu as pltpu

F32 = jnp.float32
BF16 = jnp.bfloat16

D_MODEL = 1024
POOL_WIDTH = 256
POOL_WINDOWS = (2, 4, 8, 16)
POOL_GROUP = 64
ATT_WIDTH = 768
HEAD_DIM = 64
N_HEADS = 12
N_PAIRS = N_HEADS // 2
DILATIONS = (1, 4, 16)
ATT_BLOCK = 128
N_BUCKETS = 32
MAX_DISTANCE = 2048
N_EXPERTS = 32
TOP_K = 4
D_FF = 1024
SWIGLU_LIMIT = 7.0
SWIGLU_ALPHA = 1.702
RMS_EPS = 1e-5
NEG_INF = -1e30
LOG2E = math.log2(math.e)

LANES = 128
ATT_TILE = 16 * ATT_BLOCK
BLOCKS_PER_TILE = ATT_TILE // ATT_BLOCK
N_UNITS = len(DILATIONS) * BLOCKS_PER_TILE
ATT_GROUP = 4
QKV_SLABS = 3 * N_PAIRS
IN_TM = 512
POST_TM = 256
MOE_BM = 256
DISPATCH_TB = 512
COMBINE_TB = 256
ROW_UNROLL = 8
VMEM_LIMIT = 56 * 1024 * 1024

_CLASS_ROWS = tuple(ATT_TILE // d for d in DILATIONS)
_SEG_ROWS = tuple(ATT_BLOCK + r for r in _CLASS_ROWS)
_SEG_BASE = (0, _SEG_ROWS[0], _SEG_ROWS[0] + DILATIONS[1] * _SEG_ROWS[1])
KV_ROWS = _SEG_BASE[2] + DILATIONS[2] * _SEG_ROWS[2]


def _rms(x, g):
    return x * lax.rsqrt(jnp.mean(x * x, axis=-1, keepdims=True) + RMS_EPS) * g


ROW_TILE = D_MODEL // LANES


def _store_row_tiles(ref, x):
    n = x.shape[0]
    for c in range(ROW_TILE):
        ref[pl.ds(c, n, stride=ROW_TILE), :] = x[:, c * LANES:(c + 1) * LANES]


def _load_row_tiles(ref, n):
    return jnp.concatenate(
        [ref[pl.ds(c, n, stride=ROW_TILE), :] for c in range(ROW_TILE)], axis=-1)


def _inproj_kernel(x_ref, g_ref, w_ref, u_ref, nat_ref, p4_ref, p16_ref, acc_ref):
    h = _rms(x_ref[...], g_ref[...])
    acc = jnp.dot(h.astype(BF16), w_ref[...], preferred_element_type=F32)
    u_ref[...] = acc[:, :POOL_WIDTH]
    for j in range(QKV_SLABS):
        slab = acc[:, POOL_WIDTH + j * LANES:POOL_WIDTH + (j + 1) * LANES]
        if j < N_PAIRS:
            slab = slab * (HEAD_DIM ** -0.5 * LOG2E)
        acc_ref[j] = slab
        nat_ref[j] = slab.astype(BF16)
    for j in range(QKV_SLABS):
        for d, ref in ((DILATIONS[1], p4_ref), (DILATIONS[2], p16_ref)):
            for r in range(d):
                ref[j, r] = acc_ref[j, pl.ds(r, IN_TM // d, stride=d), :].astype(BF16)


def _inproj(x, g, w_bf, S):
    sub = ATT_TILE // IN_TM
    n_tiles = S // ATT_TILE

    def res_spec(d):
        return pl.BlockSpec((QKV_SLABS, None, d, IN_TM // d, LANES),
                            lambda i: (0, i // sub, 0, i % sub, 0))

    def res_shape(d):
        return jax.ShapeDtypeStruct((QKV_SLABS, n_tiles, d, ATT_TILE // d, LANES), BF16)

    return pl.pallas_call(
        _inproj_kernel,
        grid=(S // IN_TM,),
        in_specs=[
            pl.BlockSpec((IN_TM, D_MODEL), lambda i: (i, 0)),
            pl.BlockSpec((1, D_MODEL), lambda i: (0, 0)),
            pl.BlockSpec((D_MODEL, POOL_WIDTH + 3 * ATT_WIDTH), lambda i: (0, 0)),
        ],
        out_specs=[
            pl.BlockSpec((IN_TM, POOL_WIDTH), lambda i: (i, 0)),
            pl.BlockSpec((QKV_SLABS, IN_TM, LANES), lambda i: (0, i, 0)),
            res_spec(DILATIONS[1]),
            res_spec(DILATIONS[2]),
        ],
        out_shape=[
            jax.ShapeDtypeStruct((S, POOL_WIDTH), F32),
            jax.ShapeDtypeStruct((QKV_SLABS, S, LANES), BF16),
            res_shape(DILATIONS[1]),
            res_shape(DILATIONS[2]),
        ],
        scratch_shapes=[pltpu.VMEM((QKV_SLABS, IN_TM, LANES), F32)],
        compiler_params=pltpu.CompilerParams(
            dimension_semantics=("arbitrary",), vmem_limit_bytes=VMEM_LIMIT),
        name="inproj",
    )(x, g, w_bf)


def _t5_bucket(dist):
    max_exact = N_BUCKETS // 2
    is_small = dist < max_exact
    nf = jnp.maximum(dist, 1).astype(jnp.float32)
    large = max_exact + (jnp.log(nf / max_exact) / math.log(MAX_DISTANCE / max_exact)
                         * (N_BUCKETS - max_exact)).astype(jnp.int32)
    large = jnp.minimum(large, N_BUCKETS - 1)
    return jnp.where(is_small, dist, large)


def _bias_kernel(ids_ref, rb_ref, o_ref):
    h = pl.program_id(0)
    col = lax.broadcasted_iota(jnp.int32, (ATT_BLOCK, 2 * ATT_BLOCK), 1)
    for b in range(len(DILATIONS)):
        ids = ids_ref[b]
        tab = jnp.full(ids.shape, NEG_INF, F32)
        for k in range(N_BUCKETS):
            tab = jnp.where(ids == k, rb_ref[k, h] * LOG2E, tab)
        o_ref[0, b] = tab
        o_ref[1, b] = jnp.where(col < ATT_BLOCK, NEG_INF, tab)


def _bias_tables(rel_bias):
    qi = jnp.arange(ATT_BLOCK)[:, None]
    kj = jnp.arange(2 * ATT_BLOCK)[None, :]
    dist = qi + ATT_BLOCK - kj
    ok = (dist >= 0) & (dist <= ATT_BLOCK)
    ids = jnp.stack([jnp.where(ok, _t5_bucket(jnp.clip(dist, 0) * d), -1) for d in DILATIONS])
    return pl.pallas_call(
        _bias_kernel,
        grid=(N_HEADS,),
        in_specs=[
            pl.BlockSpec((len(DILATIONS), ATT_BLOCK, 2 * ATT_BLOCK), lambda h: (0, 0, 0)),
            pl.BlockSpec(memory_space=pltpu.SMEM),
        ],
        out_specs=pl.BlockSpec((2, len(DILATIONS), None, ATT_BLOCK, 2 * ATT_BLOCK),
                               lambda h: (0, 0, h, 0, 0)),
        out_shape=jax.ShapeDtypeStruct(
            (2, len(DILATIONS), N_HEADS, ATT_BLOCK, 2 * ATT_BLOCK), F32),
        compiler_params=pltpu.CompilerParams(dimension_semantics=("arbitrary",)),
        name="bias_tables",
    )(ids.astype(jnp.int32), rel_bias.astype(F32))


def _attn_kernel(qn, knp, kn, vnp, vn, q4, k4p, k4, v4p, v4, q16, k16p, k16, v16p, v16,
                 bias_ref, o_ref, qbuf, kbuf, vbuf0, vbuf1, sbuf0, sbuf1, pbuf0, pbuf1,
                 obuf, mbuf, lbuf, tbuf):
    i = pl.program_id(0)
    at_start = i == 0
    G = ATT_GROUP
    n_groups = N_UNITS // G
    sbufs, pbufs, vbufs = (sbuf0, sbuf1), (pbuf0, pbuf1), (vbuf0, vbuf1)
    lane = lax.broadcasted_iota(jnp.int32, (ATT_BLOCK, LANES), 1)
    lo = lane < HEAD_DIM
    head_mask = (jnp.where(lo, 1.0, 0.0).astype(BF16), jnp.where(lo, 0.0, 1.0).astype(BF16))

    @pl.when(jnp.logical_and(at_start, pl.program_id(1) == 0))
    def _():
        for buf in sbufs + pbufs:
            buf[...] = jnp.zeros_like(buf)

    for b, q in enumerate((qn, q4, q16)):
        qbuf[b * ATT_TILE:(b + 1) * ATT_TILE] = q[...]

    def fill_k(dst0, rows, val):
        kbuf[dst0:dst0 + rows] = val

    def fill_v(dst0, rows, val):
        for a in (0, 1):
            keep = head_mask[a][0:1]
            vbufs[a][dst0:dst0 + rows] = val * keep + (1 - keep)

    for fill, prevs, curs in ((fill_k, (knp, k4p, k16p), (kn, k4, k16)),
                              (fill_v, (vnp, v4p, v16p), (vn, v4, v16))):
        fill(0, ATT_BLOCK, prevs[0][...])
        fill(ATT_BLOCK, ATT_TILE, curs[0][...])
        for b in (1, 2):
            rows = _CLASS_ROWS[b]
            for c in range(DILATIONS[b]):
                base = _SEG_BASE[b] + c * _SEG_ROWS[b]
                fill(base, ATT_BLOCK, prevs[b][c])
                fill(base + ATT_BLOCK, rows, curs[b][c * rows:(c + 1) * rows])

    def geometry(u):
        b = u // BLOCKS_PER_TILE
        w = u % BLOCKS_PER_TILE
        sh = 4 - 2 * b
        c = lax.shift_right_logical(w, sh)
        a = w - lax.shift_left(c, sh)
        seg = ATT_BLOCK + lax.shift_right_logical(ATT_TILE, 2 * b)
        base = b * _SEG_BASE[1] + jnp.where(b == 2, _SEG_BASE[2] - 2 * _SEG_BASE[1], 0)
        krow = pl.multiple_of(base + c * seg + a * ATT_BLOCK, ATT_BLOCK)
        variant = jnp.logical_and(at_start, a == 0).astype(jnp.int32)
        return b, krow, variant

    def scores(g, sbuf):
        for j in range(G):
            u = g * G + j
            b, krow, variant = geometry(u)
            q = qbuf[pl.ds(pl.multiple_of(u * ATT_BLOCK, ATT_BLOCK), ATT_BLOCK), :]
            k = kbuf[pl.ds(krow, 2 * ATT_BLOCK), :]
            for a in (0, 1):
                s = lax.dot_general(q * head_mask[a], k, (((1,), (1,)), ((), ())),
                                    preferred_element_type=F32)
                sbuf[j, a] = s + bias_ref[variant, b, a]

    def softmax(g, sbuf, pbuf):
        for j in range(G):
            rows = pl.ds(pl.multiple_of((g * G + j) * ATT_BLOCK, ATT_BLOCK), ATT_BLOCK)
            maxes = []
            for a in (0, 1):
                s = sbuf[j, a]
                m = jnp.max(s, axis=-1, keepdims=True)
                pbuf[j, a] = jnp.exp2(s - m).astype(BF16)
                maxes.append(jnp.broadcast_to(m, (ATT_BLOCK, LANES)))
            mbuf[rows, :] = jnp.where(lo, maxes[0], maxes[1])

    def values(g, pbuf):
        for j in range(G):
            u = g * G + j
            _, krow, _ = geometry(u)
            rows = pl.ds(pl.multiple_of(u * ATT_BLOCK, ATT_BLOCK), ATT_BLOCK)
            o0 = jnp.dot(pbuf[j, 0], vbuf0[pl.ds(krow, 2 * ATT_BLOCK), :],
                         preferred_element_type=F32)
            o1 = jnp.dot(pbuf[j, 1], vbuf1[pl.ds(krow, 2 * ATT_BLOCK), :],
                         preferred_element_type=F32)
            obuf[rows, :] = jnp.where(lo, o0, o1)
            lbuf[rows, :] = pltpu.roll(jnp.where(lo, o1, o0), HEAD_DIM, axis=1)

    def trip_pair(tt, carry):
        clamp = lambda g: jnp.clip(g, 0, n_groups - 1)
        for par in (0, 1):
            t = 2 * tt + par
            values(clamp(t - 2), pbufs[par])
            scores(clamp(t), sbufs[par])
            softmax(clamp(t - 1), sbufs[1 - par], pbufs[1 - par])
        return carry

    assert n_groups % 2 == 0
    lax.fori_loop(0, (n_groups + 2) // 2, trip_pair, 0)

    for b in (1, 2):
        d, rows = DILATIONS[b], _CLASS_ROWS[b]
        for r in range(d):
            src = slice(b * ATT_TILE + r * rows, b * ATT_TILE + (r + 1) * rows)
            dst = pl.ds(r, rows, stride=d)
            for n, buf in enumerate((obuf, mbuf, lbuf)):
                tbuf[3 * (b - 1) + n, dst, :] = buf[src]

    def merge(c, carry):
        rows = pl.ds(pl.multiple_of(c * ATT_BLOCK, ATT_BLOCK), ATT_BLOCK)
        ms = (mbuf[rows, :], tbuf[1, rows, :], tbuf[4, rows, :])
        os_ = (obuf[rows, :], tbuf[0, rows, :], tbuf[3, rows, :])
        ls = (lbuf[rows, :], tbuf[2, rows, :], tbuf[5, rows, :])
        mm = jnp.maximum(jnp.maximum(ms[0], ms[1]), ms[2])
        ws = [jnp.exp2(m - mm) for m in ms]
        num = ws[0] * os_[0] + ws[1] * os_[1] + ws[2] * os_[2]
        den = ws[0] * ls[0] + ws[1] * ls[1] + ws[2] * ls[2]
        o_ref[rows, :] = (num / den).astype(o_ref.dtype)
        return carry

    lax.fori_loop(0, BLOCKS_PER_TILE, merge, 0)


def _attention(nat, p4, p16, bias, S):
    n_tiles = S // ATT_TILE
    d4, d16 = DILATIONS[1], DILATIONS[2]
    flat4 = p4.reshape(QKV_SLABS, S, LANES)
    flat16 = p16.reshape(QKV_SLABS, S, LANES)
    blk4 = p4.reshape(QKV_SLABS, n_tiles, d4, _CLASS_ROWS[1] // ATT_BLOCK, ATT_BLOCK, LANES)
    prev = lambda i: jnp.maximum(i - 1, 0)

    def cur(off):
        return pl.BlockSpec((None, ATT_TILE, LANES), lambda i, p: (off + p, i, 0))

    def prev_nat(off):
        return pl.BlockSpec((None, ATT_BLOCK, LANES),
                            lambda i, p: (off + p, jnp.maximum(i * BLOCKS_PER_TILE - 1, 0), 0))

    def prev4(off):
        last = _CLASS_ROWS[1] // ATT_BLOCK - 1
        return pl.BlockSpec((None, None, d4, None, ATT_BLOCK, LANES),
                            lambda i, p: (off + p, prev(i), 0, last, 0, 0))

    def prev16(off):
        return pl.BlockSpec((None, None, d16, ATT_BLOCK, LANES),
                            lambda i, p: (off + p, prev(i), 0, 0, 0))

    k_off, v_off = N_PAIRS, 2 * N_PAIRS
    return pl.pallas_call(
        _attn_kernel,
        grid=(n_tiles, N_PAIRS),
        in_specs=[
            cur(0), prev_nat(k_off), cur(k_off), prev_nat(v_off), cur(v_off),
            cur(0), prev4(k_off), cur(k_off), prev4(v_off), cur(v_off),
            cur(0), prev16(k_off), cur(k_off), prev16(v_off), cur(v_off),
            pl.BlockSpec((2, len(DILATIONS), 2, ATT_BLOCK, 2 * ATT_BLOCK),
                         lambda i, p: (0, 0, p, 0, 0)),
        ],
        out_specs=pl.BlockSpec((None, ATT_TILE, LANES), lambda i, p: (p, i, 0)),
        out_shape=jax.ShapeDtypeStruct((N_PAIRS, S, LANES), BF16),
        scratch_shapes=[
            pltpu.VMEM((N_UNITS * ATT_BLOCK, LANES), BF16),
            pltpu.VMEM((KV_ROWS, LANES), BF16),
            pltpu.VMEM((KV_ROWS, LANES), BF16),
            pltpu.VMEM((KV_ROWS, LANES), BF16),
            pltpu.VMEM((ATT_GROUP, 2, ATT_BLOCK, 2 * ATT_BLOCK), F32),
            pltpu.VMEM((ATT_GROUP, 2, ATT_BLOCK, 2 * ATT_BLOCK), F32),
            pltpu.VMEM((ATT_GROUP, 2, ATT_BLOCK, 2 * ATT_BLOCK), BF16),
            pltpu.VMEM((ATT_GROUP, 2, ATT_BLOCK, 2 * ATT_BLOCK), BF16),
            pltpu.VMEM((N_UNITS * ATT_BLOCK, LANES), F32),
            pltpu.VMEM((N_UNITS * ATT_BLOCK, LANES), F32),
            pltpu.VMEM((N_UNITS * ATT_BLOCK, LANES), F32),
            pltpu.VMEM((6, ATT_TILE, LANES), F32),
        ],
        compiler_params=pltpu.CompilerParams(
            dimension_semantics=("arbitrary", "arbitrary"), vmem_limit_bytes=VMEM_LIMIT),
        name="attention",
    )(nat, nat, nat, nat, nat, flat4, blk4, flat4, blk4, flat4,
      flat16, p16, flat16, p16, flat16, bias)


def _post_kernel(x_ref, u_ref, up_ref, att_ref, pw_ref, ps_ref, wo_ref, g2_ref, rw_ref, rb_ref,
                 x1_ref, h2_ref, topi_ref, gate_ref, rank_ref, cnt_ref, ue, carry):
    i = pl.program_id(0)
    tm = POST_TM
    maxw = max(POOL_WINDOWS)

    @pl.when(i == 0)
    def _():
        carry[...] = jnp.zeros_like(carry)

    hist = up_ref[...]
    ue[0:maxw] = jnp.where(i == 0, jnp.zeros_like(hist), hist)
    ue[maxw:] = u_ref[...]
    t_glob = i * tm + lax.broadcasted_iota(jnp.int32, (tm, LANES), 0)
    lane = lax.broadcasted_iota(jnp.int32, (tm, LANES), 1)
    halves = []
    for half, (w_lo, w_hi) in enumerate(((2, 4), (8, 16))):
        cols = slice(half * LANES, (half + 1) * LANES)
        s = ue[maxw:, cols]
        tok = s
        s_lo = None
        for j in range(1, w_hi):
            s = s + ue[maxw - j:maxw - j + tm, cols]
            if j == w_lo - 1:
                s_lo = s
        cnt_lo = jnp.minimum(t_glob + 1, w_lo).astype(F32)
        cnt_hi = jnp.minimum(t_glob + 1, w_hi).astype(F32)
        halves.append(jnp.where(lane < POOL_GROUP, s_lo / cnt_lo, s / cnt_hi) - tok)
    pooled = jnp.concatenate(halves, axis=-1)
    mixed = jnp.dot(pooled.astype(BF16), pw_ref[...], preferred_element_type=F32) * ps_ref[...]

    mix = jnp.concatenate([mixed.astype(BF16)] + [att_ref[j] for j in range(N_PAIRS)], axis=-1)
    x1 = x_ref[...] + jnp.dot(mix, wo_ref[...], preferred_element_type=F32)
    x1_ref[...] = x1
    h2 = _rms(x1, g2_ref[...])
    _store_row_tiles(h2_ref, h2)

    logits = jnp.dot(h2, rw_ref[...], preferred_element_type=F32,
                     precision=lax.Precision.HIGHEST) + rb_ref[...]
    eidx = lax.broadcasted_iota(jnp.int32, (tm, N_EXPERTS), 1)
    vals = logits
    top_v, top_i = [], []
    for _ in range(TOP_K):
        m = jnp.max(vals, axis=-1, keepdims=True)
        idx = jnp.min(jnp.where(vals == m, eidx, N_EXPERTS), axis=-1, keepdims=True)
        top_v.append(m)
        top_i.append(idx)
        vals = jnp.where(eidx == idx, -jnp.inf, vals)
    ex = [jnp.exp(v - top_v[0]) for v in top_v]
    den = ex[0] + ex[1] + ex[2] + ex[3]

    hot = [(eidx == idx) for idx in top_i]
    cnt = sum(h.astype(F32) for h in hot)
    row = lax.broadcasted_iota(jnp.int32, (tm, tm), 0)
    col = lax.broadcasted_iota(jnp.int32, (tm, tm), 1)
    strict = jnp.where(col < row, 1.0, 0.0).astype(BF16)
    before = jnp.dot(strict, cnt.astype(BF16), preferred_element_type=F32) + carry[...]
    ranks = [jnp.sum(jnp.where(h, before, 0.0), axis=-1, keepdims=True) for h in hot]
    carry[...] = carry[...] + jnp.sum(cnt, axis=0, keepdims=True)
    cnt_ref[...] = carry[...].astype(jnp.int32)

    k4 = lax.broadcasted_iota(jnp.int32, (tm, TOP_K), 1)

    def pack(cols):
        out = jnp.broadcast_to(cols[0], (tm, TOP_K))
        for k in range(1, TOP_K):
            out = jnp.where(k4 == k, jnp.broadcast_to(cols[k], (tm, TOP_K)), out)
        return out

    topi_ref[...] = pack(top_i)
    gate_ref[...] = pack([e / den for e in ex])
    rank_ref[...] = pack(ranks).astype(jnp.int32)


def _post(x, u, att, pw_bd, pool_scale, wo_bf, g2, router_w, router_b, S):
    tm = POST_TM
    maxw = max(POOL_WINDOWS)
    row = lambda w: pl.BlockSpec((tm, w), lambda i: (i, 0))
    const = lambda a, b: pl.BlockSpec((a, b), lambda i: (0, 0))
    return pl.pallas_call(
        _post_kernel,
        grid=(S // tm,),
        in_specs=[
            row(D_MODEL),
            row(POOL_WIDTH),
            pl.BlockSpec((maxw, POOL_WIDTH), lambda i: (jnp.maximum(i * (tm // maxw) - 1, 0), 0)),
            pl.BlockSpec((N_PAIRS, tm, LANES), lambda i: (0, i, 0)),
            const(POOL_WIDTH, POOL_WIDTH),
            const(1, POOL_WIDTH),
            const(D_MODEL, D_MODEL),
            const(1, D_MODEL),
            const(D_MODEL, N_EXPERTS),
            const(1, N_EXPERTS),
        ],
        out_specs=[row(D_MODEL), pl.BlockSpec((tm * ROW_TILE, LANES), lambda i: (i, 0)),
                   row(TOP_K), row(TOP_K), row(TOP_K), const(1, N_EXPERTS)],
        out_shape=[
            jax.ShapeDtypeStruct((S, D_MODEL), F32),
            jax.ShapeDtypeStruct((S * ROW_TILE, LANES), F32),
            jax.ShapeDtypeStruct((S, TOP_K), jnp.int32),
            jax.ShapeDtypeStruct((S, TOP_K), F32),
            jax.ShapeDtypeStruct((S, TOP_K), jnp.int32),
            jax.ShapeDtypeStruct((1, N_EXPERTS), jnp.int32),
        ],
        scratch_shapes=[pltpu.VMEM((tm + maxw, POOL_WIDTH), F32),
                        pltpu.VMEM((1, N_EXPERTS), F32)],
        compiler_params=pltpu.CompilerParams(
            dimension_semantics=("arbitrary",), vmem_limit_bytes=VMEM_LIMIT),
        name="post_attention",
    )(x, u, u, att, pw_bd, pool_scale, wo_bf, g2, router_w, router_b)


def _expert_kernel(bexp_ref, nblk_ref, xb_ref, wgu_ref, bgu_ref, wd_ref, bd_ref, y_ref,
                   wgu_bf, wd_bf):
    b = pl.program_id(0)
    prev = bexp_ref[jnp.maximum(b - 1, 0)]
    fresh = jnp.logical_or(b == 0, bexp_ref[b] != prev)

    @pl.when(jnp.logical_and(fresh, b < nblk_ref[0]))
    def _():
        wgu_bf[...] = wgu_ref[...].astype(BF16)
        wd_bf[...] = wd_ref[...].astype(BF16)

    @pl.when(b < nblk_ref[0])
    def _():
        x = _load_row_tiles(xb_ref, MOE_BM).astype(BF16)
        gu = jnp.dot(x, wgu_bf[...], preferred_element_type=F32) + bgu_ref[...]
        gate = jnp.minimum(gu[:, :D_FF], SWIGLU_LIMIT)
        up = jnp.clip(gu[:, D_FF:], -SWIGLU_LIMIT, SWIGLU_LIMIT)
        act = (up + 1.0) * (gate * jax.nn.sigmoid(gate * SWIGLU_ALPHA))
        y = jnp.dot(act.astype(BF16), wd_bf[...], preferred_element_type=F32) + bd_ref[...]
        _store_row_tiles(y_ref, y)

    @pl.when(b >= nblk_ref[0])
    def _():
        y_ref[...] = jnp.zeros_like(y_ref)


def _experts(block_exp, nblk, xb, w_gate_up, b_gate_up, w_down, b_down):
    nb = xb.shape[0] // (MOE_BM * ROW_TILE)
    tiles = (MOE_BM * ROW_TILE, LANES)
    blk = lambda b, be, nk: (jnp.minimum(b, nk[0] - 1), 0)
    wsel = lambda b, be, nk: (be[jnp.minimum(b, nk[0] - 1)], 0, 0)
    return pl.pallas_call(
        _expert_kernel,
        grid_spec=pltpu.PrefetchScalarGridSpec(
            num_scalar_prefetch=2,
            grid=(nb,),
            in_specs=[
                pl.BlockSpec(tiles, blk),
                pl.BlockSpec((None, D_MODEL, 2 * D_FF), wsel),
                pl.BlockSpec((None, 1, 2 * D_FF), wsel),
                pl.BlockSpec((None, D_FF, D_MODEL), wsel),
                pl.BlockSpec((None, 1, D_MODEL), wsel),
            ],
            out_specs=pl.BlockSpec(tiles, lambda b, be, nk: (b, 0)),
            scratch_shapes=[pltpu.VMEM((D_MODEL, 2 * D_FF), BF16),
                            pltpu.VMEM((D_FF, D_MODEL), BF16)],
        ),
        out_shape=jax.ShapeDtypeStruct(xb.shape, F32),
        compiler_params=pltpu.CompilerParams(
            dimension_semantics=("arbitrary",), vmem_limit_bytes=VMEM_LIMIT),
        name="moe_experts",
    )(block_exp, nblk, xb, w_gate_up, b_gate_up, w_down, b_down)


def _drain_rows(wait_one, n):
    def body(j, carry):
        for _ in range(ROW_UNROLL):
            wait_one()
        return carry

    lax.fori_loop(0, n // ROW_UNROLL, body, 0)


def _dispatch_kernel(dest_ref, padlo_ref, padhi_ref, nblk_ref, h2_ref, xb_hbm, sem):
    i = pl.program_id(0)
    tb = DISPATCH_TB

    def row_copy(j, dst_row):
        src = h2_ref.at[pl.ds(pl.multiple_of(j * ROW_TILE, ROW_TILE), ROW_TILE)]
        dst = xb_hbm.at[pl.ds(pl.multiple_of(dst_row * ROW_TILE, ROW_TILE), ROW_TILE)]
        return pltpu.make_async_copy(src, dst, sem)

    def issue(jo, carry):
        for ji in range(ROW_UNROLL):
            j = jo * ROW_UNROLL + ji
            for k in range(TOP_K):
                row_copy(j, dest_ref[(i * tb + j) * TOP_K + k]).start(priority=k % 2)
        return carry

    lax.fori_loop(0, tb // ROW_UNROLL, issue, 0)
    _drain_rows(lambda: row_copy(0, 0).wait(), tb * TOP_K)

    @pl.when(i == pl.num_programs(0) - 1)
    def _():
        def per_expert(e, carry):
            lo, hi = padlo_ref[e], padhi_ref[e]

            def fill(s, c):
                row_copy(0, s).start()
                return c

            def drain(s, c):
                row_copy(0, 0).wait()
                return c

            lax.fori_loop(lo, hi, fill, 0)
            lax.fori_loop(lo, hi, drain, 0)
            return carry

        lax.fori_loop(0, N_EXPERTS, per_expert, 0)

        def block_copy(b):
            rows = MOE_BM * ROW_TILE
            return pltpu.make_async_copy(
                h2_ref.at[pl.ds(0, rows)],
                xb_hbm.at[pl.ds(pl.multiple_of(b * rows, rows), rows)], sem)

        def fill_block(b, c):
            block_copy(b).start()
            return c

        def drain_block(b, c):
            block_copy(b).wait()
            return c

        n_blocks = xb_hbm.shape[0] // (MOE_BM * ROW_TILE)
        lax.fori_loop(nblk_ref[0], n_blocks, fill_block, 0)
        lax.fori_loop(nblk_ref[0], n_blocks, drain_block, 0)


def _dispatch(dest_flat, padlo, padhi, nblk, h2, n_rows):
    N = h2.shape[0] // ROW_TILE
    assert DISPATCH_TB >= MOE_BM and DISPATCH_TB % ROW_UNROLL == 0
    return pl.pallas_call(
        _dispatch_kernel,
        grid_spec=pltpu.PrefetchScalarGridSpec(
            num_scalar_prefetch=4,
            grid=(N // DISPATCH_TB,),
            in_specs=[pl.BlockSpec((DISPATCH_TB * ROW_TILE, LANES), lambda i, *_: (i, 0))],
            out_specs=pl.BlockSpec(memory_space=pl.ANY),
            scratch_shapes=[pltpu.SemaphoreType.DMA(())],
        ),
        out_shape=jax.ShapeDtypeStruct((n_rows * ROW_TILE, LANES), F32),
        compiler_params=pltpu.CompilerParams(
            dimension_semantics=("arbitrary",), has_side_effects=True,
            vmem_limit_bytes=VMEM_LIMIT),
        name="moe_dispatch",
    )(dest_flat, padlo, padhi, nblk, h2)


def _combine_kernel(dest_ref, x1_ref, gate_ref, g_ref, y_hbm, o_ref, ybuf, sems):
    i = pl.program_id(0)
    n = pl.num_programs(0)
    tb = COMBINE_TB

    def row_copy(src_row, slot, k, j):
        src = y_hbm.at[pl.ds(pl.multiple_of(src_row * ROW_TILE, ROW_TILE), ROW_TILE)]
        dst = ybuf.at[slot, k, pl.ds(pl.multiple_of(j * ROW_TILE, ROW_TILE), ROW_TILE)]
        return pltpu.make_async_copy(src, dst, sems.at[slot])

    def gather(step, slot):
        def issue(jo, carry):
            for ji in range(ROW_UNROLL):
                j = jo * ROW_UNROLL + ji
                for k in range(TOP_K):
                    row_copy(dest_ref[(step * tb + j) * TOP_K + k], slot, k, j).start(
                        priority=k % 2)
            return carry

        lax.fori_loop(0, tb // ROW_UNROLL, issue, 0)

    slot = i % 2

    @pl.when(i == 0)
    def _():
        gather(0, 0)

    @pl.when(i + 1 < n)
    def _():
        gather(i + 1, 1 - slot)

    _drain_rows(lambda: row_copy(0, slot, 0, 0).wait(), tb * TOP_K)

    gates = gate_ref[...]
    moe = gates[:, 0:1] * _load_row_tiles(ybuf.at[slot, 0], tb)
    for k in range(1, TOP_K):
        moe = moe + gates[:, k:k + 1] * _load_row_tiles(ybuf.at[slot, k], tb)
    o_ref[...] = _rms(x1_ref[...] + moe, g_ref[...])


def _combine(dest_flat, x1, gates, final_g, y):
    N = x1.shape[0]
    tb = COMBINE_TB
    return pl.pallas_call(
        _combine_kernel,
        grid_spec=pltpu.PrefetchScalarGridSpec(
            num_scalar_prefetch=1,
            grid=(N // tb,),
            in_specs=[
                pl.BlockSpec((tb, D_MODEL), lambda i, *_: (i, 0)),
                pl.BlockSpec((tb, TOP_K), lambda i, *_: (i, 0)),
                pl.BlockSpec((1, D_MODEL), lambda i, *_: (0, 0)),
                pl.BlockSpec(memory_space=pl.ANY),
            ],
            out_specs=pl.BlockSpec((tb, D_MODEL), lambda i, *_: (i, 0)),
            scratch_shapes=[pltpu.VMEM((2, TOP_K, tb * ROW_TILE, LANES), F32),
                            pltpu.SemaphoreType.DMA((2,))],
        ),
        out_shape=jax.ShapeDtypeStruct((N, D_MODEL), F32),
        compiler_params=pltpu.CompilerParams(
            dimension_semantics=("arbitrary",), vmem_limit_bytes=VMEM_LIMIT),
        name="moe_combine",
    )(dest_flat, x1, gates, final_g, y)


def _layer(x2, norm1_g, w_in, pool_w, pool_scale, rel_bias, w_out, norm2_g,
           router_w, router_b, w_gate_up, b_gate_up, w_down, b_down, final_g):
    S = x2.shape[0]
    u, nat, p4, p16 = _inproj(x2, norm1_g.reshape(1, D_MODEL), w_in.astype(BF16), S)
    att = _attention(nat, p4, p16, _bias_tables(rel_bias), S)

    n_g = len(POOL_WINDOWS)
    pw_bd = (jnp.eye(n_g, dtype=F32)[:, None, :, None] * pool_w[:, :, None, :]).reshape(
        POOL_WIDTH, POOL_WIDTH).astype(BF16)
    x1, h2, topi, gates, rank, counts = _post(
        x2, u, att, pw_bd, pool_scale.reshape(1, POOL_WIDTH), w_out.astype(BF16),
        norm2_g.reshape(1, D_MODEL), router_w, router_b.reshape(1, N_EXPERTS), S)

    counts = counts.reshape(N_EXPERTS)
    blocks_per = (counts + MOE_BM - 1) // MOE_BM
    block_end = jnp.cumsum(blocks_per)
    pstart = ((block_end - blocks_per) * MOE_BM).astype(jnp.int32)
    nb = (S * TOP_K) // MOE_BM + N_EXPERTS
    block_exp = jnp.minimum(
        jnp.sum(block_end[None, :] <= jnp.arange(nb, dtype=jnp.int32)[:, None], axis=1),
        N_EXPERTS - 1).astype(jnp.int32)
    nblk = block_end[-1:].astype(jnp.int32)
    padlo = (pstart + counts).astype(jnp.int32)
    padhi = (pstart + blocks_per * MOE_BM).astype(jnp.int32)
    expert_ids = jnp.arange(N_EXPERTS, dtype=jnp.int32)
    dest = rank + jnp.sum(jnp.where(topi[:, :, None] == expert_ids, pstart, 0), axis=-1)
    dest_flat = dest.reshape(S * TOP_K).astype(jnp.int32)

    xb = _dispatch(dest_flat, padlo, padhi, nblk, h2, nb * MOE_BM)
    y = _experts(block_exp, nblk, xb, w_gate_up, b_gate_up.reshape(N_EXPERTS, 1, 2 * D_FF),
                 w_down, b_down.reshape(N_EXPERTS, 1, D_MODEL))
    return _combine(dest_flat, x1, gates, final_g.reshape(1, D_MODEL), y)


def kernel(x, norm1_g, w_in, pool_w, pool_scale, rel_bias, w_out, norm2_g, router_w, router_b,
           w_gate_up, b_gate_up, w_down, b_down, final_g):
    B, S, D = x.shape
    assert B == 1 and D == D_MODEL and S % ATT_TILE == 0, x.shape
    out = _layer(x.reshape(B * S, D), norm1_g[0], w_in[0], pool_w[0], pool_scale[0], rel_bias,
                 w_out[0], norm2_g[0], router_w[0], router_b[0], w_gate_up[0], b_gate_up[0],
                 w_down[0], b_down[0], final_g)
    return out.reshape(B, S, D)
```

```python
import math

import jax
import jax.numpy as jnp
from jax import lax
from jax.experimental import pallas as pl
from jax.experimental.pallas import tpu as pltpu

F32 = jnp.float32
BF16 = jnp.bfloat16

D_MODEL = 1024
POOL_WIDTH = 256
POOL_WINDOWS = (2, 4, 8, 16)
POOL_GROUP = 64
ATT_WIDTH = 768
HEAD_DIM = 64
N_HEADS = 12
N_PAIRS = N_HEADS // 2
DILATIONS = (1, 4, 16)
ATT_BLOCK = 128
N_BUCKETS = 32
MAX_DISTANCE = 2048
N_EXPERTS = 32
TOP_K = 4
D_FF = 1024
SWIGLU_LIMIT = 7.0
SWIGLU_ALPHA = 1.702
RMS_EPS = 1e-5
NEG_INF = -1e30
LOG2E = math.log2(math.e)

LANES = 128
ATT_TILE = 16 * ATT_BLOCK
BLOCKS_PER_TILE = ATT_TILE // ATT_BLOCK
N_UNITS = len(DILATIONS) * BLOCKS_PER_TILE
ATT_GROUP = 4
QKV_SLABS = 3 * N_PAIRS
IN_TM = 512
POST_TM = 256
MOE_BM = 256
DISPATCH_TB = 512
COMBINE_TB = 256
ROW_UNROLL = 8
VMEM_LIMIT = 56 * 1024 * 1024

_CLASS_ROWS = tuple(ATT_TILE // d for d in DILATIONS)
_SEG_ROWS = tuple(ATT_BLOCK + r for r in _CLASS_ROWS)
_SEG_BASE = (0, _SEG_ROWS[0], _SEG_ROWS[0] + DILATIONS[1] * _SEG_ROWS[1])
KV_ROWS = _SEG_BASE[2] + DILATIONS[2] * _SEG_ROWS[2]


def _rms(x, g):
    return x * lax.rsqrt(jnp.mean(x * x, axis=-1, keepdims=True) + RMS_EPS) * g


ROW_TILE = D_MODEL // LANES


def _store_row_tiles(ref, x):
    n = x.shape[0]
    for c in range(ROW_TILE):
        ref[pl.ds(c, n, stride=ROW_TILE), :] = x[:, c * LANES:(c + 1) * LANES]


def _load_row_tiles(ref, n):
    return jnp.concatenate(
        [ref[pl.ds(c, n, stride=ROW_TILE), :] for c in range(ROW_TILE)], axis=-1)


def _inproj_kernel(x_ref, g_ref, w_ref, u_ref, nat_ref, p4_ref, p16_ref, h_ref, acc_ref):
    h_ref[...] = _rms(x_ref[...], g_ref[...]).astype(BF16)
    u_ref[...] = jnp.dot(h_ref[...], w_ref[:, :POOL_WIDTH], preferred_element_type=F32)
    for jj in range(QKV_SLABS // 2):
        c0 = POOL_WIDTH + 2 * jj * LANES
        res = jnp.dot(h_ref[...], w_ref[:, c0:c0 + 2 * LANES], preferred_element_type=F32)
        for s in (0, 1):
            j = 2 * jj + s
            slab = res[:, s * LANES:(s + 1) * LANES]
            if j < N_PAIRS:
                slab = slab * (HEAD_DIM ** -0.5 * LOG2E)
            acc_ref[j] = slab
            nat_ref[j] = slab.astype(BF16)
    for j in range(QKV_SLABS):
        for d, ref in ((DILATIONS[1], p4_ref), (DILATIONS[2], p16_ref)):
            for r in range(d):
                ref[j, r] = acc_ref[j, pl.ds(r, IN_TM // d, stride=d), :].astype(BF16)


def _inproj(x, g, w_bf, S):
    sub = ATT_TILE // IN_TM
    n_tiles = S // ATT_TILE

    def res_spec(d):
        return pl.BlockSpec((QKV_SLABS, None, d, IN_TM // d, LANES),
                            lambda i: (0, i // sub, 0, i % sub, 0))

    def res_shape(d):
        return jax.ShapeDtypeStruct((QKV_SLABS, n_tiles, d, ATT_TILE // d, LANES), BF16)

    return pl.pallas_call(
        _inproj_kernel,
        grid=(S // IN_TM,),
        in_specs=[
            pl.BlockSpec((IN_TM, D_MODEL), lambda i: (i, 0)),
            pl.BlockSpec((1, D_MODEL), lambda i: (0, 0)),
            pl.BlockSpec((D_MODEL, POOL_WIDTH + 3 * ATT_WIDTH), lambda i: (0, 0)),
        ],
        out_specs=[
            pl.BlockSpec((IN_TM, POOL_WIDTH), lambda i: (i, 0)),
            pl.BlockSpec((QKV_SLABS, IN_TM, LANES), lambda i: (0, i, 0)),
            res_spec(DILATIONS[1]),
            res_spec(DILATIONS[2]),
        ],
        out_shape=[
            jax.ShapeDtypeStruct((S, POOL_WIDTH), F32),
            jax.ShapeDtypeStruct((QKV_SLABS, S, LANES), BF16),
            res_shape(DILATIONS[1]),
            res_shape(DILATIONS[2]),
        ],
        scratch_shapes=[pltpu.VMEM((IN_TM, D_MODEL), BF16),
                        pltpu.VMEM((QKV_SLABS, IN_TM, LANES), F32)],
        compiler_params=pltpu.CompilerParams(
            dimension_semantics=("arbitrary",), vmem_limit_bytes=VMEM_LIMIT),
        name="inproj",
    )(x, g, w_bf)


def _t5_bucket(dist):
    max_exact = N_BUCKETS // 2
    is_small = dist < max_exact
    nf = jnp.maximum(dist, 1).astype(jnp.float32)
    large = max_exact + (jnp.log(nf / max_exact) / math.log(MAX_DISTANCE / max_exact)
                         * (N_BUCKETS - max_exact)).astype(jnp.int32)
    large = jnp.minimum(large, N_BUCKETS - 1)
    return jnp.where(is_small, dist, large)


def _bias_kernel(ids_ref, rb_ref, o_ref):
    h = pl.program_id(0)
    col = lax.broadcasted_iota(jnp.int32, (ATT_BLOCK, 2 * ATT_BLOCK), 1)
    for b in range(len(DILATIONS)):
        ids = ids_ref[b]
        tab = jnp.full(ids.shape, NEG_INF, F32)
        for k in range(N_BUCKETS):
            tab = jnp.where(ids == k, rb_ref[k, h] * LOG2E, tab)
        o_ref[0, b] = tab
        o_ref[1, b] = jnp.where(col < ATT_BLOCK, NEG_INF, tab)


def _bias_tables(rel_bias):
    qi = jnp.arange(ATT_BLOCK)[:, None]
    kj = jnp.arange(2 * ATT_BLOCK)[None, :]
    dist = qi + ATT_BLOCK - kj
    ok = (dist >= 0) & (dist <= ATT_BLOCK)
    ids = jnp.stack([jnp.where(ok, _t5_bucket(jnp.clip(dist, 0) * d), -1) for d in DILATIONS])
    return pl.pallas_call(
        _bias_kernel,
        grid=(N_HEADS,),
        in_specs=[
            pl.BlockSpec((len(DILATIONS), ATT_BLOCK, 2 * ATT_BLOCK), lambda h: (0, 0, 0)),
            pl.BlockSpec(memory_space=pltpu.SMEM),
        ],
        out_specs=pl.BlockSpec((2, len(DILATIONS), None, ATT_BLOCK, 2 * ATT_BLOCK),
                               lambda h: (0, 0, h, 0, 0)),
        out_shape=jax.ShapeDtypeStruct(
            (2, len(DILATIONS), N_HEADS, ATT_BLOCK, 2 * ATT_BLOCK), F32),
        compiler_params=pltpu.CompilerParams(dimension_semantics=("arbitrary",)),
        name="bias_tables",
    )(ids.astype(jnp.int32), rel_bias.astype(F32))


def _attn_kernel(qn, knp, kn, vnp, vn, q4, k4p, k4, v4p, v4, q16, k16p, k16, v16p, v16,
                 bias_ref, o_ref, qbuf, kbuf, vbuf0, vbuf1, sbuf0, sbuf1, pbuf0, pbuf1,
                 obuf, mbuf, lbuf, tbuf):
    i = pl.program_id(0)
    at_start = i == 0
    G = ATT_GROUP
    n_groups = N_UNITS // G
    sbufs, pbufs, vbufs = (sbuf0, sbuf1), (pbuf0, pbuf1), (vbuf0, vbuf1)
    lane = lax.broadcasted_iota(jnp.int32, (ATT_BLOCK, LANES), 1)
    lo = lane < HEAD_DIM
    head_mask = (jnp.where(lo, 1.0, 0.0).astype(BF16), jnp.where(lo, 0.0, 1.0).astype(BF16))

    @pl.when(jnp.logical_and(at_start, pl.program_id(1) == 0))
    def _():
        for buf in sbufs + pbufs:
            buf[...] = jnp.zeros_like(buf)

    for b, q in enumerate((qn, q4, q16)):
        qbuf[b * ATT_TILE:(b + 1) * ATT_TILE] = q[...]

    def fill_k(dst0, rows, val):
        kbuf[dst0:dst0 + rows] = val

    def fill_v(dst0, rows, val):
        for a in (0, 1):
            keep = head_mask[a][0:1]
            vbufs[a][dst0:dst0 + rows] = val * keep + (1 - keep)

    for fill, prevs, curs in ((fill_k, (knp, k4p, k16p), (kn, k4, k16)),
                              (fill_v, (vnp, v4p, v16p), (vn, v4, v16))):
        fill(0, ATT_BLOCK, prevs[0][...])
        fill(ATT_BLOCK, ATT_TILE, curs[0][...])
        for b in (1, 2):
            rows = _CLASS_ROWS[b]
            for c in range(DILATIONS[b]):
                base = _SEG_BASE[b] + c * _SEG_ROWS[b]
                fill(base, ATT_BLOCK, prevs[b][c])
                fill(base + ATT_BLOCK, rows, curs[b][c * rows:(c + 1) * rows])

    def geometry(u):
        b = u // BLOCKS_PER_TILE
        w = u % BLOCKS_PER_TILE
        sh = 4 - 2 * b
        c = lax.shift_right_logical(w, sh)
        a = w - lax.shift_left(c, sh)
        seg = ATT_BLOCK + lax.shift_right_logical(ATT_TILE, 2 * b)
        base = b * _SEG_BASE[1] + jnp.where(b == 2, _SEG_BASE[2] - 2 * _SEG_BASE[1], 0)
        krow = pl.multiple_of(base + c * seg + a * ATT_BLOCK, ATT_BLOCK)
        variant = jnp.logical_and(at_start, a == 0).astype(jnp.int32)
        return b, krow, variant

    def scores(g, sbuf):
        for j in range(G):
            u = g * G + j
            b, krow, variant = geometry(u)
            q = qbuf[pl.ds(pl.multiple_of(u * ATT_BLOCK, ATT_BLOCK), ATT_BLOCK), :]
            k = kbuf[pl.ds(krow, 2 * ATT_BLOCK), :]
            for a in (0, 1):
                s = lax.dot_general(q * head_mask[a], k, (((1,), (1,)), ((), ())),
                                    preferred_element_type=F32)
                sbuf[j, a] = s + bias_ref[variant, b, a]

    def softmax(g, sbuf, pbuf):
        for j in range(G):
            rows = pl.ds(pl.multiple_of((g * G + j) * ATT_BLOCK, ATT_BLOCK), ATT_BLOCK)
            maxes = []
            for a in (0, 1):
                s = sbuf[j, a]
                m = jnp.max(s, axis=-1, keepdims=True)
                pbuf[j, a] = jnp.exp2(s - m).astype(BF16)
                maxes.append(jnp.broadcast_to(m, (ATT_BLOCK, LANES)))
            mbuf[rows, :] = jnp.where(lo, maxes[0], maxes[1])

    def values(g, pbuf):
        for j in range(G):
            u = g * G + j
            _, krow, _ = geometry(u)
            rows = pl.ds(pl.multiple_of(u * ATT_BLOCK, ATT_BLOCK), ATT_BLOCK)
            o0 = jnp.dot(pbuf[j, 0], vbuf0[pl.ds(krow, 2 * ATT_BLOCK), :],
                         preferred_element_type=F32)
            o1 = jnp.dot(pbuf[j, 1], vbuf1[pl.ds(krow, 2 * ATT_BLOCK), :],
                         preferred_element_type=F32)
            obuf[rows, :] = jnp.where(lo, o0, o1)
            lbuf[rows, :] = pltpu.roll(jnp.where(lo, o1, o0), HEAD_DIM, axis=1)

    def trip_pair(tt, carry):
        clamp = lambda g: jnp.clip(g, 0, n_groups - 1)
        for par in (0, 1):
            t = 2 * tt + par
            values(clamp(t - 2), pbufs[par])
            scores(clamp(t), sbufs[par])
            softmax(clamp(t - 1), sbufs[1 - par], pbufs[1 - par])
        return carry

    assert n_groups % 2 == 0
    lax.fori_loop(0, (n_groups + 2) // 2, trip_pair, 0)

    for b in (1, 2):
        d, rows = DILATIONS[b], _CLASS_ROWS[b]
        for r in range(d):
            src = slice(b * ATT_TILE + r * rows, b * ATT_TILE + (r + 1) * rows)
            dst = pl.ds(r, rows, stride=d)
            for n, buf in enumerate((obuf, mbuf, lbuf)):
                tbuf[3 * (b - 1) + n, dst, :] = buf[src]

    def merge(c, carry):
        rows = pl.ds(pl.multiple_of(c * ATT_BLOCK, ATT_BLOCK), ATT_BLOCK)
        ms = (mbuf[rows, :], tbuf[1, rows, :], tbuf[4, rows, :])
        os_ = (obuf[rows, :], tbuf[0, rows, :], tbuf[3, rows, :])
        ls = (lbuf[rows, :], tbuf[2, rows, :], tbuf[5, rows, :])
        mm = jnp.maximum(jnp.maximum(ms[0], ms[1]), ms[2])
        ws = [jnp.exp2(m - mm) for m in ms]
        num = ws[0] * os_[0] + ws[1] * os_[1] + ws[2] * os_[2]
        den = ws[0] * ls[0] + ws[1] * ls[1] + ws[2] * ls[2]
        o_ref[rows, :] = (num / den).astype(o_ref.dtype)
        return carry

    lax.fori_loop(0, BLOCKS_PER_TILE, merge, 0)


def _attention(nat, p4, p16, bias, S):
    n_tiles = S // ATT_TILE
    d4, d16 = DILATIONS[1], DILATIONS[2]
    flat4 = p4.reshape(QKV_SLABS, S, LANES)
    flat16 = p16.reshape(QKV_SLABS, S, LANES)
    blk4 = p4.reshape(QKV_SLABS, n_tiles, d4, _CLASS_ROWS[1] // ATT_BLOCK, ATT_BLOCK, LANES)
    prev = lambda i: jnp.maximum(i - 1, 0)

    def cur(off):
        return pl.BlockSpec((None, ATT_TILE, LANES), lambda i, p: (off + p, i, 0))

    def prev_nat(off):
        return pl.BlockSpec((None, ATT_BLOCK, LANES),
                            lambda i, p: (off + p, jnp.maximum(i * BLOCKS_PER_TILE - 1, 0), 0))

    def prev4(off):
        last = _CLASS_ROWS[1] // ATT_BLOCK - 1
        return pl.BlockSpec((None, None, d4, None, ATT_BLOCK, LANES),
                            lambda i, p: (off + p, prev(i), 0, last, 0, 0))

    def prev16(off):
        return pl.BlockSpec((None, None, d16, ATT_BLOCK, LANES),
                            lambda i, p: (off + p, prev(i), 0, 0, 0))

    k_off, v_off = N_PAIRS, 2 * N_PAIRS
    return pl.pallas_call(
        _attn_kernel,
        grid=(n_tiles, N_PAIRS),
        in_specs=[
            cur(0), prev_nat(k_off), cur(k_off), prev_nat(v_off), cur(v_off),
            cur(0), prev4(k_off), cur(k_off), prev4(v_off), cur(v_off),
            cur(0), prev16(k_off), cur(k_off), prev16(v_off), cur(v_off),
            pl.BlockSpec((2, len(DILATIONS), 2, ATT_BLOCK, 2 * ATT_BLOCK),
                         lambda i, p: (0, 0, p, 0, 0)),
        ],
        out_specs=pl.BlockSpec((None, ATT_TILE, LANES), lambda i, p: (p, i, 0)),
        out_shape=jax.ShapeDtypeStruct((N_PAIRS, S, LANES), BF16),
        scratch_shapes=[
            pltpu.VMEM((N_UNITS * ATT_BLOCK, LANES), BF16),
            pltpu.VMEM((KV_ROWS, LANES), BF16),
            pltpu.VMEM((KV_ROWS, LANES), BF16),
            pltpu.VMEM((KV_ROWS, LANES), BF16),
            pltpu.VMEM((ATT_GROUP, 2, ATT_BLOCK, 2 * ATT_BLOCK), F32),
            pltpu.VMEM((ATT_GROUP, 2, ATT_BLOCK, 2 * ATT_BLOCK), F32),
            pltpu.VMEM((ATT_GROUP, 2, ATT_BLOCK, 2 * ATT_BLOCK), BF16),
            pltpu.VMEM((ATT_GROUP, 2, ATT_BLOCK, 2 * ATT_BLOCK), BF16),
            pltpu.VMEM((N_UNITS * ATT_BLOCK, LANES), F32),
            pltpu.VMEM((N_UNITS * ATT_BLOCK, LANES), F32),
            pltpu.VMEM((N_UNITS * ATT_BLOCK, LANES), F32),
            pltpu.VMEM((6, ATT_TILE, LANES), F32),
        ],
        compiler_params=pltpu.CompilerParams(
            dimension_semantics=("arbitrary", "arbitrary"), vmem_limit_bytes=VMEM_LIMIT),
        name="attention",
    )(nat, nat, nat, nat, nat, flat4, blk4, flat4, blk4, flat4,
      flat16, p16, flat16, p16, flat16, bias)


def _post_kernel(x_ref, u_ref, up_ref, att_ref, pw_ref, ps_ref, wo_ref, g2_ref, rwh_ref, rwl_ref, rb_ref,
                 x1_ref, h2_ref, topi_ref, gate_ref, rank_ref, cnt_ref, ue, carry):
    i = pl.program_id(0)
    tm = POST_TM
    maxw = max(POOL_WINDOWS)

    @pl.when(i == 0)
    def _():
        carry[...] = jnp.zeros_like(carry)

    hist = up_ref[...]
    ue[0:maxw] = jnp.where(i == 0, jnp.zeros_like(hist), hist)
    ue[maxw:] = u_ref[...]
    t_glob = i * tm + lax.broadcasted_iota(jnp.int32, (tm, LANES), 0)
    lane = lax.broadcasted_iota(jnp.int32, (tm, LANES), 1)
    halves = []
    for half, (w_lo, w_hi) in enumerate(((2, 4), (8, 16))):
        cols = slice(half * LANES, (half + 1) * LANES)
        s = ue[maxw:, cols]
        tok = s
        s_lo = None
        for j in range(1, w_hi):
            s = s + ue[maxw - j:maxw - j + tm, cols]
            if j == w_lo - 1:
                s_lo = s
        cnt_lo = jnp.minimum(t_glob + 1, w_lo).astype(F32)
        cnt_hi = jnp.minimum(t_glob + 1, w_hi).astype(F32)
        halves.append(jnp.where(lane < POOL_GROUP, s_lo / cnt_lo, s / cnt_hi) - tok)
    pooled = jnp.concatenate(halves, axis=-1)
    mixed = jnp.dot(pooled.astype(BF16), pw_ref[...], preferred_element_type=F32) * ps_ref[...]

    mix = jnp.concatenate([mixed.astype(BF16)] + [att_ref[j] for j in range(N_PAIRS)], axis=-1)
    x1 = x_ref[...] + jnp.dot(mix, wo_ref[...], preferred_element_type=F32)
    x1_ref[...] = x1
    h2 = _rms(x1, g2_ref[...])
    _store_row_tiles(h2_ref, h2)

    h2_hi = h2.astype(BF16)
    h2_lo = (h2 - h2_hi.astype(F32)).astype(BF16)
    logits = (jnp.dot(h2_hi, rwh_ref[...], preferred_element_type=F32)
              + jnp.dot(h2_hi, rwl_ref[...], preferred_element_type=F32)
              + jnp.dot(h2_lo, rwh_ref[...], preferred_element_type=F32)) + rb_ref[...]
    eidx = lax.broadcasted_iota(jnp.int32, (tm, N_EXPERTS), 1)
    vals = logits
    top_v, top_i = [], []
    for _ in range(TOP_K):
        m = jnp.max(vals, axis=-1, keepdims=True)
        idx = jnp.min(jnp.where(vals == m, eidx, N_EXPERTS), axis=-1, keepdims=True)
        top_v.append(m)
        top_i.append(idx)
        vals = jnp.where(eidx == idx, -jnp.inf, vals)
    ex = [jnp.exp(v - top_v[0]) for v in top_v]
    den = ex[0] + ex[1] + ex[2] + ex[3]

    hot = [(eidx == idx) for idx in top_i]
    cnt = sum(h.astype(F32) for h in hot)
    row = lax.broadcasted_iota(jnp.int32, (tm, tm), 0)
    col = lax.broadcasted_iota(jnp.int32, (tm, tm), 1)
    strict = jnp.where(col < row, 1.0, 0.0).astype(BF16)
    before = jnp.dot(strict, cnt.astype(BF16), preferred_element_type=F32) + carry[...]
    ranks = [jnp.sum(jnp.where(h, before, 0.0), axis=-1, keepdims=True) for h in hot]
    carry[...] = carry[...] + jnp.sum(cnt, axis=0, keepdims=True)
    cnt_ref[...] = carry[...].astype(jnp.int32)

    k4 = lax.broadcasted_iota(jnp.int32, (tm, TOP_K), 1)

    def pack(cols):
        out = jnp.broadcast_to(cols[0], (tm, TOP_K))
        for k in range(1, TOP_K):
            out = jnp.where(k4 == k, jnp.broadcast_to(cols[k], (tm, TOP_K)), out)
        return out

    topi_ref[...] = pack(top_i)
    gate_ref[...] = pack([e / den for e in ex])
    rank_ref[...] = pack(ranks).astype(jnp.int32)


def _post(x, u, att, pw_bd, pool_scale, wo_bf, g2, router_w, router_b, S):
    tm = POST_TM
    maxw = max(POOL_WINDOWS)
    row = lambda w: pl.BlockSpec((tm, w), lambda i: (i, 0))
    const = lambda a, b: pl.BlockSpec((a, b), lambda i: (0, 0))
    rw_hi = router_w.astype(BF16)
    rw_hi_rest = (router_w - rw_hi.astype(F32)).astype(BF16)
    return pl.pallas_call(
        _post_kernel,
        grid=(S // tm,),
        in_specs=[
            row(D_MODEL),
            row(POOL_WIDTH),
            pl.BlockSpec((maxw, POOL_WIDTH), lambda i: (jnp.maximum(i * (tm // maxw) - 1, 0), 0)),
            pl.BlockSpec((N_PAIRS, tm, LANES), lambda i: (0, i, 0)),
            const(POOL_WIDTH, POOL_WIDTH),
            const(1, POOL_WIDTH),
            const(D_MODEL, D_MODEL),
            const(1, D_MODEL),
            const(D_MODEL, N_EXPERTS),
            const(D_MODEL, N_EXPERTS),
            const(1, N_EXPERTS),
        ],
        out_specs=[row(D_MODEL), pl.BlockSpec((tm * ROW_TILE, LANES), lambda i: (i, 0)),
                   row(TOP_K), row(TOP_K), row(TOP_K), const(1, N_EXPERTS)],
        out_shape=[
            jax.ShapeDtypeStruct((S, D_MODEL), F32),
            jax.ShapeDtypeStruct((S * ROW_TILE, LANES), F32),
            jax.ShapeDtypeStruct((S, TOP_K), jnp.int32),
            jax.ShapeDtypeStruct((S, TOP_K), F32),
            jax.ShapeDtypeStruct((S, TOP_K), jnp.int32),
            jax.ShapeDtypeStruct((1, N_EXPERTS), jnp.int32),
        ],
        scratch_shapes=[pltpu.VMEM((tm + maxw, POOL_WIDTH), F32),
                        pltpu.VMEM((1, N_EXPERTS), F32)],
        compiler_params=pltpu.CompilerParams(
            dimension_semantics=("arbitrary",), vmem_limit_bytes=VMEM_LIMIT),
        name="post_attention",
    )(x, u, u, att, pw_bd, pool_scale, wo_bf, g2, rw_hi, rw_hi_rest, router_b)


def _expert_kernel(bexp_ref, nblk_ref, nexte_ref, slot_ref, xb_ref, wgu_hbm, bgu_ref, wd_hbm,
                   bd_ref, y_ref, wgu_f32, wd_f32, wgu_bf, wd_bf, sems):
    b = pl.program_id(0)
    e = bexp_ref[b]
    prev = bexp_ref[jnp.maximum(b - 1, 0)]
    fresh = jnp.logical_and(jnp.logical_or(b == 0, e != prev), b < nblk_ref[0])

    def fetch(expert, slot):
        return (pltpu.make_async_copy(wgu_hbm.at[expert], wgu_f32.at[slot], sems.at[0, slot]),
                pltpu.make_async_copy(wd_hbm.at[expert], wd_f32.at[slot], sems.at[1, slot]))

    @pl.when(b == 0)
    def _():
        for cp in fetch(e, slot_ref[e]):
            cp.start()

    @pl.when(fresh)
    def _():
        slot = slot_ref[e]
        for cp in fetch(e, slot):
            cp.wait()
        nxt = nexte_ref[e]

        @pl.when(nxt < N_EXPERTS)
        def _():
            for cp in fetch(nxt, 1 - slot):
                cp.start()

        wgu_bf[...] = wgu_f32[slot].astype(BF16)
        wd_bf[...] = wd_f32[slot].astype(BF16)

    @pl.when(b < nblk_ref[0])
    def _():
        x = _load_row_tiles(xb_ref, MOE_BM).astype(BF16)
        gu = jnp.dot(x, wgu_bf[...], preferred_element_type=F32) + bgu_ref[...]
        gate = jnp.minimum(gu[:, :D_FF], SWIGLU_LIMIT)
        up = jnp.clip(gu[:, D_FF:], -SWIGLU_LIMIT, SWIGLU_LIMIT)
        act = (up + 1.0) * (gate * jax.nn.sigmoid(gate * SWIGLU_ALPHA))
        y = jnp.dot(act.astype(BF16), wd_bf[...], preferred_element_type=F32) + bd_ref[...]
        _store_row_tiles(y_ref, y)

    @pl.when(b >= nblk_ref[0])
    def _():
        y_ref[...] = jnp.zeros_like(y_ref)


def _experts(block_exp, nblk, next_expert, expert_slot, xb, w_gate_up, b_gate_up, w_down, b_down):
    nb = xb.shape[0] // (MOE_BM * ROW_TILE)
    tiles = (MOE_BM * ROW_TILE, LANES)
    blk = lambda b, be, nk, *_: (jnp.minimum(b, nk[0] - 1), 0)
    wsel = lambda b, be, nk, *_: (be[jnp.minimum(b, nk[0] - 1)], 0, 0)
    return pl.pallas_call(
        _expert_kernel,
        grid_spec=pltpu.PrefetchScalarGridSpec(
            num_scalar_prefetch=4,
            grid=(nb,),
            in_specs=[
                pl.BlockSpec(tiles, blk),
                pl.BlockSpec(memory_space=pl.ANY),
                pl.BlockSpec((None, 1, 2 * D_FF), wsel),
                pl.BlockSpec(memory_space=pl.ANY),
                pl.BlockSpec((None, 1, D_MODEL), wsel),
            ],
            out_specs=pl.BlockSpec(tiles, lambda b, *_: (b, 0)),
            scratch_shapes=[pltpu.VMEM((2, D_MODEL, 2 * D_FF), F32),
                            pltpu.VMEM((2, D_FF, D_MODEL), F32),
                            pltpu.VMEM((D_MODEL, 2 * D_FF), BF16),
                            pltpu.VMEM((D_FF, D_MODEL), BF16),
                            pltpu.SemaphoreType.DMA((2, 2))],
        ),
        out_shape=jax.ShapeDtypeStruct(xb.shape, F32),
        compiler_params=pltpu.CompilerParams(
            dimension_semantics=("arbitrary",), vmem_limit_bytes=VMEM_LIMIT),
        name="moe_experts",
    )(block_exp, nblk, next_expert, expert_slot, xb, w_gate_up, b_gate_up, w_down, b_down)


def _drain_rows(wait_one, n):
    def body(j, carry):
        for _ in range(ROW_UNROLL):
            wait_one()
        return carry

    lax.fori_loop(0, n // ROW_UNROLL, body, 0)


def _dispatch_kernel(dest_ref, padlo_ref, padhi_ref, nblk_ref, h2_ref, xb_hbm, sem):
    i = pl.program_id(0)
    tb = DISPATCH_TB

    def row_copy(j, dst_row):
        src = h2_ref.at[pl.ds(pl.multiple_of(j * ROW_TILE, ROW_TILE), ROW_TILE)]
        dst = xb_hbm.at[pl.ds(pl.multiple_of(dst_row * ROW_TILE, ROW_TILE), ROW_TILE)]
        return pltpu.make_async_copy(src, dst, sem)

    def issue(jo, carry):
        for ji in range(ROW_UNROLL):
            j = jo * ROW_UNROLL + ji
            for k in range(TOP_K):
                row_copy(j, dest_ref[(i * tb + j) * TOP_K + k]).start(priority=k % 2)
        return carry

    lax.fori_loop(0, tb // ROW_UNROLL, issue, 0)
    _drain_rows(lambda: row_copy(0, 0).wait(), tb * TOP_K)

    @pl.when(i == pl.num_programs(0) - 1)
    def _():
        def per_expert(e, carry):
            lo, hi = padlo_ref[e], padhi_ref[e]

            def fill(s, c):
                row_copy(0, s).start()
                return c

            def drain(s, c):
                row_copy(0, 0).wait()
                return c

            lax.fori_loop(lo, hi, fill, 0)
            lax.fori_loop(lo, hi, drain, 0)
            return carry

        lax.fori_loop(0, N_EXPERTS, per_expert, 0)

        def block_copy(b):
            rows = MOE_BM * ROW_TILE
            return pltpu.make_async_copy(
                h2_ref.at[pl.ds(0, rows)],
                xb_hbm.at[pl.ds(pl.multiple_of(b * rows, rows), rows)], sem)

        def fill_block(b, c):
            block_copy(b).start()
            return c

        def drain_block(b, c):
            block_copy(b).wait()
            return c

        n_blocks = xb_hbm.shape[0] // (MOE_BM * ROW_TILE)
        lax.fori_loop(nblk_ref[0], n_blocks, fill_block, 0)
        lax.fori_loop(nblk_ref[0], n_blocks, drain_block, 0)


def _dispatch(dest_flat, padlo, padhi, nblk, h2, n_rows):
    N = h2.shape[0] // ROW_TILE
    assert DISPATCH_TB >= MOE_BM and DISPATCH_TB % ROW_UNROLL == 0
    return pl.pallas_call(
        _dispatch_kernel,
        grid_spec=pltpu.PrefetchScalarGridSpec(
            num_scalar_prefetch=4,
            grid=(N // DISPATCH_TB,),
            in_specs=[pl.BlockSpec((DISPATCH_TB * ROW_TILE, LANES), lambda i, *_: (i, 0))],
            out_specs=pl.BlockSpec(memory_space=pl.ANY),
            scratch_shapes=[pltpu.SemaphoreType.DMA(())],
        ),
        out_shape=jax.ShapeDtypeStruct((n_rows * ROW_TILE, LANES), F32),
        compiler_params=pltpu.CompilerParams(
            dimension_semantics=("arbitrary",), has_side_effects=True,
            vmem_limit_bytes=VMEM_LIMIT),
        name="moe_dispatch",
    )(dest_flat, padlo, padhi, nblk, h2)


def _combine_kernel(dest_ref, x1_ref, gate_ref, g_ref, y_hbm, o_ref, ybuf, sems):
    i = pl.program_id(0)
    n = pl.num_programs(0)
    tb = COMBINE_TB

    def row_copy(src_row, slot, k, j):
        src = y_hbm.at[pl.ds(pl.multiple_of(src_row * ROW_TILE, ROW_TILE), ROW_TILE)]
        dst = ybuf.at[slot, k, pl.ds(pl.multiple_of(j * ROW_TILE, ROW_TILE), ROW_TILE)]
        return pltpu.make_async_copy(src, dst, sems.at[slot])

    def gather(step, slot):
        def issue(jo, carry):
            for ji in range(ROW_UNROLL):
                j = jo * ROW_UNROLL + ji
                for k in range(TOP_K):
                    row_copy(dest_ref[(step * tb + j) * TOP_K + k], slot, k, j).start(
                        priority=k % 2)
            return carry

        lax.fori_loop(0, tb // ROW_UNROLL, issue, 0)

    slot = i % 2

    @pl.when(i == 0)
    def _():
        gather(0, 0)

    @pl.when(i + 1 < n)
    def _():
        gather(i + 1, 1 - slot)

    _drain_rows(lambda: row_copy(0, slot, 0, 0).wait(), tb * TOP_K)

    gates = gate_ref[...]
    moe = gates[:, 0:1] * _load_row_tiles(ybuf.at[slot, 0], tb)
    for k in range(1, TOP_K):
        moe = moe + gates[:, k:k + 1] * _load_row_tiles(ybuf.at[slot, k], tb)
    o_ref[...] = _rms(x1_ref[...] + moe, g_ref[...])


def _combine(dest_flat, x1, gates, final_g, y):
    N = x1.shape[0]
    tb = COMBINE_TB
    return pl.pallas_call(
        _combine_kernel,
        grid_spec=pltpu.PrefetchScalarGridSpec(
            num_scalar_prefetch=1,
            grid=(N // tb,),
            in_specs=[
                pl.BlockSpec((tb, D_MODEL), lambda i, *_: (i, 0)),
                pl.BlockSpec((tb, TOP_K), lambda i, *_: (i, 0)),
                pl.BlockSpec((1, D_MODEL), lambda i, *_: (0, 0)),
                pl.BlockSpec(memory_space=pl.ANY),
            ],
            out_specs=pl.BlockSpec((tb, D_MODEL), lambda i, *_: (i, 0)),
            scratch_shapes=[pltpu.VMEM((2, TOP_K, tb * ROW_TILE, LANES), F32),
                            pltpu.SemaphoreType.DMA((2,))],
        ),
        out_shape=jax.ShapeDtypeStruct((N, D_MODEL), F32),
        compiler_params=pltpu.CompilerParams(
            dimension_semantics=("arbitrary",), vmem_limit_bytes=VMEM_LIMIT),
        name="moe_combine",
    )(dest_flat, x1, gates, final_g, y)


def _layer(x2, norm1_g, w_in, pool_w, pool_scale, rel_bias, w_out, norm2_g,
           router_w, router_b, w_gate_up, b_gate_up, w_down, b_down, final_g):
    S = x2.shape[0]
    u, nat, p4, p16 = _inproj(x2, norm1_g.reshape(1, D_MODEL), w_in.astype(BF16), S)
    att = _attention(nat, p4, p16, _bias_tables(rel_bias), S)

    n_g = len(POOL_WINDOWS)
    pw_bd = (jnp.eye(n_g, dtype=F32)[:, None, :, None] * pool_w[:, :, None, :]).reshape(
        POOL_WIDTH, POOL_WIDTH).astype(BF16)
    x1, h2, topi, gates, rank, counts = _post(
        x2, u, att, pw_bd, pool_scale.reshape(1, POOL_WIDTH), w_out.astype(BF16),
        norm2_g.reshape(1, D_MODEL), router_w, router_b.reshape(1, N_EXPERTS), S)

    counts = counts.reshape(N_EXPERTS)
    blocks_per = (counts + MOE_BM - 1) // MOE_BM
    block_end = jnp.cumsum(blocks_per)
    pstart = ((block_end - blocks_per) * MOE_BM).astype(jnp.int32)
    nb = (S * TOP_K) // MOE_BM + N_EXPERTS
    block_exp = jnp.minimum(
        jnp.sum(block_end[None, :] <= jnp.arange(nb, dtype=jnp.int32)[:, None], axis=1),
        N_EXPERTS - 1).astype(jnp.int32)
    nblk = block_end[-1:].astype(jnp.int32)
    padlo = (pstart + counts).astype(jnp.int32)
    padhi = (pstart + blocks_per * MOE_BM).astype(jnp.int32)
    expert_ids = jnp.arange(N_EXPERTS, dtype=jnp.int32)
    dest = rank + jnp.sum(jnp.where(topi[:, :, None] == expert_ids, pstart, 0), axis=-1)
    dest_flat = dest.reshape(S * TOP_K).astype(jnp.int32)

    xb = _dispatch(dest_flat, padlo, padhi, nblk, h2, nb * MOE_BM)
    nonempty = blocks_per > 0
    first_at_or_after = jnp.flip(lax.cummin(jnp.flip(jnp.where(nonempty, expert_ids, N_EXPERTS))))
    next_expert = jnp.concatenate(
        [first_at_or_after[1:], jnp.full((1,), N_EXPERTS, jnp.int32)]).astype(jnp.int32)
    expert_slot = ((jnp.cumsum(nonempty) - nonempty) % 2).astype(jnp.int32)
    y = _experts(block_exp, nblk, next_expert, expert_slot, xb, w_gate_up, b_gate_up.reshape(N_EXPERTS, 1, 2 * D_FF),
                 w_down, b_down.reshape(N_EXPERTS, 1, D_MODEL))
    return _combine(dest_flat, x1, gates, final_g.reshape(1, D_MODEL), y)


def kernel(x, norm1_g, w_in, pool_w, pool_scale, rel_bias, w_out, norm2_g, router_w, router_b,
           w_gate_up, b_gate_up, w_down, b_down, final_g):
    B, S, D = x.shape
    assert B == 1 and D == D_MODEL and S % ATT_TILE == 0, x.shape
    out = _layer(x.reshape(B * S, D), norm1_g[0], w_in[0], pool_w[0], pool_scale[0], rel_bias,
                 w_out[0], norm2_g[0], router_w[0], router_b[0], w_gate_up[0], b_gate_up[0],
                 w_down[0], b_down[0], final_g)
    return out.reshape(B, S, D)
```

```python
import functools
import math

import jax
import jax.numpy as jnp
from jax import lax
from jax.experimental import pallas as pl
from jax.experimental.pallas import tpu as pltpu

F32 = jnp.float32
BF16 = jnp.bfloat16

D_MODEL = 1024
POOL_WIDTH = 256
POOL_WINDOWS = (2, 4, 8, 16)
POOL_GROUP = 64
ATT_WIDTH = 768
HEAD_DIM = 64
N_HEADS = 12
N_PAIRS = N_HEADS // 2
DILATIONS = (1, 4, 16)
ATT_BLOCK = 128
N_BUCKETS = 32
MAX_DISTANCE = 2048
N_EXPERTS = 32
TOP_K = 4
D_FF = 1024
SWIGLU_LIMIT = 7.0
SWIGLU_ALPHA = 1.702
RMS_EPS = 1e-5
NEG_INF = -1e30
LOG2E = math.log2(math.e)

LANES = 128
ATT_TILE = 16 * ATT_BLOCK
BLOCKS_PER_TILE = ATT_TILE // ATT_BLOCK
N_UNITS = len(DILATIONS) * BLOCKS_PER_TILE
ATT_GROUP = 4
QKV_SLABS = 3 * N_PAIRS
IN_TM = 512
POST_TM = 256
MOE_BM = 256
COMBINE_TB = 256
ROW_UNROLL = 8
VMEM_LIMIT = 56 * 1024 * 1024

_CLASS_ROWS = tuple(ATT_TILE // d for d in DILATIONS)
_SEG_ROWS = tuple(ATT_BLOCK + r for r in _CLASS_ROWS)
_SEG_BASE = (0, _SEG_ROWS[0], _SEG_ROWS[0] + DILATIONS[1] * _SEG_ROWS[1])
KV_ROWS = _SEG_BASE[2] + DILATIONS[2] * _SEG_ROWS[2]


def _rms(x, g):
    return x * lax.rsqrt(jnp.mean(x * x, axis=-1, keepdims=True) + RMS_EPS) * g


ROW_TILE = D_MODEL // LANES


def _store_row_tiles(ref, x):
    n = x.shape[0]
    for c in range(ROW_TILE):
        ref[pl.ds(c, n, stride=ROW_TILE), :] = x[:, c * LANES:(c + 1) * LANES]


def _load_row_tiles(ref, n):
    return jnp.concatenate(
        [ref[pl.ds(c, n, stride=ROW_TILE), :] for c in range(ROW_TILE)], axis=-1)


def _inproj_kernel(x_ref, g_ref, w_ref, u_ref, nat_ref, p4_ref, p16_ref, h_ref, acc_ref):
    h_ref[...] = _rms(x_ref[...], g_ref[...]).astype(BF16)
    u_ref[...] = jnp.dot(h_ref[...], w_ref[:, :POOL_WIDTH], preferred_element_type=F32)
    for jj in range(QKV_SLABS // 2):
        c0 = POOL_WIDTH + 2 * jj * LANES
        res = jnp.dot(h_ref[...], w_ref[:, c0:c0 + 2 * LANES], preferred_element_type=F32)
        for s in (0, 1):
            j = 2 * jj + s
            slab = res[:, s * LANES:(s + 1) * LANES]
            if j < N_PAIRS:
                slab = slab * (HEAD_DIM ** -0.5 * LOG2E)
            acc_ref[j] = slab
            nat_ref[j] = slab.astype(BF16)
    for j in range(QKV_SLABS):
        for d, ref in ((DILATIONS[1], p4_ref), (DILATIONS[2], p16_ref)):
            for r in range(d):
                ref[j, r] = acc_ref[j, pl.ds(r, IN_TM // d, stride=d), :].astype(BF16)


def _inproj(x, g, w_bf, S):
    sub = ATT_TILE // IN_TM
    n_tiles = S // ATT_TILE

    def res_spec(d):
        return pl.BlockSpec((QKV_SLABS, None, d, IN_TM // d, LANES),
                            lambda i: (0, i // sub, 0, i % sub, 0))

    def res_shape(d):
        return jax.ShapeDtypeStruct((QKV_SLABS, n_tiles, d, ATT_TILE // d, LANES), BF16)

    return pl.pallas_call(
        _inproj_kernel,
        grid=(S // IN_TM,),
        in_specs=[
            pl.BlockSpec((IN_TM, D_MODEL), lambda i: (i, 0)),
            pl.BlockSpec((1, D_MODEL), lambda i: (0, 0)),
            pl.BlockSpec((D_MODEL, POOL_WIDTH + 3 * ATT_WIDTH), lambda i: (0, 0)),
        ],
        out_specs=[
            pl.BlockSpec((IN_TM, POOL_WIDTH), lambda i: (i, 0)),
            pl.BlockSpec((QKV_SLABS, IN_TM, LANES), lambda i: (0, i, 0)),
            res_spec(DILATIONS[1]),
            res_spec(DILATIONS[2]),
        ],
        out_shape=[
            jax.ShapeDtypeStruct((S, POOL_WIDTH), F32),
            jax.ShapeDtypeStruct((QKV_SLABS, S, LANES), BF16),
            res_shape(DILATIONS[1]),
            res_shape(DILATIONS[2]),
        ],
        scratch_shapes=[pltpu.VMEM((IN_TM, D_MODEL), BF16),
                        pltpu.VMEM((QKV_SLABS, IN_TM, LANES), F32)],
        compiler_params=pltpu.CompilerParams(
            dimension_semantics=("arbitrary",), vmem_limit_bytes=VMEM_LIMIT),
        name="inproj",
    )(x, g, w_bf)


def _t5_bucket(dist):
    max_exact = N_BUCKETS // 2
    is_small = dist < max_exact
    nf = jnp.maximum(dist, 1).astype(jnp.float32)
    large = max_exact + (jnp.log(nf / max_exact) / math.log(MAX_DISTANCE / max_exact)
                         * (N_BUCKETS - max_exact)).astype(jnp.int32)
    large = jnp.minimum(large, N_BUCKETS - 1)
    return jnp.where(is_small, dist, large)


def _bias_kernel(ids_ref, rb_ref, o_ref):
    h = pl.program_id(0)
    col = lax.broadcasted_iota(jnp.int32, (ATT_BLOCK, 2 * ATT_BLOCK), 1)
    for b in range(len(DILATIONS)):
        ids = ids_ref[b]
        tab = jnp.full(ids.shape, NEG_INF, F32)
        for k in range(N_BUCKETS):
            tab = jnp.where(ids == k, rb_ref[k, h] * LOG2E, tab)
        o_ref[0, b] = tab
        o_ref[1, b] = jnp.where(col < ATT_BLOCK, NEG_INF, tab)


def _bias_tables(rel_bias):
    qi = jnp.arange(ATT_BLOCK)[:, None]
    kj = jnp.arange(2 * ATT_BLOCK)[None, :]
    dist = qi + ATT_BLOCK - kj
    ok = (dist >= 0) & (dist <= ATT_BLOCK)
    ids = jnp.stack([jnp.where(ok, _t5_bucket(jnp.clip(dist, 0) * d), -1) for d in DILATIONS])
    return pl.pallas_call(
        _bias_kernel,
        grid=(N_HEADS,),
        in_specs=[
            pl.BlockSpec((len(DILATIONS), ATT_BLOCK, 2 * ATT_BLOCK), lambda h: (0, 0, 0)),
            pl.BlockSpec(memory_space=pltpu.SMEM),
        ],
        out_specs=pl.BlockSpec((2, len(DILATIONS), None, ATT_BLOCK, 2 * ATT_BLOCK),
                               lambda h: (0, 0, h, 0, 0)),
        out_shape=jax.ShapeDtypeStruct(
            (2, len(DILATIONS), N_HEADS, ATT_BLOCK, 2 * ATT_BLOCK), F32),
        compiler_params=pltpu.CompilerParams(dimension_semantics=("arbitrary",)),
        name="bias_tables",
    )(ids.astype(jnp.int32), rel_bias.astype(F32))


def _attn_kernel(qn, knp, kn, vnp, vn, q4, k4p, k4, v4p, v4, q16, k16p, k16, v16p, v16,
                 bias_ref, o_ref, qbuf, kbuf, vbuf0, vbuf1, sbuf0, sbuf1, pbuf0, pbuf1,
                 obuf, mbuf, lbuf, tbuf):
    i = pl.program_id(0)
    at_start = i == 0
    G = ATT_GROUP
    n_groups = N_UNITS // G
    sbufs, pbufs, vbufs = (sbuf0, sbuf1), (pbuf0, pbuf1), (vbuf0, vbuf1)
    lane = lax.broadcasted_iota(jnp.int32, (ATT_BLOCK, LANES), 1)
    lo = lane < HEAD_DIM
    head_mask = (jnp.where(lo, 1.0, 0.0).astype(BF16), jnp.where(lo, 0.0, 1.0).astype(BF16))

    @pl.when(jnp.logical_and(at_start, pl.program_id(1) == 0))
    def _():
        for buf in sbufs + pbufs:
            buf[...] = jnp.zeros_like(buf)

    for b, q in enumerate((qn, q4, q16)):
        qbuf[b * ATT_TILE:(b + 1) * ATT_TILE] = q[...]

    def fill_k(dst0, rows, val):
        kbuf[dst0:dst0 + rows] = val

    def fill_v(dst0, rows, val):
        for a in (0, 1):
            keep = head_mask[a][0:1]
            vbufs[a][dst0:dst0 + rows] = val * keep + (1 - keep)

    for fill, prevs, curs in ((fill_k, (knp, k4p, k16p), (kn, k4, k16)),
                              (fill_v, (vnp, v4p, v16p), (vn, v4, v16))):
        fill(0, ATT_BLOCK, prevs[0][...])
        fill(ATT_BLOCK, ATT_TILE, curs[0][...])
        for b in (1, 2):
            rows = _CLASS_ROWS[b]
            for c in range(DILATIONS[b]):
                base = _SEG_BASE[b] + c * _SEG_ROWS[b]
                fill(base, ATT_BLOCK, prevs[b][c])
                fill(base + ATT_BLOCK, rows, curs[b][c * rows:(c + 1) * rows])

    def geometry(u):
        b = u // BLOCKS_PER_TILE
        w = u % BLOCKS_PER_TILE
        sh = 4 - 2 * b
        c = lax.shift_right_logical(w, sh)
        a = w - lax.shift_left(c, sh)
        seg = ATT_BLOCK + lax.shift_right_logical(ATT_TILE, 2 * b)
        base = b * _SEG_BASE[1] + jnp.where(b == 2, _SEG_BASE[2] - 2 * _SEG_BASE[1], 0)
        krow = pl.multiple_of(base + c * seg + a * ATT_BLOCK, ATT_BLOCK)
        variant = jnp.logical_and(at_start, a == 0).astype(jnp.int32)
        return b, krow, variant

    def scores(g, sbuf):
        for j in range(G):
            u = g * G + j
            b, krow, variant = geometry(u)
            q = qbuf[pl.ds(pl.multiple_of(u * ATT_BLOCK, ATT_BLOCK), ATT_BLOCK), :]
            k = kbuf[pl.ds(krow, 2 * ATT_BLOCK), :]
            for a in (0, 1):
                s = lax.dot_general(q * head_mask[a], k, (((1,), (1,)), ((), ())),
                                    preferred_element_type=F32)
                sbuf[j, a] = s + bias_ref[variant, b, a]

    def softmax(g, sbuf, pbuf):
        for j in range(G):
            rows = pl.ds(pl.multiple_of((g * G + j) * ATT_BLOCK, ATT_BLOCK), ATT_BLOCK)
            maxes = []
            for a in (0, 1):
                s = sbuf[j, a]
                m = jnp.max(s, axis=-1, keepdims=True)
                pbuf[j, a] = jnp.exp2(s - m).astype(BF16)
                maxes.append(jnp.broadcast_to(m, (ATT_BLOCK, LANES)))
            mbuf[rows, :] = jnp.where(lo, maxes[0], maxes[1])

    def values(g, pbuf):
        for j in range(G):
            u = g * G + j
            _, krow, _ = geometry(u)
            rows = pl.ds(pl.multiple_of(u * ATT_BLOCK, ATT_BLOCK), ATT_BLOCK)
            o0 = jnp.dot(pbuf[j, 0], vbuf0[pl.ds(krow, 2 * ATT_BLOCK), :],
                         preferred_element_type=F32)
            o1 = jnp.dot(pbuf[j, 1], vbuf1[pl.ds(krow, 2 * ATT_BLOCK), :],
                         preferred_element_type=F32)
            obuf[rows, :] = jnp.where(lo, o0, o1)
            lbuf[rows, :] = pltpu.roll(jnp.where(lo, o1, o0), HEAD_DIM, axis=1)

    def trip_pair(tt, carry):
        clamp = lambda g: jnp.clip(g, 0, n_groups - 1)
        for par in (0, 1):
            t = 2 * tt + par
            values(clamp(t - 2), pbufs[par])
            scores(clamp(t), sbufs[par])
            softmax(clamp(t - 1), sbufs[1 - par], pbufs[1 - par])
        return carry

    assert n_groups % 2 == 0
    lax.fori_loop(0, (n_groups + 2) // 2, trip_pair, 0)

    for b in (1, 2):
        d, rows = DILATIONS[b], _CLASS_ROWS[b]
        for r in range(d):
            src = slice(b * ATT_TILE + r * rows, b * ATT_TILE + (r + 1) * rows)
            dst = pl.ds(r, rows, stride=d)
            for n, buf in enumerate((obuf, mbuf, lbuf)):
                tbuf[3 * (b - 1) + n, dst, :] = buf[src]

    def merge(c, carry):
        rows = pl.ds(pl.multiple_of(c * ATT_BLOCK, ATT_BLOCK), ATT_BLOCK)
        ms = (mbuf[rows, :], tbuf[1, rows, :], tbuf[4, rows, :])
        os_ = (obuf[rows, :], tbuf[0, rows, :], tbuf[3, rows, :])
        ls = (lbuf[rows, :], tbuf[2, rows, :], tbuf[5, rows, :])
        mm = jnp.maximum(jnp.maximum(ms[0], ms[1]), ms[2])
        ws = [jnp.exp2(m - mm) for m in ms]
        num = ws[0] * os_[0] + ws[1] * os_[1] + ws[2] * os_[2]
        den = ws[0] * ls[0] + ws[1] * ls[1] + ws[2] * ls[2]
        o_ref[rows, :] = (num / den).astype(o_ref.dtype)
        return carry

    lax.fori_loop(0, BLOCKS_PER_TILE, merge, 0)


def _attention(nat, p4, p16, bias, S):
    n_tiles = S // ATT_TILE
    d4, d16 = DILATIONS[1], DILATIONS[2]
    flat4 = p4.reshape(QKV_SLABS, S, LANES)
    flat16 = p16.reshape(QKV_SLABS, S, LANES)
    blk4 = p4.reshape(QKV_SLABS, n_tiles, d4, _CLASS_ROWS[1] // ATT_BLOCK, ATT_BLOCK, LANES)
    prev = lambda i: jnp.maximum(i - 1, 0)

    def cur(off):
        return pl.BlockSpec((None, ATT_TILE, LANES), lambda i, p: (off + p, i, 0))

    def prev_nat(off):
        return pl.BlockSpec((None, ATT_BLOCK, LANES),
                            lambda i, p: (off + p, jnp.maximum(i * BLOCKS_PER_TILE - 1, 0), 0))

    def prev4(off):
        last = _CLASS_ROWS[1] // ATT_BLOCK - 1
        return pl.BlockSpec((None, None, d4, None, ATT_BLOCK, LANES),
                            lambda i, p: (off + p, prev(i), 0, last, 0, 0))

    def prev16(off):
        return pl.BlockSpec((None, None, d16, ATT_BLOCK, LANES),
                            lambda i, p: (off + p, prev(i), 0, 0, 0))

    k_off, v_off = N_PAIRS, 2 * N_PAIRS
    return pl.pallas_call(
        _attn_kernel,
        grid=(n_tiles, N_PAIRS),
        in_specs=[
            cur(0), prev_nat(k_off), cur(k_off), prev_nat(v_off), cur(v_off),
            cur(0), prev4(k_off), cur(k_off), prev4(v_off), cur(v_off),
            cur(0), prev16(k_off), cur(k_off), prev16(v_off), cur(v_off),
            pl.BlockSpec((2, len(DILATIONS), 2, ATT_BLOCK, 2 * ATT_BLOCK),
                         lambda i, p: (0, 0, p, 0, 0)),
        ],
        out_specs=pl.BlockSpec((None, ATT_TILE, LANES), lambda i, p: (p, i, 0)),
        out_shape=jax.ShapeDtypeStruct((N_PAIRS, S, LANES), BF16),
        scratch_shapes=[
            pltpu.VMEM((N_UNITS * ATT_BLOCK, LANES), BF16),
            pltpu.VMEM((KV_ROWS, LANES), BF16),
            pltpu.VMEM((KV_ROWS, LANES), BF16),
            pltpu.VMEM((KV_ROWS, LANES), BF16),
            pltpu.VMEM((ATT_GROUP, 2, ATT_BLOCK, 2 * ATT_BLOCK), F32),
            pltpu.VMEM((ATT_GROUP, 2, ATT_BLOCK, 2 * ATT_BLOCK), F32),
            pltpu.VMEM((ATT_GROUP, 2, ATT_BLOCK, 2 * ATT_BLOCK), BF16),
            pltpu.VMEM((ATT_GROUP, 2, ATT_BLOCK, 2 * ATT_BLOCK), BF16),
            pltpu.VMEM((N_UNITS * ATT_BLOCK, LANES), F32),
            pltpu.VMEM((N_UNITS * ATT_BLOCK, LANES), F32),
            pltpu.VMEM((N_UNITS * ATT_BLOCK, LANES), F32),
            pltpu.VMEM((6, ATT_TILE, LANES), F32),
        ],
        compiler_params=pltpu.CompilerParams(
            dimension_semantics=("arbitrary", "arbitrary"), vmem_limit_bytes=VMEM_LIMIT),
        name="attention",
    )(nat, nat, nat, nat, nat, flat4, blk4, flat4, blk4, flat4,
      flat16, p16, flat16, p16, flat16, bias)


def _drain_rows(wait_one, n):
    def body(j, carry):
        for _ in range(ROW_UNROLL):
            wait_one()
        return carry

    lax.fori_loop(0, n // ROW_UNROLL, body, 0)


def _post_kernel(n_tiles, x_ref, u_ref, up_ref, att_ref, pw_ref, ps_ref, wo_ref, g2_ref,
                 rwh_ref, rwl_ref, rb_ref, x1_ref, gate_ref, dest_ref, alloc_ref, cnt_ref, xb_hbm,
                 ue, carry, curblk, nfree, hbuf, dbuf, dsm, fbuf, fsm, row_sems, idx_sem):
    i = pl.program_id(0)
    tm = POST_TM
    maxw = max(POOL_WINDOWS)
    inv_bm = 1.0 / MOE_BM
    blocks_of = lambda c: jnp.floor((c + (MOE_BM - 1)) * inv_bm)

    def row_copy(slot, j, dst_row):
        src = hbuf.at[slot, pl.ds(pl.multiple_of(j * ROW_TILE, ROW_TILE), ROW_TILE)]
        dst = xb_hbm.at[pl.ds(pl.multiple_of(dst_row * ROW_TILE, ROW_TILE), ROW_TILE)]
        return pltpu.make_async_copy(src, dst, row_sems.at[slot])

    @pl.when(i == 0)
    def _():
        carry[...] = jnp.zeros_like(carry)
        curblk[...] = jnp.zeros_like(curblk)
        nfree[...] = jnp.zeros_like(nfree)

    @pl.when(i > 0)
    def _():
        pltpu.make_async_copy(dbuf, dsm, idx_sem).wait()
        slot = (i - 1) % 2

        def issue(jo, c):
            for ji in range(ROW_UNROLL):
                j = jo * ROW_UNROLL + ji
                for k in range(TOP_K):
                    row_copy(slot, j, dsm[j, k]).start(priority=k % 2)
            return c

        lax.fori_loop(0, tm // ROW_UNROLL, issue, 0)

    @pl.when(i < n_tiles)
    def _():
        hist = up_ref[...]
        ue[0:maxw] = jnp.where(i == 0, jnp.zeros_like(hist), hist)
        ue[maxw:] = u_ref[...]
        t_glob = i * tm + lax.broadcasted_iota(jnp.int32, (tm, LANES), 0)
        lane = lax.broadcasted_iota(jnp.int32, (tm, LANES), 1)
        halves = []
        for half, (w_lo, w_hi) in enumerate(((2, 4), (8, 16))):
            cols = slice(half * LANES, (half + 1) * LANES)
            s = ue[maxw:, cols]
            tok = s
            s_lo = None
            for j in range(1, w_hi):
                s = s + ue[maxw - j:maxw - j + tm, cols]
                if j == w_lo - 1:
                    s_lo = s
            cnt_lo = jnp.minimum(t_glob + 1, w_lo).astype(F32)
            cnt_hi = jnp.minimum(t_glob + 1, w_hi).astype(F32)
            halves.append(jnp.where(lane < POOL_GROUP, s_lo / cnt_lo, s / cnt_hi) - tok)
        pooled = jnp.concatenate(halves, axis=-1)
        mixed = jnp.dot(pooled.astype(BF16), pw_ref[...], preferred_element_type=F32) * ps_ref[...]

        mix = jnp.concatenate([mixed.astype(BF16)] + [att_ref[j] for j in range(N_PAIRS)],
                              axis=-1)
        x1 = x_ref[...] + jnp.dot(mix, wo_ref[...], preferred_element_type=F32)
        x1_ref[...] = x1
        h2 = _rms(x1, g2_ref[...])

        h2_hi = h2.astype(BF16)
        h2_lo = (h2 - h2_hi.astype(F32)).astype(BF16)
        logits = (jnp.dot(h2_hi, rwh_ref[...], preferred_element_type=F32)
                  + jnp.dot(h2_hi, rwl_ref[...], preferred_element_type=F32)
                  + jnp.dot(h2_lo, rwh_ref[...], preferred_element_type=F32)) + rb_ref[...]
        eidx = lax.broadcasted_iota(jnp.int32, (tm, N_EXPERTS), 1)
        vals = logits
        top_v, top_i = [], []
        for _ in range(TOP_K):
            m = jnp.max(vals, axis=-1, keepdims=True)
            idx = jnp.min(jnp.where(vals == m, eidx, N_EXPERTS), axis=-1, keepdims=True)
            top_v.append(m)
            top_i.append(idx)
            vals = jnp.where(eidx == idx, -jnp.inf, vals)
        ex = [jnp.exp(v - top_v[0]) for v in top_v]
        den = ex[0] + ex[1] + ex[2] + ex[3]

        hot = [(eidx == idx) for idx in top_i]
        cnt = sum(h.astype(F32) for h in hot)
        row = lax.broadcasted_iota(jnp.int32, (tm, tm), 0)
        col = lax.broadcasted_iota(jnp.int32, (tm, tm), 1)
        strict = jnp.where(col < row, 1.0, 0.0).astype(BF16)
        c0 = carry[...]
        before = jnp.dot(strict, cnt.astype(BF16), preferred_element_type=F32) + c0

        c1 = c0 + jnp.sum(cnt, axis=0, keepdims=True)
        nb0 = blocks_of(c0)
        opens = blocks_of(c1) - nb0
        er = lax.broadcasted_iota(jnp.int32, (N_EXPERTS, N_EXPERTS), 0)
        ec = lax.broadcasted_iota(jnp.int32, (N_EXPERTS, N_EXPERTS), 1)
        earlier = jnp.where(er < ec, 1.0, 0.0).astype(BF16)
        opened_before = jnp.dot(jnp.broadcast_to(opens, (8, N_EXPERTS)).astype(BF16), earlier,
                                preferred_element_type=F32)[0:1]
        new_blk = nfree[...] + opened_before
        cur = curblk[...]
        a_tok = jnp.floor(before * inv_bm)
        blk = jnp.where(a_tok == nb0 - 1.0, cur, new_blk)
        dest_full = blk * MOE_BM + (before - a_tok * MOE_BM)
        dests = [jnp.sum(jnp.where(h, dest_full, 0.0), axis=-1, keepdims=True) for h in hot]

        carry[...] = c1
        curblk[...] = jnp.where(opens > 0, new_blk, cur)
        nfree[...] = nfree[...] + jnp.sum(opens, axis=-1, keepdims=True)
        alloc_ref[...] = jnp.where(opens > 0, new_blk, -1.0).astype(jnp.int32)
        cnt_ref[...] = c1.astype(jnp.int32)

        k4 = lax.broadcasted_iota(jnp.int32, (tm, TOP_K), 1)

        def pack(cols):
            out = jnp.broadcast_to(cols[0], (tm, TOP_K))
            for k in range(1, TOP_K):
                out = jnp.where(k4 == k, jnp.broadcast_to(cols[k], (tm, TOP_K)), out)
            return out

        gate_ref[...] = pack([e / den for e in ex])
        dest = pack(dests).astype(jnp.int32)
        dest_ref[...] = dest

        @pl.when(i >= 2)
        def _():
            _drain_rows(lambda: row_copy(i % 2, 0, 0).wait(), tm * TOP_K)

        _store_row_tiles(hbuf.at[i % 2], h2)
        dbuf[...] = dest
        pltpu.make_async_copy(dbuf, dsm, idx_sem).start()

    @pl.when(i == n_tiles)
    def _():
        last = (n_tiles - 1) % 2
        _drain_rows(lambda: row_copy(last, 0, 0).wait(), tm * TOP_K)
        if n_tiles >= 2:
            _drain_rows(lambda: row_copy(1 - last, 0, 0).wait(), tm * TOP_K)

        c = carry[...]
        used_in_last = c - (blocks_of(c) - 1.0) * MOE_BM
        fbuf[0:1] = jnp.where(c > 0, curblk[...] * MOE_BM + used_in_last, 0.0).astype(jnp.int32)
        fbuf[1:2] = jnp.where(c > 0, (curblk[...] + 1.0) * MOE_BM, 0.0).astype(jnp.int32)
        fbuf[2:8] = jnp.broadcast_to(nfree[...], (6, N_EXPERTS)).astype(jnp.int32)
        fill = pltpu.make_async_copy(fbuf, fsm, idx_sem)
        fill.start()
        fill.wait()

        def per_expert(e, carry_):
            lo, hi = fsm[0, e], fsm[1, e]

            def start_row(s, c_):
                row_copy(last, 0, s).start()
                return c_

            def wait_row(s, c_):
                row_copy(last, 0, 0).wait()
                return c_

            lax.fori_loop(lo, hi, start_row, 0)
            lax.fori_loop(lo, hi, wait_row, 0)
            return carry_

        lax.fori_loop(0, N_EXPERTS, per_expert, 0)

        def block_copy(b):
            rows = MOE_BM * ROW_TILE
            return pltpu.make_async_copy(
                hbuf.at[last], xb_hbm.at[pl.ds(pl.multiple_of(b * rows, rows), rows)],
                row_sems.at[last])

        def start_block(b, c_):
            block_copy(b).start()
            return c_

        def wait_block(b, c_):
            block_copy(b).wait()
            return c_

        n_blocks = xb_hbm.shape[0] // (MOE_BM * ROW_TILE)
        lax.fori_loop(fsm[2, 0], n_blocks, start_block, 0)
        lax.fori_loop(fsm[2, 0], n_blocks, wait_block, 0)


def _post(x, u, att, pw_bd, pool_scale, wo_bf, g2, router_w, router_b, S):
    tm = POST_TM
    assert tm == MOE_BM and tm % ROW_UNROLL == 0
    n_tiles = S // tm
    n_blocks = (S * TOP_K) // MOE_BM + N_EXPERTS
    maxw = max(POOL_WINDOWS)
    tile = lambda i: jnp.minimum(i, n_tiles - 1)
    row = lambda w: pl.BlockSpec((tm, w), lambda i: (tile(i), 0))
    const = lambda a, b: pl.BlockSpec((a, b), lambda i: (0, 0))
    rw_hi = router_w.astype(BF16)
    rw_hi_rest = (router_w - rw_hi.astype(F32)).astype(BF16)
    vec = lambda: pltpu.VMEM((1, N_EXPERTS), F32)
    return pl.pallas_call(
        functools.partial(_post_kernel, n_tiles),
        grid=(n_tiles + 1,),
        in_specs=[
            row(D_MODEL),
            row(POOL_WIDTH),
            pl.BlockSpec((maxw, POOL_WIDTH),
                         lambda i: (jnp.maximum(tile(i) * (tm // maxw) - 1, 0), 0)),
            pl.BlockSpec((N_PAIRS, tm, LANES), lambda i: (0, tile(i), 0)),
            const(POOL_WIDTH, POOL_WIDTH),
            const(1, POOL_WIDTH),
            const(D_MODEL, D_MODEL),
            const(1, D_MODEL),
            const(D_MODEL, N_EXPERTS),
            const(D_MODEL, N_EXPERTS),
            const(1, N_EXPERTS),
        ],
        out_specs=[row(D_MODEL), row(TOP_K), row(TOP_K),
                   pl.BlockSpec((None, 1, N_EXPERTS), lambda i: (tile(i), 0, 0)),
                   const(1, N_EXPERTS),
                   pl.BlockSpec(memory_space=pl.ANY)],
        out_shape=[
            jax.ShapeDtypeStruct((S, D_MODEL), F32),
            jax.ShapeDtypeStruct((S, TOP_K), F32),
            jax.ShapeDtypeStruct((S, TOP_K), jnp.int32),
            jax.ShapeDtypeStruct((n_tiles, 1, N_EXPERTS), jnp.int32),
            jax.ShapeDtypeStruct((1, N_EXPERTS), jnp.int32),
            jax.ShapeDtypeStruct((n_blocks * MOE_BM * ROW_TILE, LANES), F32),
        ],
        scratch_shapes=[pltpu.VMEM((tm + maxw, POOL_WIDTH), F32),
                        vec(), vec(), vec(),
                        pltpu.VMEM((2, tm * ROW_TILE, LANES), F32),
                        pltpu.VMEM((tm, TOP_K), jnp.int32),
                        pltpu.SMEM((tm, TOP_K), jnp.int32),
                        pltpu.VMEM((8, N_EXPERTS), jnp.int32),
                        pltpu.SMEM((8, N_EXPERTS), jnp.int32),
                        pltpu.SemaphoreType.DMA((2,)),
                        pltpu.SemaphoreType.DMA(())],
        compiler_params=pltpu.CompilerParams(
            dimension_semantics=("arbitrary",), has_side_effects=True,
            vmem_limit_bytes=VMEM_LIMIT),
        name="post_attention",
    )(x, u, u, att, pw_bd, pool_scale, wo_bf, g2, rw_hi, rw_hi_rest, router_b)


def _expert_kernel(bexp_ref, nblk_ref, nexte_ref, slot_ref, order_ref, xb_ref, wgu_hbm, bgu_ref, wd_hbm,
                   bd_ref, y_ref, wgu_f32, wd_f32, wgu_bf, wd_bf, sems):
    b = pl.program_id(0)
    e = bexp_ref[b]
    prev = bexp_ref[jnp.maximum(b - 1, 0)]
    fresh = jnp.logical_and(jnp.logical_or(b == 0, e != prev), b < nblk_ref[0])

    def fetch(expert, slot):
        return (pltpu.make_async_copy(wgu_hbm.at[expert], wgu_f32.at[slot], sems.at[0, slot]),
                pltpu.make_async_copy(wd_hbm.at[expert], wd_f32.at[slot], sems.at[1, slot]))

    @pl.when(b == 0)
    def _():
        for cp in fetch(e, slot_ref[e]):
            cp.start()

    @pl.when(fresh)
    def _():
        slot = slot_ref[e]
        for cp in fetch(e, slot):
            cp.wait()
        nxt = nexte_ref[e]

        @pl.when(nxt < N_EXPERTS)
        def _():
            for cp in fetch(nxt, 1 - slot):
                cp.start()

        wgu_bf[...] = wgu_f32[slot].astype(BF16)
        wd_bf[...] = wd_f32[slot].astype(BF16)

    @pl.when(b < nblk_ref[0])
    def _():
        x = _load_row_tiles(xb_ref, MOE_BM).astype(BF16)
        gu = jnp.dot(x, wgu_bf[...], preferred_element_type=F32) + bgu_ref[...]
        gate = jnp.minimum(gu[:, :D_FF], SWIGLU_LIMIT)
        up = jnp.clip(gu[:, D_FF:], -SWIGLU_LIMIT, SWIGLU_LIMIT)
        act = (up + 1.0) * (gate * jax.nn.sigmoid(gate * SWIGLU_ALPHA))
        y = jnp.dot(act.astype(BF16), wd_bf[...], preferred_element_type=F32) + bd_ref[...]
        _store_row_tiles(y_ref, y)

    @pl.when(b >= nblk_ref[0])
    def _():
        y_ref[...] = jnp.zeros_like(y_ref)


def _experts(block_exp, nblk, next_expert, expert_slot, order, xb, w_gate_up, b_gate_up, w_down,
             b_down):
    nb = xb.shape[0] // (MOE_BM * ROW_TILE)
    tiles = (MOE_BM * ROW_TILE, LANES)
    blk = lambda b, be, nk, ne, sl, od: (od[jnp.minimum(b, nk[0] - 1)], 0)
    wsel = lambda b, be, nk, *_: (be[jnp.minimum(b, nk[0] - 1)], 0, 0)
    return pl.pallas_call(
        _expert_kernel,
        grid_spec=pltpu.PrefetchScalarGridSpec(
            num_scalar_prefetch=5,
            grid=(nb,),
            in_specs=[
                pl.BlockSpec(tiles, blk),
                pl.BlockSpec(memory_space=pl.ANY),
                pl.BlockSpec((None, 1, 2 * D_FF), wsel),
                pl.BlockSpec(memory_space=pl.ANY),
                pl.BlockSpec((None, 1, D_MODEL), wsel),
            ],
            out_specs=pl.BlockSpec(tiles, lambda b, be, nk, ne, sl, od: (od[b], 0)),
            scratch_shapes=[pltpu.VMEM((2, D_MODEL, 2 * D_FF), F32),
                            pltpu.VMEM((2, D_FF, D_MODEL), F32),
                            pltpu.VMEM((D_MODEL, 2 * D_FF), BF16),
                            pltpu.VMEM((D_FF, D_MODEL), BF16),
                            pltpu.SemaphoreType.DMA((2, 2))],
        ),
        out_shape=jax.ShapeDtypeStruct(xb.shape, F32),
        compiler_params=pltpu.CompilerParams(
            dimension_semantics=("arbitrary",), vmem_limit_bytes=VMEM_LIMIT),
        name="moe_experts",
    )(block_exp, nblk, next_expert, expert_slot, order, xb, w_gate_up, b_gate_up, w_down, b_down)


def _combine_kernel(dest_ref, x1_ref, gate_ref, g_ref, y_hbm, o_ref, ybuf, sems):
    i = pl.program_id(0)
    n = pl.num_programs(0)
    tb = COMBINE_TB

    def row_copy(src_row, slot, k, j):
        src = y_hbm.at[pl.ds(pl.multiple_of(src_row * ROW_TILE, ROW_TILE), ROW_TILE)]
        dst = ybuf.at[slot, k, pl.ds(pl.multiple_of(j * ROW_TILE, ROW_TILE), ROW_TILE)]
        return pltpu.make_async_copy(src, dst, sems.at[slot])

    def gather(step, slot):
        def issue(jo, carry):
            for ji in range(ROW_UNROLL):
                j = jo * ROW_UNROLL + ji
                for k in range(TOP_K):
                    row_copy(dest_ref[(step * tb + j) * TOP_K + k], slot, k, j).start(
                        priority=k % 2)
            return carry

        lax.fori_loop(0, tb // ROW_UNROLL, issue, 0)

    slot = i % 2

    @pl.when(i == 0)
    def _():
        gather(0, 0)

    @pl.when(i + 1 < n)
    def _():
        gather(i + 1, 1 - slot)

    _drain_rows(lambda: row_copy(0, slot, 0, 0).wait(), tb * TOP_K)

    gates = gate_ref[...]
    moe = gates[:, 0:1] * _load_row_tiles(ybuf.at[slot, 0], tb)
    for k in range(1, TOP_K):
        moe = moe + gates[:, k:k + 1] * _load_row_tiles(ybuf.at[slot, k], tb)
    o_ref[...] = _rms(x1_ref[...] + moe, g_ref[...])


def _combine(dest_flat, x1, gates, final_g, y):
    N = x1.shape[0]
    tb = COMBINE_TB
    return pl.pallas_call(
        _combine_kernel,
        grid_spec=pltpu.PrefetchScalarGridSpec(
            num_scalar_prefetch=1,
            grid=(N // tb,),
            in_specs=[
                pl.BlockSpec((tb, D_MODEL), lambda i, *_: (i, 0)),
                pl.BlockSpec((tb, TOP_K), lambda i, *_: (i, 0)),
                pl.BlockSpec((1, D_MODEL), lambda i, *_: (0, 0)),
                pl.BlockSpec(memory_space=pl.ANY),
            ],
            out_specs=pl.BlockSpec((tb, D_MODEL), lambda i, *_: (i, 0)),
            scratch_shapes=[pltpu.VMEM((2, TOP_K, tb * ROW_TILE, LANES), F32),
                            pltpu.SemaphoreType.DMA((2,))],
        ),
        out_shape=jax.ShapeDtypeStruct((N, D_MODEL), F32),
        compiler_params=pltpu.CompilerParams(
            dimension_semantics=("arbitrary",), vmem_limit_bytes=VMEM_LIMIT),
        name="moe_combine",
    )(dest_flat, x1, gates, final_g, y)


def _layer(x2, norm1_g, w_in, pool_w, pool_scale, rel_bias, w_out, norm2_g,
           router_w, router_b, w_gate_up, b_gate_up, w_down, b_down, final_g):
    S = x2.shape[0]
    u, nat, p4, p16 = _inproj(x2, norm1_g.reshape(1, D_MODEL), w_in.astype(BF16), S)
    att = _attention(nat, p4, p16, _bias_tables(rel_bias), S)

    n_g = len(POOL_WINDOWS)
    pw_bd = (jnp.eye(n_g, dtype=F32)[:, None, :, None] * pool_w[:, :, None, :]).reshape(
        POOL_WIDTH, POOL_WIDTH).astype(BF16)
    x1, gates, dest, alloc, counts, xb = _post(
        x2, u, att, pw_bd, pool_scale.reshape(1, POOL_WIDTH), w_out.astype(BF16),
        norm2_g.reshape(1, D_MODEL), router_w, router_b.reshape(1, N_EXPERTS), S)

    counts = counts.reshape(N_EXPERTS)
    blocks_per = (counts + MOE_BM - 1) // MOE_BM
    block_end = jnp.cumsum(blocks_per)
    nb = (S * TOP_K) // MOE_BM + N_EXPERTS
    step = jnp.arange(nb, dtype=jnp.int32)
    block_exp = jnp.minimum(jnp.sum(block_end[None, :] <= step[:, None], axis=1),
                            N_EXPERTS - 1).astype(jnp.int32)
    nblk = block_end[-1:].astype(jnp.int32)
    handed = alloc.reshape(-1, N_EXPERTS).T.reshape(-1)
    used = handed >= 0
    pos = jnp.cumsum(used) - 1
    order = jnp.sum(jnp.where(used[None, :] & (pos[None, :] == step[:, None]), handed[None, :], 0),
                    axis=1)
    order = jnp.where(step < nblk[0], order, step).astype(jnp.int32)
    expert_ids = jnp.arange(N_EXPERTS, dtype=jnp.int32)
    nonempty = blocks_per > 0
    first_at_or_after = jnp.flip(lax.cummin(jnp.flip(jnp.where(nonempty, expert_ids, N_EXPERTS))))
    next_expert = jnp.concatenate(
        [first_at_or_after[1:], jnp.full((1,), N_EXPERTS, jnp.int32)]).astype(jnp.int32)
    expert_slot = ((jnp.cumsum(nonempty) - nonempty) % 2).astype(jnp.int32)

    y = _experts(block_exp, nblk, next_expert, expert_slot, order, xb, w_gate_up,
                 b_gate_up.reshape(N_EXPERTS, 1, 2 * D_FF), w_down,
                 b_down.reshape(N_EXPERTS, 1, D_MODEL))
    return _combine(dest.reshape(S * TOP_K), x1, gates, final_g.reshape(1, D_MODEL), y)


def kernel(x, norm1_g, w_in, pool_w, pool_scale, rel_bias, w_out, norm2_g, router_w, router_b,
           w_gate_up, b_gate_up, w_down, b_down, final_g):
    B, S, D = x.shape
    assert B == 1 and D == D_MODEL and S % ATT_TILE == 0, x.shape
    out = _layer(x.reshape(B * S, D), norm1_g[0], w_in[0], pool_w[0], pool_scale[0], rel_bias,
                 w_out[0], norm2_g[0], router_w[0], router_b[0], w_gate_up[0], b_gate_up[0],
                 w_down[0], b_down[0], final_g)
    return out.reshape(B, S, D)
```

```python
import functools
import math

import jax
import jax.numpy as jnp
from jax import lax
from jax.experimental import pallas as pl
from jax.experimental.pallas import tpu as pltpu

F32 = jnp.float32
BF16 = jnp.bfloat16

D_MODEL = 1024
POOL_WIDTH = 256
POOL_WINDOWS = (2, 4, 8, 16)
POOL_GROUP = 64
ATT_WIDTH = 768
HEAD_DIM = 64
N_HEADS = 12
N_PAIRS = N_HEADS // 2
DILATIONS = (1, 4, 16)
ATT_BLOCK = 128
N_BUCKETS = 32
MAX_DISTANCE = 2048
N_EXPERTS = 32
TOP_K = 4
D_FF = 1024
SWIGLU_LIMIT = 7.0
SWIGLU_ALPHA = 1.702
RMS_EPS = 1e-5
NEG_INF = -1e30
LOG2E = math.log2(math.e)

LANES = 128
ATT_TILE = 16 * ATT_BLOCK
BLOCKS_PER_TILE = ATT_TILE // ATT_BLOCK
N_UNITS = len(DILATIONS) * BLOCKS_PER_TILE
ATT_GROUP = 4
QKV_SLABS = 3 * N_PAIRS
IN_TM = 512
POST_TM = 256
MOE_BM = 256
COMBINE_TB = 256
ROW_UNROLL = 8
VMEM_LIMIT = 56 * 1024 * 1024

_CLASS_ROWS = tuple(ATT_TILE // d for d in DILATIONS)
_SEG_ROWS = tuple(ATT_BLOCK + r for r in _CLASS_ROWS)
_SEG_BASE = (0, _SEG_ROWS[0], _SEG_ROWS[0] + DILATIONS[1] * _SEG_ROWS[1])
KV_ROWS = _SEG_BASE[2] + DILATIONS[2] * _SEG_ROWS[2]


def _rms(x, g):
    return x * lax.rsqrt(jnp.mean(x * x, axis=-1, keepdims=True) + RMS_EPS) * g


ROW_TILE = D_MODEL // LANES


def _store_row_tiles(ref, x):
    n = x.shape[0]
    for c in range(ROW_TILE):
        ref[pl.ds(c, n, stride=ROW_TILE), :] = x[:, c * LANES:(c + 1) * LANES]


def _load_row_tiles(ref, n):
    return jnp.concatenate(
        [ref[pl.ds(c, n, stride=ROW_TILE), :] for c in range(ROW_TILE)], axis=-1)


def _inproj_kernel(x_ref, g_ref, w_ref, u_ref, nat_ref, p4_ref, p16_ref, h_ref, acc_ref):
    h_ref[...] = _rms(x_ref[...], g_ref[...]).astype(BF16)
    u_ref[...] = jnp.dot(h_ref[...], w_ref[:, :POOL_WIDTH], preferred_element_type=F32)
    for jj in range(QKV_SLABS // 2):
        c0 = POOL_WIDTH + 2 * jj * LANES
        res = jnp.dot(h_ref[...], w_ref[:, c0:c0 + 2 * LANES], preferred_element_type=F32)
        for s in (0, 1):
            j = 2 * jj + s
            slab = res[:, s * LANES:(s + 1) * LANES]
            if j < N_PAIRS:
                slab = slab * (HEAD_DIM ** -0.5 * LOG2E)
            acc_ref[j] = slab
            nat_ref[j] = slab.astype(BF16)
    for j in range(QKV_SLABS):
        for d, ref in ((DILATIONS[1], p4_ref), (DILATIONS[2], p16_ref)):
            for r in range(d):
                ref[j, r] = acc_ref[j, pl.ds(r, IN_TM // d, stride=d), :].astype(BF16)


def _inproj(x, g, w_bf, S):
    sub = ATT_TILE // IN_TM
    n_tiles = S // ATT_TILE

    def res_spec(d):
        return pl.BlockSpec((QKV_SLABS, None, d, IN_TM // d, LANES),
                            lambda i: (0, i // sub, 0, i % sub, 0))

    def res_shape(d):
        return jax.ShapeDtypeStruct((QKV_SLABS, n_tiles, d, ATT_TILE // d, LANES), BF16)

    return pl.pallas_call(
        _inproj_kernel,
        grid=(S // IN_TM,),
        in_specs=[
            pl.BlockSpec((IN_TM, D_MODEL), lambda i: (i, 0)),
            pl.BlockSpec((1, D_MODEL), lambda i: (0, 0)),
            pl.BlockSpec((D_MODEL, POOL_WIDTH + 3 * ATT_WIDTH), lambda i: (0, 0)),
        ],
        out_specs=[
            pl.BlockSpec((IN_TM, POOL_WIDTH), lambda i: (i, 0)),
            pl.BlockSpec((QKV_SLABS, IN_TM, LANES), lambda i: (0, i, 0)),
            res_spec(DILATIONS[1]),
            res_spec(DILATIONS[2]),
        ],
        out_shape=[
            jax.ShapeDtypeStruct((S, POOL_WIDTH), F32),
            jax.ShapeDtypeStruct((QKV_SLABS, S, LANES), BF16),
            res_shape(DILATIONS[1]),
            res_shape(DILATIONS[2]),
        ],
        scratch_shapes=[pltpu.VMEM((IN_TM, D_MODEL), BF16),
                        pltpu.VMEM((QKV_SLABS, IN_TM, LANES), F32)],
        compiler_params=pltpu.CompilerParams(
            dimension_semantics=("arbitrary",), vmem_limit_bytes=VMEM_LIMIT),
        name="inproj",
    )(x, g, w_bf)


def _t5_bucket(dist):
    max_exact = N_BUCKETS // 2
    is_small = dist < max_exact
    nf = jnp.maximum(dist, 1).astype(jnp.float32)
    large = max_exact + (jnp.log(nf / max_exact) / math.log(MAX_DISTANCE / max_exact)
                         * (N_BUCKETS - max_exact)).astype(jnp.int32)
    large = jnp.minimum(large, N_BUCKETS - 1)
    return jnp.where(is_small, dist, large)


def _bias_kernel(ids_ref, rb_ref, o_ref):
    h = pl.program_id(0)
    col = lax.broadcasted_iota(jnp.int32, (ATT_BLOCK, 2 * ATT_BLOCK), 1)
    for b in range(len(DILATIONS)):
        ids = ids_ref[b]
        tab = jnp.full(ids.shape, NEG_INF, F32)
        for k in range(N_BUCKETS):
            tab = jnp.where(ids == k, rb_ref[k, h] * LOG2E, tab)
        o_ref[0, b] = tab
        o_ref[1, b] = jnp.where(col < ATT_BLOCK, NEG_INF, tab)


def _bias_tables(rel_bias):
    qi = jnp.arange(ATT_BLOCK)[:, None]
    kj = jnp.arange(2 * ATT_BLOCK)[None, :]
    dist = qi + ATT_BLOCK - kj
    ok = (dist >= 0) & (dist <= ATT_BLOCK)
    ids = jnp.stack([jnp.where(ok, _t5_bucket(jnp.clip(dist, 0) * d), -1) for d in DILATIONS])
    return pl.pallas_call(
        _bias_kernel,
        grid=(N_HEADS,),
        in_specs=[
            pl.BlockSpec((len(DILATIONS), ATT_BLOCK, 2 * ATT_BLOCK), lambda h: (0, 0, 0)),
            pl.BlockSpec(memory_space=pltpu.SMEM),
        ],
        out_specs=pl.BlockSpec((2, len(DILATIONS), None, ATT_BLOCK, 2 * ATT_BLOCK),
                               lambda h: (0, 0, h, 0, 0)),
        out_shape=jax.ShapeDtypeStruct(
            (2, len(DILATIONS), N_HEADS, ATT_BLOCK, 2 * ATT_BLOCK), F32),
        compiler_params=pltpu.CompilerParams(dimension_semantics=("arbitrary",)),
        name="bias_tables",
    )(ids.astype(jnp.int32), rel_bias.astype(F32))


def _attn_kernel(qn, knp, kn, vnp, vn, q4, k4p, k4, v4p, v4, q16, k16p, k16, v16p, v16,
                 bias_ref, o_ref, qbuf, kbuf, vbuf0, vbuf1, sbuf0, sbuf1, pbuf0, pbuf1,
                 obuf, mbuf, lbuf, tbuf):
    i = pl.program_id(0)
    at_start = i == 0
    G = ATT_GROUP
    n_groups = N_UNITS // G
    sbufs, pbufs, vbufs = (sbuf0, sbuf1), (pbuf0, pbuf1), (vbuf0, vbuf1)
    lane = lax.broadcasted_iota(jnp.int32, (ATT_BLOCK, LANES), 1)
    lo = lane < HEAD_DIM
    head_mask = (jnp.where(lo, 1.0, 0.0).astype(BF16), jnp.where(lo, 0.0, 1.0).astype(BF16))

    @pl.when(jnp.logical_and(at_start, pl.program_id(1) == 0))
    def _():
        for buf in sbufs + pbufs:
            buf[...] = jnp.zeros_like(buf)

    for b, q in enumerate((qn, q4, q16)):
        qbuf[b * ATT_TILE:(b + 1) * ATT_TILE] = q[...]

    def fill_k(dst0, rows, val):
        kbuf[dst0:dst0 + rows] = val

    def fill_v(dst0, rows, val):
        for a in (0, 1):
            keep = head_mask[a][0:1]
            vbufs[a][dst0:dst0 + rows] = val * keep + (1 - keep)

    for fill, prevs, curs in ((fill_k, (knp, k4p, k16p), (kn, k4, k16)),
                              (fill_v, (vnp, v4p, v16p), (vn, v4, v16))):
        fill(0, ATT_BLOCK, prevs[0][...])
        fill(ATT_BLOCK, ATT_TILE, curs[0][...])
        for b in (1, 2):
            rows = _CLASS_ROWS[b]
            for c in range(DILATIONS[b]):
                base = _SEG_BASE[b] + c * _SEG_ROWS[b]
                fill(base, ATT_BLOCK, prevs[b][c])
                fill(base + ATT_BLOCK, rows, curs[b][c * rows:(c + 1) * rows])

    def geometry(u):
        b = u // BLOCKS_PER_TILE
        w = u % BLOCKS_PER_TILE
        sh = 4 - 2 * b
        c = lax.shift_right_logical(w, sh)
        a = w - lax.shift_left(c, sh)
        seg = ATT_BLOCK + lax.shift_right_logical(ATT_TILE, 2 * b)
        base = b * _SEG_BASE[1] + jnp.where(b == 2, _SEG_BASE[2] - 2 * _SEG_BASE[1], 0)
        krow = pl.multiple_of(base + c * seg + a * ATT_BLOCK, ATT_BLOCK)
        variant = jnp.logical_and(at_start, a == 0).astype(jnp.int32)
        return b, krow, variant

    def scores(g, sbuf):
        for j in range(G):
            u = g * G + j
            b, krow, variant = geometry(u)
            q = qbuf[pl.ds(pl.multiple_of(u * ATT_BLOCK, ATT_BLOCK), ATT_BLOCK), :]
            k = kbuf[pl.ds(krow, 2 * ATT_BLOCK), :]
            for a in (0, 1):
                s = lax.dot_general(q * head_mask[a], k, (((1,), (1,)), ((), ())),
                                    preferred_element_type=F32)
                sbuf[j, a] = s + bias_ref[variant, b, a]

    def softmax(g, sbuf, pbuf):
        for j in range(G):
            rows = pl.ds(pl.multiple_of((g * G + j) * ATT_BLOCK, ATT_BLOCK), ATT_BLOCK)
            maxes = []
            for a in (0, 1):
                s = sbuf[j, a]
                m = jnp.max(s, axis=-1, keepdims=True)
                pbuf[j, a] = jnp.exp2(s - m).astype(BF16)
                maxes.append(jnp.broadcast_to(m, (ATT_BLOCK, LANES)))
            mbuf[rows, :] = jnp.where(lo, maxes[0], maxes[1])

    def values(g, pbuf):
        for j in range(G):
            u = g * G + j
            _, krow, _ = geometry(u)
            rows = pl.ds(pl.multiple_of(u * ATT_BLOCK, ATT_BLOCK), ATT_BLOCK)
            o0 = jnp.dot(pbuf[j, 0], vbuf0[pl.ds(krow, 2 * ATT_BLOCK), :],
                         preferred_element_type=F32)
            o1 = jnp.dot(pbuf[j, 1], vbuf1[pl.ds(krow, 2 * ATT_BLOCK), :],
                         preferred_element_type=F32)
            obuf[rows, :] = jnp.where(lo, o0, o1)
            lbuf[rows, :] = pltpu.roll(jnp.where(lo, o1, o0), HEAD_DIM, axis=1)

    def trip_pair(tt, carry):
        clamp = lambda g: jnp.clip(g, 0, n_groups - 1)
        for par in (0, 1):
            t = 2 * tt + par
            values(clamp(t - 2), pbufs[par])
            scores(clamp(t), sbufs[par])
            softmax(clamp(t - 1), sbufs[1 - par], pbufs[1 - par])
        return carry

    assert n_groups % 2 == 0
    lax.fori_loop(0, (n_groups + 2) // 2, trip_pair, 0)

    for b in (1, 2):
        d, rows = DILATIONS[b], _CLASS_ROWS[b]
        for r in range(d):
            src = slice(b * ATT_TILE + r * rows, b * ATT_TILE + (r + 1) * rows)
            dst = pl.ds(r, rows, stride=d)
            for n, buf in enumerate((obuf, mbuf, lbuf)):
                tbuf[3 * (b - 1) + n, dst, :] = buf[src]

    def merge(c, carry):
        rows = pl.ds(pl.multiple_of(c * ATT_BLOCK, ATT_BLOCK), ATT_BLOCK)
        ms = (mbuf[rows, :], tbuf[1, rows, :], tbuf[4, rows, :])
        os_ = (obuf[rows, :], tbuf[0, rows, :], tbuf[3, rows, :])
        ls = (lbuf[rows, :], tbuf[2, rows, :], tbuf[5, rows, :])
        mm = jnp.maximum(jnp.maximum(ms[0], ms[1]), ms[2])
        ws = [jnp.exp2(m - mm) for m in ms]
        num = ws[0] * os_[0] + ws[1] * os_[1] + ws[2] * os_[2]
        den = ws[0] * ls[0] + ws[1] * ls[1] + ws[2] * ls[2]
        o_ref[rows, :] = (num / den).astype(o_ref.dtype)
        return carry

    lax.fori_loop(0, BLOCKS_PER_TILE, merge, 0)


def _attention(nat, p4, p16, bias, S):
    n_tiles = S // ATT_TILE
    d4, d16 = DILATIONS[1], DILATIONS[2]
    flat4 = p4.reshape(QKV_SLABS, S, LANES)
    flat16 = p16.reshape(QKV_SLABS, S, LANES)
    blk4 = p4.reshape(QKV_SLABS, n_tiles, d4, _CLASS_ROWS[1] // ATT_BLOCK, ATT_BLOCK, LANES)
    prev = lambda i: jnp.maximum(i - 1, 0)

    def cur(off):
        return pl.BlockSpec((None, ATT_TILE, LANES), lambda i, p: (off + p, i, 0))

    def prev_nat(off):
        return pl.BlockSpec((None, ATT_BLOCK, LANES),
                            lambda i, p: (off + p, jnp.maximum(i * BLOCKS_PER_TILE - 1, 0), 0))

    def prev4(off):
        last = _CLASS_ROWS[1] // ATT_BLOCK - 1
        return pl.BlockSpec((None, None, d4, None, ATT_BLOCK, LANES),
                            lambda i, p: (off + p, prev(i), 0, last, 0, 0))

    def prev16(off):
        return pl.BlockSpec((None, None, d16, ATT_BLOCK, LANES),
                            lambda i, p: (off + p, prev(i), 0, 0, 0))

    k_off, v_off = N_PAIRS, 2 * N_PAIRS
    return pl.pallas_call(
        _attn_kernel,
        grid=(n_tiles, N_PAIRS),
        in_specs=[
            cur(0), prev_nat(k_off), cur(k_off), prev_nat(v_off), cur(v_off),
            cur(0), prev4(k_off), cur(k_off), prev4(v_off), cur(v_off),
            cur(0), prev16(k_off), cur(k_off), prev16(v_off), cur(v_off),
            pl.BlockSpec((2, len(DILATIONS), 2, ATT_BLOCK, 2 * ATT_BLOCK),
                         lambda i, p: (0, 0, p, 0, 0)),
        ],
        out_specs=pl.BlockSpec((None, ATT_TILE, LANES), lambda i, p: (p, i, 0)),
        out_shape=jax.ShapeDtypeStruct((N_PAIRS, S, LANES), BF16),
        scratch_shapes=[
            pltpu.VMEM((N_UNITS * ATT_BLOCK, LANES), BF16),
            pltpu.VMEM((KV_ROWS, LANES), BF16),
            pltpu.VMEM((KV_ROWS, LANES), BF16),
            pltpu.VMEM((KV_ROWS, LANES), BF16),
            pltpu.VMEM((ATT_GROUP, 2, ATT_BLOCK, 2 * ATT_BLOCK), F32),
            pltpu.VMEM((ATT_GROUP, 2, ATT_BLOCK, 2 * ATT_BLOCK), F32),
            pltpu.VMEM((ATT_GROUP, 2, ATT_BLOCK, 2 * ATT_BLOCK), BF16),
            pltpu.VMEM((ATT_GROUP, 2, ATT_BLOCK, 2 * ATT_BLOCK), BF16),
            pltpu.VMEM((N_UNITS * ATT_BLOCK, LANES), F32),
            pltpu.VMEM((N_UNITS * ATT_BLOCK, LANES), F32),
            pltpu.VMEM((N_UNITS * ATT_BLOCK, LANES), F32),
            pltpu.VMEM((6, ATT_TILE, LANES), F32),
        ],
        compiler_params=pltpu.CompilerParams(
            dimension_semantics=("arbitrary", "arbitrary"), vmem_limit_bytes=VMEM_LIMIT),
        name="attention",
    )(nat, nat, nat, nat, nat, flat4, blk4, flat4, blk4, flat4,
      flat16, p16, flat16, p16, flat16, bias)


def _drain_rows(wait_one, n):
    def body(j, carry):
        for _ in range(ROW_UNROLL):
            wait_one()
        return carry

    lax.fori_loop(0, n // ROW_UNROLL, body, 0)


def _post_kernel(n_tiles, x_ref, u_ref, up_ref, att_ref, pw_ref, ps_ref, wo_ref, g2_ref,
                 rwh_ref, rwl_ref, rb_ref, x1_ref, gate_ref, dest_ref, alloc_ref, cnt_ref, xb_hbm,
                 ue, carry, curblk, nfree, hbuf, dbuf, dsm, fbuf, fsm, row_sems, idx_sem):
    i = pl.program_id(0)
    tm = POST_TM
    maxw = max(POOL_WINDOWS)
    inv_bm = 1.0 / MOE_BM
    blocks_of = lambda c: jnp.floor((c + (MOE_BM - 1)) * inv_bm)

    def row_copy(slot, j, dst_row):
        src = hbuf.at[slot, pl.ds(pl.multiple_of(j * ROW_TILE, ROW_TILE), ROW_TILE)]
        dst = xb_hbm.at[pl.ds(pl.multiple_of(dst_row * ROW_TILE, ROW_TILE), ROW_TILE)]
        return pltpu.make_async_copy(src, dst, row_sems.at[slot])

    @pl.when(i == 0)
    def _():
        carry[...] = jnp.zeros_like(carry)
        curblk[...] = jnp.zeros_like(curblk)
        nfree[...] = jnp.zeros_like(nfree)

    @pl.when(i < n_tiles)
    def _():
        hist = up_ref[...]
        ue[0:maxw] = jnp.where(i == 0, jnp.zeros_like(hist), hist)
        ue[maxw:] = u_ref[...]
        t_glob = i * tm + lax.broadcasted_iota(jnp.int32, (tm, LANES), 0)
        lane = lax.broadcasted_iota(jnp.int32, (tm, LANES), 1)
        halves = []
        for half, (w_lo, w_hi) in enumerate(((2, 4), (8, 16))):
            cols = slice(half * LANES, (half + 1) * LANES)
            s = ue[maxw:, cols]
            tok = s
            s_lo = None
            for j in range(1, w_hi):
                s = s + ue[maxw - j:maxw - j + tm, cols]
                if j == w_lo - 1:
                    s_lo = s
            cnt_lo = jnp.minimum(t_glob + 1, w_lo).astype(F32)
            cnt_hi = jnp.minimum(t_glob + 1, w_hi).astype(F32)
            halves.append(jnp.where(lane < POOL_GROUP, s_lo / cnt_lo, s / cnt_hi) - tok)
        pooled = jnp.concatenate(halves, axis=-1)
        mixed = jnp.dot(pooled.astype(BF16), pw_ref[...], preferred_element_type=F32) * ps_ref[...]

        mix = jnp.concatenate([mixed.astype(BF16)] + [att_ref[j] for j in range(N_PAIRS)],
                              axis=-1)
        x1 = x_ref[...] + jnp.dot(mix, wo_ref[...], preferred_element_type=F32)
        x1_ref[...] = x1
        h2 = _rms(x1, g2_ref[...])

        h2_hi = h2.astype(BF16)
        h2_lo = (h2 - h2_hi.astype(F32)).astype(BF16)
        logits = (jnp.dot(h2_hi, rwh_ref[...], preferred_element_type=F32)
                  + jnp.dot(h2_hi, rwl_ref[...], preferred_element_type=F32)
                  + jnp.dot(h2_lo, rwh_ref[...], preferred_element_type=F32)) + rb_ref[...]
        eidx = lax.broadcasted_iota(jnp.int32, (tm, N_EXPERTS), 1)
        vals = logits
        top_v, top_i = [], []
        for _ in range(TOP_K):
            m = jnp.max(vals, axis=-1, keepdims=True)
            idx = jnp.min(jnp.where(vals == m, eidx, N_EXPERTS), axis=-1, keepdims=True)
            top_v.append(m)
            top_i.append(idx)
            vals = jnp.where(eidx == idx, -jnp.inf, vals)
        ex = [jnp.exp(v - top_v[0]) for v in top_v]
        den = ex[0] + ex[1] + ex[2] + ex[3]

        hot = [(eidx == idx) for idx in top_i]
        cnt = sum(h.astype(F32) for h in hot)
        row = lax.broadcasted_iota(jnp.int32, (tm, tm), 0)
        col = lax.broadcasted_iota(jnp.int32, (tm, tm), 1)
        strict = jnp.where(col < row, 1.0, 0.0).astype(BF16)
        c0 = carry[...]
        before = jnp.dot(strict, cnt.astype(BF16), preferred_element_type=F32) + c0

        c1 = c0 + jnp.sum(cnt, axis=0, keepdims=True)
        nb0 = blocks_of(c0)
        opens = blocks_of(c1) - nb0
        er = lax.broadcasted_iota(jnp.int32, (N_EXPERTS, N_EXPERTS), 0)
        ec = lax.broadcasted_iota(jnp.int32, (N_EXPERTS, N_EXPERTS), 1)
        earlier = jnp.where(er < ec, 1.0, 0.0).astype(BF16)
        opened_before = jnp.dot(jnp.broadcast_to(opens, (8, N_EXPERTS)).astype(BF16), earlier,
                                preferred_element_type=F32)[0:1]
        new_blk = nfree[...] + opened_before
        cur = curblk[...]
        a_tok = jnp.floor(before * inv_bm)
        blk = jnp.where(a_tok == nb0 - 1.0, cur, new_blk)
        dest_full = blk * MOE_BM + (before - a_tok * MOE_BM)
        dests = [jnp.sum(jnp.where(h, dest_full, 0.0), axis=-1, keepdims=True) for h in hot]

        carry[...] = c1
        curblk[...] = jnp.where(opens > 0, new_blk, cur)
        nfree[...] = nfree[...] + jnp.sum(opens, axis=-1, keepdims=True)
        alloc_ref[...] = jnp.where(opens > 0, new_blk, -1.0).astype(jnp.int32)
        cnt_ref[...] = c1.astype(jnp.int32)

        k4 = lax.broadcasted_iota(jnp.int32, (tm, TOP_K), 1)

        def pack(cols):
            out = jnp.broadcast_to(cols[0], (tm, TOP_K))
            for k in range(1, TOP_K):
                out = jnp.where(k4 == k, jnp.broadcast_to(cols[k], (tm, TOP_K)), out)
            return out

        gate_ref[...] = pack([e / den for e in ex])
        dest = pack(dests).astype(jnp.int32)
        dest_ref[...] = dest

        @pl.when(i >= 2)
        def _():
            _drain_rows(lambda: row_copy(i % 2, 0, 0).wait(), tm * TOP_K)

        _store_row_tiles(hbuf.at[i % 2], h2)

    @pl.when(i > 0)
    def _():
        pltpu.make_async_copy(dbuf, dsm, idx_sem).wait()
        slot = (i - 1) % 2

        def issue(jo, c):
            for ji in range(ROW_UNROLL):
                j = jo * ROW_UNROLL + ji
                for k in range(TOP_K):
                    row_copy(slot, j, dsm[j, k]).start(priority=k % 2)
            return c

        lax.fori_loop(0, tm // ROW_UNROLL, issue, 0)

    @pl.when(i < n_tiles)
    def _():
        dbuf[...] = dest_ref[...]
        pltpu.make_async_copy(dbuf, dsm, idx_sem).start()

    @pl.when(i == n_tiles)
    def _():
        last = (n_tiles - 1) % 2
        _drain_rows(lambda: row_copy(last, 0, 0).wait(), tm * TOP_K)
        if n_tiles >= 2:
            _drain_rows(lambda: row_copy(1 - last, 0, 0).wait(), tm * TOP_K)

        c = carry[...]
        used_in_last = c - (blocks_of(c) - 1.0) * MOE_BM
        fbuf[0:1] = jnp.where(c > 0, curblk[...] * MOE_BM + used_in_last, 0.0).astype(jnp.int32)
        fbuf[1:2] = jnp.where(c > 0, (curblk[...] + 1.0) * MOE_BM, 0.0).astype(jnp.int32)
        fbuf[2:8] = jnp.broadcast_to(nfree[...], (6, N_EXPERTS)).astype(jnp.int32)
        fill = pltpu.make_async_copy(fbuf, fsm, idx_sem)
        fill.start()
        fill.wait()

        def per_expert(e, carry_):
            lo, hi = fsm[0, e], fsm[1, e]

            def start_row(s, c_):
                row_copy(last, 0, s).start()
                return c_

            def wait_row(s, c_):
                row_copy(last, 0, 0).wait()
                return c_

            lax.fori_loop(lo, hi, start_row, 0)
            lax.fori_loop(lo, hi, wait_row, 0)
            return carry_

        lax.fori_loop(0, N_EXPERTS, per_expert, 0)

        def block_copy(b):
            rows = MOE_BM * ROW_TILE
            return pltpu.make_async_copy(
                hbuf.at[last], xb_hbm.at[pl.ds(pl.multiple_of(b * rows, rows), rows)],
                row_sems.at[last])

        def start_block(b, c_):
            block_copy(b).start()
            return c_

        def wait_block(b, c_):
            block_copy(b).wait()
            return c_

        n_blocks = xb_hbm.shape[0] // (MOE_BM * ROW_TILE)
        lax.fori_loop(fsm[2, 0], n_blocks, start_block, 0)
        lax.fori_loop(fsm[2, 0], n_blocks, wait_block, 0)


def _post(x, u, att, pw_bd, pool_scale, wo_bf, g2, router_w, router_b, S):
    tm = POST_TM
    assert tm == MOE_BM and tm % ROW_UNROLL == 0
    n_tiles = S // tm
    n_blocks = (S * TOP_K) // MOE_BM + N_EXPERTS
    maxw = max(POOL_WINDOWS)
    tile = lambda i: jnp.minimum(i, n_tiles - 1)
    row = lambda w: pl.BlockSpec((tm, w), lambda i: (tile(i), 0))
    const = lambda a, b: pl.BlockSpec((a, b), lambda i: (0, 0))
    rw_hi = router_w.astype(BF16)
    rw_hi_rest = (router_w - rw_hi.astype(F32)).astype(BF16)
    vec = lambda: pltpu.VMEM((1, N_EXPERTS), F32)
    return pl.pallas_call(
        functools.partial(_post_kernel, n_tiles),
        grid=(n_tiles + 1,),
        in_specs=[
            row(D_MODEL),
            row(POOL_WIDTH),
            pl.BlockSpec((maxw, POOL_WIDTH),
                         lambda i: (jnp.maximum(tile(i) * (tm // maxw) - 1, 0), 0)),
            pl.BlockSpec((N_PAIRS, tm, LANES), lambda i: (0, tile(i), 0)),
            const(POOL_WIDTH, POOL_WIDTH),
            const(1, POOL_WIDTH),
            const(D_MODEL, D_MODEL),
            const(1, D_MODEL),
            const(D_MODEL, N_EXPERTS),
            const(D_MODEL, N_EXPERTS),
            const(1, N_EXPERTS),
        ],
        out_specs=[row(D_MODEL), row(TOP_K), row(TOP_K),
                   pl.BlockSpec((None, 1, N_EXPERTS), lambda i: (tile(i), 0, 0)),
                   const(1, N_EXPERTS),
                   pl.BlockSpec(memory_space=pl.ANY)],
        out_shape=[
            jax.ShapeDtypeStruct((S, D_MODEL), F32),
            jax.ShapeDtypeStruct((S, TOP_K), F32),
            jax.ShapeDtypeStruct((S, TOP_K), jnp.int32),
            jax.ShapeDtypeStruct((n_tiles, 1, N_EXPERTS), jnp.int32),
            jax.ShapeDtypeStruct((1, N_EXPERTS), jnp.int32),
            jax.ShapeDtypeStruct((n_blocks * MOE_BM * ROW_TILE, LANES), F32),
        ],
        scratch_shapes=[pltpu.VMEM((tm + maxw, POOL_WIDTH), F32),
                        vec(), vec(), vec(),
                        pltpu.VMEM((2, tm * ROW_TILE, LANES), F32),
                        pltpu.VMEM((tm, TOP_K), jnp.int32),
                        pltpu.SMEM((tm, TOP_K), jnp.int32),
                        pltpu.VMEM((8, N_EXPERTS), jnp.int32),
                        pltpu.SMEM((8, N_EXPERTS), jnp.int32),
                        pltpu.SemaphoreType.DMA((2,)),
                        pltpu.SemaphoreType.DMA(())],
        compiler_params=pltpu.CompilerParams(
            dimension_semantics=("arbitrary",), has_side_effects=True,
            vmem_limit_bytes=VMEM_LIMIT),
        name="post_attention",
    )(x, u, u, att, pw_bd, pool_scale, wo_bf, g2, rw_hi, rw_hi_rest, router_b)


def _expert_kernel(bexp_ref, nblk_ref, nexte_ref, slot_ref, order_ref, xb_ref, wgu_hbm, bgu_ref, wd_hbm,
                   bd_ref, y_ref, wgu_f32, wd_f32, wgu_bf, wd_bf, sems):
    b = pl.program_id(0)
    e = bexp_ref[b]
    prev = bexp_ref[jnp.maximum(b - 1, 0)]
    fresh = jnp.logical_and(jnp.logical_or(b == 0, e != prev), b < nblk_ref[0])

    def fetch(expert, slot):
        return (pltpu.make_async_copy(wgu_hbm.at[expert], wgu_f32.at[slot], sems.at[0, slot]),
                pltpu.make_async_copy(wd_hbm.at[expert], wd_f32.at[slot], sems.at[1, slot]))

    @pl.when(b == 0)
    def _():
        for cp in fetch(e, slot_ref[e]):
            cp.start()

    @pl.when(fresh)
    def _():
        slot = slot_ref[e]
        for cp in fetch(e, slot):
            cp.wait()
        nxt = nexte_ref[e]

        @pl.when(nxt < N_EXPERTS)
        def _():
            for cp in fetch(nxt, 1 - slot):
                cp.start()

        wgu_bf[...] = wgu_f32[slot].astype(BF16)
        wd_bf[...] = wd_f32[slot].astype(BF16)

    @pl.when(b < nblk_ref[0])
    def _():
        x = _load_row_tiles(xb_ref, MOE_BM).astype(BF16)
        gu = jnp.dot(x, wgu_bf[...], preferred_element_type=F32) + bgu_ref[...]
        gate = jnp.minimum(gu[:, :D_FF], SWIGLU_LIMIT)
        up = jnp.clip(gu[:, D_FF:], -SWIGLU_LIMIT, SWIGLU_LIMIT)
        act = (up + 1.0) * (gate * jax.nn.sigmoid(gate * SWIGLU_ALPHA))
        y = jnp.dot(act.astype(BF16), wd_bf[...], preferred_element_type=F32) + bd_ref[...]
        _store_row_tiles(y_ref, y)

    @pl.when(b >= nblk_ref[0])
    def _():
        y_ref[...] = jnp.zeros_like(y_ref)


def _experts(block_exp, nblk, next_expert, expert_slot, order, xb, w_gate_up, b_gate_up, w_down,
             b_down):
    nb = xb.shape[0] // (MOE_BM * ROW_TILE)
    tiles = (MOE_BM * ROW_TILE, LANES)
    blk = lambda b, be, nk, ne, sl, od: (od[jnp.minimum(b, nk[0] - 1)], 0)
    wsel = lambda b, be, nk, *_: (be[jnp.minimum(b, nk[0] - 1)], 0, 0)
    return pl.pallas_call(
        _expert_kernel,
        grid_spec=pltpu.PrefetchScalarGridSpec(
            num_scalar_prefetch=5,
            grid=(nb,),
            in_specs=[
                pl.BlockSpec(tiles, blk),
                pl.BlockSpec(memory_space=pl.ANY),
                pl.BlockSpec((None, 1, 2 * D_FF), wsel),
                pl.BlockSpec(memory_space=pl.ANY),
                pl.BlockSpec((None, 1, D_MODEL), wsel),
            ],
            out_specs=pl.BlockSpec(tiles, lambda b, be, nk, ne, sl, od: (od[b], 0)),
            scratch_shapes=[pltpu.VMEM((2, D_MODEL, 2 * D_FF), F32),
                            pltpu.VMEM((2, D_FF, D_MODEL), F32),
                            pltpu.VMEM((D_MODEL, 2 * D_FF), BF16),
                            pltpu.VMEM((D_FF, D_MODEL), BF16),
                            pltpu.SemaphoreType.DMA((2, 2))],
        ),
        out_shape=jax.ShapeDtypeStruct(xb.shape, F32),
        compiler_params=pltpu.CompilerParams(
            dimension_semantics=("arbitrary",), vmem_limit_bytes=VMEM_LIMIT),
        name="moe_experts",
    )(block_exp, nblk, next_expert, expert_slot, order, xb, w_gate_up, b_gate_up, w_down, b_down)


def _combine_kernel(dest_ref, x1_ref, gate_ref, g_ref, y_hbm, o_ref, ybuf, sems):
    i = pl.program_id(0)
    n = pl.num_programs(0)
    tb = COMBINE_TB

    def row_copy(src_row, slot, k, j):
        src = y_hbm.at[pl.ds(pl.multiple_of(src_row * ROW_TILE, ROW_TILE), ROW_TILE)]
        dst = ybuf.at[slot, k, pl.ds(pl.multiple_of(j * ROW_TILE, ROW_TILE), ROW_TILE)]
        return pltpu.make_async_copy(src, dst, sems.at[slot])

    def gather(step, slot):
        def issue(jo, carry):
            for ji in range(ROW_UNROLL):
                j = jo * ROW_UNROLL + ji
                for k in range(TOP_K):
                    row_copy(dest_ref[(step * tb + j) * TOP_K + k], slot, k, j).start(
                        priority=k % 2)
            return carry

        lax.fori_loop(0, tb // ROW_UNROLL, issue, 0)

    slot = i % 2

    @pl.when(i == 0)
    def _():
        gather(0, 0)

    @pl.when(i + 1 < n)
    def _():
        gather(i + 1, 1 - slot)

    _drain_rows(lambda: row_copy(0, slot, 0, 0).wait(), tb * TOP_K)

    gates = gate_ref[...]
    moe = gates[:, 0:1] * _load_row_tiles(ybuf.at[slot, 0], tb)
    for k in range(1, TOP_K):
        moe = moe + gates[:, k:k + 1] * _load_row_tiles(ybuf.at[slot, k], tb)
    o_ref[...] = _rms(x1_ref[...] + moe, g_ref[...])


def _combine(dest_flat, x1, gates, final_g, y):
    N = x1.shape[0]
    tb = COMBINE_TB
    return pl.pallas_call(
        _combine_kernel,
        grid_spec=pltpu.PrefetchScalarGridSpec(
            num_scalar_prefetch=1,
            grid=(N // tb,),
            in_specs=[
                pl.BlockSpec((tb, D_MODEL), lambda i, *_: (i, 0)),
                pl.BlockSpec((tb, TOP_K), lambda i, *_: (i, 0)),
                pl.BlockSpec((1, D_MODEL), lambda i, *_: (0, 0)),
                pl.BlockSpec(memory_space=pl.ANY),
            ],
            out_specs=pl.BlockSpec((tb, D_MODEL), lambda i, *_: (i, 0)),
            scratch_shapes=[pltpu.VMEM((2, TOP_K, tb * ROW_TILE, LANES), F32),
                            pltpu.SemaphoreType.DMA((2,))],
        ),
        out_shape=jax.ShapeDtypeStruct((N, D_MODEL), F32),
        compiler_params=pltpu.CompilerParams(
            dimension_semantics=("arbitrary",), vmem_limit_bytes=VMEM_LIMIT),
        name="moe_combine",
    )(dest_flat, x1, gates, final_g, y)


def _layer(x2, norm1_g, w_in, pool_w, pool_scale, rel_bias, w_out, norm2_g,
           router_w, router_b, w_gate_up, b_gate_up, w_down, b_down, final_g):
    S = x2.shape[0]
    u, nat, p4, p16 = _inproj(x2, norm1_g.reshape(1, D_MODEL), w_in.astype(BF16), S)
    att = _attention(nat, p4, p16, _bias_tables(rel_bias), S)

    n_g = len(POOL_WINDOWS)
    pw_bd = (jnp.eye(n_g, dtype=F32)[:, None, :, None] * pool_w[:, :, None, :]).reshape(
        POOL_WIDTH, POOL_WIDTH).astype(BF16)
    x1, gates, dest, alloc, counts, xb = _post(
        x2, u, att, pw_bd, pool_scale.reshape(1, POOL_WIDTH), w_out.astype(BF16),
        norm2_g.reshape(1, D_MODEL), router_w, router_b.reshape(1, N_EXPERTS), S)

    counts = counts.reshape(N_EXPERTS)
    blocks_per = (counts + MOE_BM - 1) // MOE_BM
    block_end = jnp.cumsum(blocks_per)
    nb = (S * TOP_K) // MOE_BM + N_EXPERTS
    step = jnp.arange(nb, dtype=jnp.int32)
    block_exp = jnp.minimum(jnp.sum(block_end[None, :] <= step[:, None], axis=1),
                            N_EXPERTS - 1).astype(jnp.int32)
    nblk = block_end[-1:].astype(jnp.int32)
    handed = alloc.reshape(-1, N_EXPERTS).T.reshape(-1)
    used = handed >= 0
    pos = jnp.cumsum(used) - 1
    order = jnp.sum(jnp.where(used[None, :] & (pos[None, :] == step[:, None]), handed[None, :], 0),
                    axis=1)
    order = jnp.where(step < nblk[0], order, step).astype(jnp.int32)
    expert_ids = jnp.arange(N_EXPERTS, dtype=jnp.int32)
    nonempty = blocks_per > 0
    first_at_or_after = jnp.flip(lax.cummin(jnp.flip(jnp.where(nonempty, expert_ids, N_EXPERTS))))
    next_expert = jnp.concatenate(
        [first_at_or_after[1:], jnp.full((1,), N_EXPERTS, jnp.int32)]).astype(jnp.int32)
    expert_slot = ((jnp.cumsum(nonempty) - nonempty) % 2).astype(jnp.int32)

    y = _experts(block_exp, nblk, next_expert, expert_slot, order, xb, w_gate_up,
                 b_gate_up.reshape(N_EXPERTS, 1, 2 * D_FF), w_down,
                 b_down.reshape(N_EXPERTS, 1, D_MODEL))
    return _combine(dest.reshape(S * TOP_K), x1, gates, final_g.reshape(1, D_MODEL), y)


def kernel(x, norm1_g, w_in, pool_w, pool_scale, rel_bias, w_out, norm2_g, router_w, router_b,
           w_gate_up, b_gate_up, w_down, b_down, final_g):
    B, S, D = x.shape
    assert B == 1 and D == D_MODEL and S % ATT_TILE == 0, x.shape
    out = _layer(x.reshape(B * S, D), norm1_g[0], w_in[0], pool_w[0], pool_scale[0], rel_bias,
                 w_out[0], norm2_g[0], router_w[0], router_b[0], w_gate_up[0], b_gate_up[0],
                 w_down[0], b_down[0], final_g)
    return out.reshape(B, S, D)
```

```python
import functools
import math

import jax
import jax.numpy as jnp
from jax import lax
from jax.experimental import pallas as pl
from jax.experimental.pallas import tpu as pltpu

F32 = jnp.float32
BF16 = jnp.bfloat16

D_MODEL = 1024
POOL_WIDTH = 256
POOL_WINDOWS = (2, 4, 8, 16)
POOL_GROUP = 64
ATT_WIDTH = 768
HEAD_DIM = 64
N_HEADS = 12
N_PAIRS = N_HEADS // 2
DILATIONS = (1, 4, 16)
ATT_BLOCK = 128
N_BUCKETS = 32
MAX_DISTANCE = 2048
N_EXPERTS = 32
TOP_K = 4
D_FF = 1024
SWIGLU_LIMIT = 7.0
SWIGLU_ALPHA = 1.702
RMS_EPS = 1e-5
NEG_INF = -1e30
LOG2E = math.log2(math.e)

LANES = 128
ATT_TILE = 16 * ATT_BLOCK
BLOCKS_PER_TILE = ATT_TILE // ATT_BLOCK
N_UNITS = len(DILATIONS) * BLOCKS_PER_TILE
ATT_GROUP = 4
QKV_SLABS = 3 * N_PAIRS
IN_TM = 512
POST_TM = 256
MOE_BM = 256
COMBINE_TB = 256
ROW_UNROLL = 8
ROW_BUFS = 3
VMEM_LIMIT = 56 * 1024 * 1024

_CLASS_ROWS = tuple(ATT_TILE // d for d in DILATIONS)
_SEG_ROWS = tuple(ATT_BLOCK + r for r in _CLASS_ROWS)
_SEG_BASE = (0, _SEG_ROWS[0], _SEG_ROWS[0] + DILATIONS[1] * _SEG_ROWS[1])
KV_ROWS = _SEG_BASE[2] + DILATIONS[2] * _SEG_ROWS[2]


def _rms(x, g):
    return x * lax.rsqrt(jnp.mean(x * x, axis=-1, keepdims=True) + RMS_EPS) * g


ROW_TILE = D_MODEL // LANES


def _store_row_tiles(ref, x):
    n = x.shape[0]
    for c in range(ROW_TILE):
        ref[pl.ds(c, n, stride=ROW_TILE), :] = x[:, c * LANES:(c + 1) * LANES]


def _load_row_tiles(ref, n):
    return jnp.concatenate(
        [ref[pl.ds(c, n, stride=ROW_TILE), :] for c in range(ROW_TILE)], axis=-1)


def _inproj_kernel(x_ref, g_ref, w_ref, u_ref, nat_ref, p4_ref, p16_ref, h_ref, acc_ref):
    h_ref[...] = _rms(x_ref[...], g_ref[...]).astype(BF16)
    u_ref[...] = jnp.dot(h_ref[...], w_ref[:, :POOL_WIDTH], preferred_element_type=F32)
    for jj in range(QKV_SLABS // 2):
        c0 = POOL_WIDTH + 2 * jj * LANES
        res = jnp.dot(h_ref[...], w_ref[:, c0:c0 + 2 * LANES], preferred_element_type=F32)
        for s in (0, 1):
            j = 2 * jj + s
            slab = res[:, s * LANES:(s + 1) * LANES]
            if j < N_PAIRS:
                slab = slab * (HEAD_DIM ** -0.5 * LOG2E)
            acc_ref[j] = slab
            nat_ref[j] = slab.astype(BF16)
    for j in range(QKV_SLABS):
        for d, ref in ((DILATIONS[1], p4_ref), (DILATIONS[2], p16_ref)):
            for r in range(d):
                ref[j, r] = acc_ref[j, pl.ds(r, IN_TM // d, stride=d), :].astype(BF16)


def _inproj(x, g, w_bf, S):
    sub = ATT_TILE // IN_TM
    n_tiles = S // ATT_TILE

    def res_spec(d):
        return pl.BlockSpec((QKV_SLABS, None, d, IN_TM // d, LANES),
                            lambda i: (0, i // sub, 0, i % sub, 0))

    def res_shape(d):
        return jax.ShapeDtypeStruct((QKV_SLABS, n_tiles, d, ATT_TILE // d, LANES), BF16)

    return pl.pallas_call(
        _inproj_kernel,
        grid=(S // IN_TM,),
        in_specs=[
            pl.BlockSpec((IN_TM, D_MODEL), lambda i: (i, 0)),
            pl.BlockSpec((1, D_MODEL), lambda i: (0, 0)),
            pl.BlockSpec((D_MODEL, POOL_WIDTH + 3 * ATT_WIDTH), lambda i: (0, 0)),
        ],
        out_specs=[
            pl.BlockSpec((IN_TM, POOL_WIDTH), lambda i: (i, 0)),
            pl.BlockSpec((QKV_SLABS, IN_TM, LANES), lambda i: (0, i, 0)),
            res_spec(DILATIONS[1]),
            res_spec(DILATIONS[2]),
        ],
        out_shape=[
            jax.ShapeDtypeStruct((S, POOL_WIDTH), F32),
            jax.ShapeDtypeStruct((QKV_SLABS, S, LANES), BF16),
            res_shape(DILATIONS[1]),
            res_shape(DILATIONS[2]),
        ],
        scratch_shapes=[pltpu.VMEM((IN_TM, D_MODEL), BF16),
                        pltpu.VMEM((QKV_SLABS, IN_TM, LANES), F32)],
        compiler_params=pltpu.CompilerParams(
            dimension_semantics=("arbitrary",), vmem_limit_bytes=VMEM_LIMIT),
        name="inproj",
    )(x, g, w_bf)


def _t5_bucket(dist):
    max_exact = N_BUCKETS // 2
    is_small = dist < max_exact
    nf = jnp.maximum(dist, 1).astype(jnp.float32)
    large = max_exact + (jnp.log(nf / max_exact) / math.log(MAX_DISTANCE / max_exact)
                         * (N_BUCKETS - max_exact)).astype(jnp.int32)
    large = jnp.minimum(large, N_BUCKETS - 1)
    return jnp.where(is_small, dist, large)


def _bias_kernel(ids_ref, rb_ref, o_ref):
    h = pl.program_id(0)
    col = lax.broadcasted_iota(jnp.int32, (ATT_BLOCK, 2 * ATT_BLOCK), 1)
    for b in range(len(DILATIONS)):
        ids = ids_ref[b]
        tab = jnp.full(ids.shape, NEG_INF, F32)
        for k in range(N_BUCKETS):
            tab = jnp.where(ids == k, rb_ref[k, h] * LOG2E, tab)
        o_ref[0, b] = tab
        o_ref[1, b] = jnp.where(col < ATT_BLOCK, NEG_INF, tab)


def _bias_tables(rel_bias):
    qi = jnp.arange(ATT_BLOCK)[:, None]
    kj = jnp.arange(2 * ATT_BLOCK)[None, :]
    dist = qi + ATT_BLOCK - kj
    ok = (dist >= 0) & (dist <= ATT_BLOCK)
    ids = jnp.stack([jnp.where(ok, _t5_bucket(jnp.clip(dist, 0) * d), -1) for d in DILATIONS])
    return pl.pallas_call(
        _bias_kernel,
        grid=(N_HEADS,),
        in_specs=[
            pl.BlockSpec((len(DILATIONS), ATT_BLOCK, 2 * ATT_BLOCK), lambda h: (0, 0, 0)),
            pl.BlockSpec(memory_space=pltpu.SMEM),
        ],
        out_specs=pl.BlockSpec((2, len(DILATIONS), None, ATT_BLOCK, 2 * ATT_BLOCK),
                               lambda h: (0, 0, h, 0, 0)),
        out_shape=jax.ShapeDtypeStruct(
            (2, len(DILATIONS), N_HEADS, ATT_BLOCK, 2 * ATT_BLOCK), F32),
        compiler_params=pltpu.CompilerParams(dimension_semantics=("arbitrary",)),
        name="bias_tables",
    )(ids.astype(jnp.int32), rel_bias.astype(F32))


def _attn_kernel(qn, knp, kn, vnp, vn, q4, k4p, k4, v4p, v4, q16, k16p, k16, v16p, v16,
                 bias_ref, o_ref, qbuf, kbuf, vbuf0, vbuf1, sbuf0, sbuf1, pbuf0, pbuf1,
                 obuf, mbuf, lbuf, tbuf):
    i = pl.program_id(0)
    at_start = i == 0
    G = ATT_GROUP
    n_groups = N_UNITS // G
    sbufs, pbufs, vbufs = (sbuf0, sbuf1), (pbuf0, pbuf1), (vbuf0, vbuf1)
    lane = lax.broadcasted_iota(jnp.int32, (ATT_BLOCK, LANES), 1)
    lo = lane < HEAD_DIM
    head_mask = (jnp.where(lo, 1.0, 0.0).astype(BF16), jnp.where(lo, 0.0, 1.0).astype(BF16))

    @pl.when(jnp.logical_and(at_start, pl.program_id(1) == 0))
    def _():
        for buf in sbufs + pbufs:
            buf[...] = jnp.zeros_like(buf)

    for b, q in enumerate((qn, q4, q16)):
        qbuf[b * ATT_TILE:(b + 1) * ATT_TILE] = q[...]

    def fill_k(dst0, rows, val):
        kbuf[dst0:dst0 + rows] = val

    def fill_v(dst0, rows, val):
        for a in (0, 1):
            keep = head_mask[a][0:1]
            vbufs[a][dst0:dst0 + rows] = val * keep + (1 - keep)

    for fill, prevs, curs in ((fill_k, (knp, k4p, k16p), (kn, k4, k16)),
                              (fill_v, (vnp, v4p, v16p), (vn, v4, v16))):
        fill(0, ATT_BLOCK, prevs[0][...])
        fill(ATT_BLOCK, ATT_TILE, curs[0][...])
        for b in (1, 2):
            rows = _CLASS_ROWS[b]
            for c in range(DILATIONS[b]):
                base = _SEG_BASE[b] + c * _SEG_ROWS[b]
                fill(base, ATT_BLOCK, prevs[b][c])
                fill(base + ATT_BLOCK, rows, curs[b][c * rows:(c + 1) * rows])

    def geometry(u):
        b = u // BLOCKS_PER_TILE
        w = u % BLOCKS_PER_TILE
        sh = 4 - 2 * b
        c = lax.shift_right_logical(w, sh)
        a = w - lax.shift_left(c, sh)
        seg = ATT_BLOCK + lax.shift_right_logical(ATT_TILE, 2 * b)
        base = b * _SEG_BASE[1] + jnp.where(b == 2, _SEG_BASE[2] - 2 * _SEG_BASE[1], 0)
        krow = pl.multiple_of(base + c * seg + a * ATT_BLOCK, ATT_BLOCK)
        variant = jnp.logical_and(at_start, a == 0).astype(jnp.int32)
        return b, krow, variant

    def scores(g, sbuf):
        for j in range(G):
            u = g * G + j
            b, krow, variant = geometry(u)
            q = qbuf[pl.ds(pl.multiple_of(u * ATT_BLOCK, ATT_BLOCK), ATT_BLOCK), :]
            k = kbuf[pl.ds(krow, 2 * ATT_BLOCK), :]
            for a in (0, 1):
                s = lax.dot_general(q * head_mask[a], k, (((1,), (1,)), ((), ())),
                                    preferred_element_type=F32)
                sbuf[j, a] = s + bias_ref[variant, b, a]

    def softmax(g, sbuf, pbuf):
        for j in range(G):
            rows = pl.ds(pl.multiple_of((g * G + j) * ATT_BLOCK, ATT_BLOCK), ATT_BLOCK)
            maxes = []
            for a in (0, 1):
                s = sbuf[j, a]
                m = jnp.max(s, axis=-1, keepdims=True)
                pbuf[j, a] = jnp.exp2(s - m).astype(BF16)
                maxes.append(jnp.broadcast_to(m, (ATT_BLOCK, LANES)))
            mbuf[rows, :] = jnp.where(lo, maxes[0], maxes[1])

    def values(g, pbuf):
        for j in range(G):
            u = g * G + j
            _, krow, _ = geometry(u)
            rows = pl.ds(pl.multiple_of(u * ATT_BLOCK, ATT_BLOCK), ATT_BLOCK)
            o0 = jnp.dot(pbuf[j, 0], vbuf0[pl.ds(krow, 2 * ATT_BLOCK), :],
                         preferred_element_type=F32)
            o1 = jnp.dot(pbuf[j, 1], vbuf1[pl.ds(krow, 2 * ATT_BLOCK), :],
                         preferred_element_type=F32)
            obuf[rows, :] = jnp.where(lo, o0, o1)
            lbuf[rows, :] = pltpu.roll(jnp.where(lo, o1, o0), HEAD_DIM, axis=1)

    def trip_pair(tt, carry):
        clamp = lambda g: jnp.clip(g, 0, n_groups - 1)
        for par in (0, 1):
            t = 2 * tt + par
            values(clamp(t - 2), pbufs[par])
            scores(clamp(t), sbufs[par])
            softmax(clamp(t - 1), sbufs[1 - par], pbufs[1 - par])
        return carry

    assert n_groups % 2 == 0
    lax.fori_loop(0, (n_groups + 2) // 2, trip_pair, 0)

    for b in (1, 2):
        d, rows = DILATIONS[b], _CLASS_ROWS[b]
        for r in range(d):
            src = slice(b * ATT_TILE + r * rows, b * ATT_TILE + (r + 1) * rows)
            dst = pl.ds(r, rows, stride=d)
            for n, buf in enumerate((obuf, mbuf, lbuf)):
                tbuf[3 * (b - 1) + n, dst, :] = buf[src]

    def merge(c, carry):
        rows = pl.ds(pl.multiple_of(c * ATT_BLOCK, ATT_BLOCK), ATT_BLOCK)
        ms = (mbuf[rows, :], tbuf[1, rows, :], tbuf[4, rows, :])
        os_ = (obuf[rows, :], tbuf[0, rows, :], tbuf[3, rows, :])
        ls = (lbuf[rows, :], tbuf[2, rows, :], tbuf[5, rows, :])
        mm = jnp.maximum(jnp.maximum(ms[0], ms[1]), ms[2])
        ws = [jnp.exp2(m - mm) for m in ms]
        num = ws[0] * os_[0] + ws[1] * os_[1] + ws[2] * os_[2]
        den = ws[0] * ls[0] + ws[1] * ls[1] + ws[2] * ls[2]
        o_ref[rows, :] = (num / den).astype(o_ref.dtype)
        return carry

    lax.fori_loop(0, BLOCKS_PER_TILE, merge, 0)


def _attention(nat, p4, p16, bias, S):
    n_tiles = S // ATT_TILE
    d4, d16 = DILATIONS[1], DILATIONS[2]
    flat4 = p4.reshape(QKV_SLABS, S, LANES)
    flat16 = p16.reshape(QKV_SLABS, S, LANES)
    blk4 = p4.reshape(QKV_SLABS, n_tiles, d4, _CLASS_ROWS[1] // ATT_BLOCK, ATT_BLOCK, LANES)
    prev = lambda i: jnp.maximum(i - 1, 0)

    def cur(off):
        return pl.BlockSpec((None, ATT_TILE, LANES), lambda i, p: (off + p, i, 0))

    def prev_nat(off):
        return pl.BlockSpec((None, ATT_BLOCK, LANES),
                            lambda i, p: (off + p, jnp.maximum(i * BLOCKS_PER_TILE - 1, 0), 0))

    def prev4(off):
        last = _CLASS_ROWS[1] // ATT_BLOCK - 1
        return pl.BlockSpec((None, None, d4, None, ATT_BLOCK, LANES),
                            lambda i, p: (off + p, prev(i), 0, last, 0, 0))

    def prev16(off):
        return pl.BlockSpec((None, None, d16, ATT_BLOCK, LANES),
                            lambda i, p: (off + p, prev(i), 0, 0, 0))

    k_off, v_off = N_PAIRS, 2 * N_PAIRS
    return pl.pallas_call(
        _attn_kernel,
        grid=(n_tiles, N_PAIRS),
        in_specs=[
            cur(0), prev_nat(k_off), cur(k_off), prev_nat(v_off), cur(v_off),
            cur(0), prev4(k_off), cur(k_off), prev4(v_off), cur(v_off),
            cur(0), prev16(k_off), cur(k_off), prev16(v_off), cur(v_off),
            pl.BlockSpec((2, len(DILATIONS), 2, ATT_BLOCK, 2 * ATT_BLOCK),
                         lambda i, p: (0, 0, p, 0, 0)),
        ],
        out_specs=pl.BlockSpec((None, ATT_TILE, LANES), lambda i, p: (p, i, 0)),
        out_shape=jax.ShapeDtypeStruct((N_PAIRS, S, LANES), BF16),
        scratch_shapes=[
            pltpu.VMEM((N_UNITS * ATT_BLOCK, LANES), BF16),
            pltpu.VMEM((KV_ROWS, LANES), BF16),
            pltpu.VMEM((KV_ROWS, LANES), BF16),
            pltpu.VMEM((KV_ROWS, LANES), BF16),
            pltpu.VMEM((ATT_GROUP, 2, ATT_BLOCK, 2 * ATT_BLOCK), F32),
            pltpu.VMEM((ATT_GROUP, 2, ATT_BLOCK, 2 * ATT_BLOCK), F32),
            pltpu.VMEM((ATT_GROUP, 2, ATT_BLOCK, 2 * ATT_BLOCK), BF16),
            pltpu.VMEM((ATT_GROUP, 2, ATT_BLOCK, 2 * ATT_BLOCK), BF16),
            pltpu.VMEM((N_UNITS * ATT_BLOCK, LANES), F32),
            pltpu.VMEM((N_UNITS * ATT_BLOCK, LANES), F32),
            pltpu.VMEM((N_UNITS * ATT_BLOCK, LANES), F32),
            pltpu.VMEM((6, ATT_TILE, LANES), F32),
        ],
        compiler_params=pltpu.CompilerParams(
            dimension_semantics=("arbitrary", "arbitrary"), vmem_limit_bytes=VMEM_LIMIT),
        name="attention",
    )(nat, nat, nat, nat, nat, flat4, blk4, flat4, blk4, flat4,
      flat16, p16, flat16, p16, flat16, bias)


def _drain_rows(wait_one, n):
    def body(j, carry):
        for _ in range(ROW_UNROLL):
            wait_one()
        return carry

    lax.fori_loop(0, n // ROW_UNROLL, body, 0)


def _post_kernel(n_tiles, x_ref, u_ref, up_ref, att_ref, pw_ref, ps_ref, wo_ref, g2_ref,
                 rwh_ref, rwl_ref, rb_ref, x1_ref, gate_ref, dest_ref, alloc_ref, cnt_ref, xb_hbm,
                 ue, carry, curblk, nfree, hbuf, dbuf, dsm, fbuf, fsm, row_sems, idx_sem):
    i = pl.program_id(0)
    tm = POST_TM
    maxw = max(POOL_WINDOWS)
    inv_bm = 1.0 / MOE_BM
    blocks_of = lambda c: jnp.floor((c + (MOE_BM - 1)) * inv_bm)

    def row_copy(slot, j, dst_row):
        src = hbuf.at[slot, pl.ds(pl.multiple_of(j * ROW_TILE, ROW_TILE), ROW_TILE)]
        dst = xb_hbm.at[pl.ds(pl.multiple_of(dst_row * ROW_TILE, ROW_TILE), ROW_TILE)]
        return pltpu.make_async_copy(src, dst, row_sems.at[slot])

    @pl.when(i == 0)
    def _():
        carry[...] = jnp.zeros_like(carry)
        curblk[...] = jnp.zeros_like(curblk)
        nfree[...] = jnp.zeros_like(nfree)

    @pl.when(i < n_tiles)
    def _():
        hist = up_ref[...]
        ue[0:maxw] = jnp.where(i == 0, jnp.zeros_like(hist), hist)
        ue[maxw:] = u_ref[...]
        t_glob = i * tm + lax.broadcasted_iota(jnp.int32, (tm, LANES), 0)
        lane = lax.broadcasted_iota(jnp.int32, (tm, LANES), 1)
        halves = []
        for half, (w_lo, w_hi) in enumerate(((2, 4), (8, 16))):
            cols = slice(half * LANES, (half + 1) * LANES)
            s = ue[maxw:, cols]
            tok = s
            s_lo = None
            for j in range(1, w_hi):
                s = s + ue[maxw - j:maxw - j + tm, cols]
                if j == w_lo - 1:
                    s_lo = s
            cnt_lo = jnp.minimum(t_glob + 1, w_lo).astype(F32)
            cnt_hi = jnp.minimum(t_glob + 1, w_hi).astype(F32)
            halves.append(jnp.where(lane < POOL_GROUP, s_lo / cnt_lo, s / cnt_hi) - tok)
        pooled = jnp.concatenate(halves, axis=-1)
        mixed = jnp.dot(pooled.astype(BF16), pw_ref[...], preferred_element_type=F32) * ps_ref[...]

        mix = jnp.concatenate([mixed.astype(BF16)] + [att_ref[j] for j in range(N_PAIRS)],
                              axis=-1)
        x1 = x_ref[...] + jnp.dot(mix, wo_ref[...], preferred_element_type=F32)
        x1_ref[...] = x1
        h2 = _rms(x1, g2_ref[...])

        h2_hi = h2.astype(BF16)
        h2_lo = (h2 - h2_hi.astype(F32)).astype(BF16)
        logits = (jnp.dot(h2_hi, rwh_ref[...], preferred_element_type=F32)
                  + jnp.dot(h2_hi, rwl_ref[...], preferred_element_type=F32)
                  + jnp.dot(h2_lo, rwh_ref[...], preferred_element_type=F32)) + rb_ref[...]
        eidx = lax.broadcasted_iota(jnp.int32, (tm, N_EXPERTS), 1)
        vals = logits
        top_v, top_i = [], []
        for _ in range(TOP_K):
            m = jnp.max(vals, axis=-1, keepdims=True)
            idx = jnp.min(jnp.where(vals == m, eidx, N_EXPERTS), axis=-1, keepdims=True)
            top_v.append(m)
            top_i.append(idx)
            vals = jnp.where(eidx == idx, -jnp.inf, vals)
        ex = [jnp.exp(v - top_v[0]) for v in top_v]
        den = ex[0] + ex[1] + ex[2] + ex[3]

        hot = [(eidx == idx) for idx in top_i]
        cnt = sum(h.astype(F32) for h in hot)
        row = lax.broadcasted_iota(jnp.int32, (tm, tm), 0)
        col = lax.broadcasted_iota(jnp.int32, (tm, tm), 1)
        strict = jnp.where(col < row, 1.0, 0.0).astype(BF16)
        c0 = carry[...]
        before = jnp.dot(strict, cnt.astype(BF16), preferred_element_type=F32) + c0

        c1 = c0 + jnp.sum(cnt, axis=0, keepdims=True)
        nb0 = blocks_of(c0)
        opens = blocks_of(c1) - nb0
        er = lax.broadcasted_iota(jnp.int32, (N_EXPERTS, N_EXPERTS), 0)
        ec = lax.broadcasted_iota(jnp.int32, (N_EXPERTS, N_EXPERTS), 1)
        earlier = jnp.where(er < ec, 1.0, 0.0).astype(BF16)
        opened_before = jnp.dot(jnp.broadcast_to(opens, (8, N_EXPERTS)).astype(BF16), earlier,
                                preferred_element_type=F32)[0:1]
        new_blk = nfree[...] + opened_before
        cur = curblk[...]
        a_tok = jnp.floor(before * inv_bm)
        blk = jnp.where(a_tok == nb0 - 1.0, cur, new_blk)
        dest_full = blk * MOE_BM + (before - a_tok * MOE_BM)
        dests = [jnp.sum(jnp.where(h, dest_full, 0.0), axis=-1, keepdims=True) for h in hot]

        carry[...] = c1
        curblk[...] = jnp.where(opens > 0, new_blk, cur)
        nfree[...] = nfree[...] + jnp.sum(opens, axis=-1, keepdims=True)
        alloc_ref[...] = jnp.where(opens > 0, new_blk, -1.0).astype(jnp.int32)
        cnt_ref[...] = c1.astype(jnp.int32)

        k4 = lax.broadcasted_iota(jnp.int32, (tm, TOP_K), 1)

        def pack(cols):
            out = jnp.broadcast_to(cols[0], (tm, TOP_K))
            for k in range(1, TOP_K):
                out = jnp.where(k4 == k, jnp.broadcast_to(cols[k], (tm, TOP_K)), out)
            return out

        gate_ref[...] = pack([e / den for e in ex])
        dest = pack(dests).astype(jnp.int32)
        dest_ref[...] = dest

        @pl.when(i >= ROW_BUFS)
        def _():
            _drain_rows(lambda: row_copy(i % ROW_BUFS, 0, 0).wait(), tm * TOP_K)

        _store_row_tiles(hbuf.at[i % ROW_BUFS], h2)

    @pl.when(i > 0)
    def _():
        pltpu.make_async_copy(dbuf, dsm, idx_sem).wait()
        slot = (i - 1) % ROW_BUFS

        def issue(jo, c):
            for ji in range(ROW_UNROLL):
                j = jo * ROW_UNROLL + ji
                for k in range(TOP_K):
                    row_copy(slot, j, dsm[j, k]).start(priority=k % 2)
            return c

        lax.fori_loop(0, tm // ROW_UNROLL, issue, 0)

    @pl.when(i < n_tiles)
    def _():
        dbuf[...] = dest_ref[...]
        pltpu.make_async_copy(dbuf, dsm, idx_sem).start()

    @pl.when(i == n_tiles)
    def _():
        last = (n_tiles - 1) % ROW_BUFS
        for back in range(min(ROW_BUFS, n_tiles)):
            s = (n_tiles - 1 - back) % ROW_BUFS
            _drain_rows(lambda: row_copy(s, 0, 0).wait(), tm * TOP_K)

        c = carry[...]
        used_in_last = c - (blocks_of(c) - 1.0) * MOE_BM
        fbuf[0:1] = jnp.where(c > 0, curblk[...] * MOE_BM + used_in_last, 0.0).astype(jnp.int32)
        fbuf[1:2] = jnp.where(c > 0, (curblk[...] + 1.0) * MOE_BM, 0.0).astype(jnp.int32)
        fbuf[2:8] = jnp.broadcast_to(nfree[...], (6, N_EXPERTS)).astype(jnp.int32)
        fill = pltpu.make_async_copy(fbuf, fsm, idx_sem)
        fill.start()
        fill.wait()

        def per_expert(e, carry_):
            lo, hi = fsm[0, e], fsm[1, e]

            def start_row(s, c_):
                row_copy(last, 0, s).start()
                return c_

            def wait_row(s, c_):
                row_copy(last, 0, 0).wait()
                return c_

            lax.fori_loop(lo, hi, start_row, 0)
            lax.fori_loop(lo, hi, wait_row, 0)
            return carry_

        lax.fori_loop(0, N_EXPERTS, per_expert, 0)

        def block_copy(b):
            rows = MOE_BM * ROW_TILE
            return pltpu.make_async_copy(
                hbuf.at[last], xb_hbm.at[pl.ds(pl.multiple_of(b * rows, rows), rows)],
                row_sems.at[last])

        def start_block(b, c_):
            block_copy(b).start()
            return c_

        def wait_block(b, c_):
            block_copy(b).wait()
            return c_

        n_blocks = xb_hbm.shape[0] // (MOE_BM * ROW_TILE)
        lax.fori_loop(fsm[2, 0], n_blocks, start_block, 0)
        lax.fori_loop(fsm[2, 0], n_blocks, wait_block, 0)


def _post(x, u, att, pw_bd, pool_scale, wo_bf, g2, router_w, router_b, S):
    tm = POST_TM
    assert tm == MOE_BM and tm % ROW_UNROLL == 0
    n_tiles = S // tm
    n_blocks = (S * TOP_K) // MOE_BM + N_EXPERTS
    maxw = max(POOL_WINDOWS)
    tile = lambda i: jnp.minimum(i, n_tiles - 1)
    row = lambda w: pl.BlockSpec((tm, w), lambda i: (tile(i), 0))
    const = lambda a, b: pl.BlockSpec((a, b), lambda i: (0, 0))
    rw_hi = router_w.astype(BF16)
    rw_hi_rest = (router_w - rw_hi.astype(F32)).astype(BF16)
    vec = lambda: pltpu.VMEM((1, N_EXPERTS), F32)
    return pl.pallas_call(
        functools.partial(_post_kernel, n_tiles),
        grid=(n_tiles + 1,),
        in_specs=[
            row(D_MODEL),
            row(POOL_WIDTH),
            pl.BlockSpec((maxw, POOL_WIDTH),
                         lambda i: (jnp.maximum(tile(i) * (tm // maxw) - 1, 0), 0)),
            pl.BlockSpec((N_PAIRS, tm, LANES), lambda i: (0, tile(i), 0)),
            const(POOL_WIDTH, POOL_WIDTH),
            const(1, POOL_WIDTH),
            const(D_MODEL, D_MODEL),
            const(1, D_MODEL),
            const(D_MODEL, N_EXPERTS),
            const(D_MODEL, N_EXPERTS),
            const(1, N_EXPERTS),
        ],
        out_specs=[row(D_MODEL), row(TOP_K), row(TOP_K),
                   pl.BlockSpec((None, 1, N_EXPERTS), lambda i: (tile(i), 0, 0)),
                   const(1, N_EXPERTS),
                   pl.BlockSpec(memory_space=pl.ANY)],
        out_shape=[
            jax.ShapeDtypeStruct((S, D_MODEL), F32),
            jax.ShapeDtypeStruct((S, TOP_K), F32),
            jax.ShapeDtypeStruct((S, TOP_K), jnp.int32),
            jax.ShapeDtypeStruct((n_tiles, 1, N_EXPERTS), jnp.int32),
            jax.ShapeDtypeStruct((1, N_EXPERTS), jnp.int32),
            jax.ShapeDtypeStruct((n_blocks * MOE_BM * ROW_TILE, LANES), F32),
        ],
        scratch_shapes=[pltpu.VMEM((tm + maxw, POOL_WIDTH), F32),
                        vec(), vec(), vec(),
                        pltpu.VMEM((ROW_BUFS, tm * ROW_TILE, LANES), F32),
                        pltpu.VMEM((tm, TOP_K), jnp.int32),
                        pltpu.SMEM((tm, TOP_K), jnp.int32),
                        pltpu.VMEM((8, N_EXPERTS), jnp.int32),
                        pltpu.SMEM((8, N_EXPERTS), jnp.int32),
                        pltpu.SemaphoreType.DMA((ROW_BUFS,)),
                        pltpu.SemaphoreType.DMA(())],
        compiler_params=pltpu.CompilerParams(
            dimension_semantics=("arbitrary",), has_side_effects=True,
            vmem_limit_bytes=VMEM_LIMIT),
        name="post_attention",
    )(x, u, u, att, pw_bd, pool_scale, wo_bf, g2, rw_hi, rw_hi_rest, router_b)


def _expert_kernel(bexp_ref, nblk_ref, nexte_ref, slot_ref, order_ref, xb_ref, wgu_hbm, bgu_ref, wd_hbm,
                   bd_ref, y_ref, wgu_f32, wd_f32, wgu_bf, wd_bf, sems):
    b = pl.program_id(0)
    e = bexp_ref[b]
    prev = bexp_ref[jnp.maximum(b - 1, 0)]
    fresh = jnp.logical_and(jnp.logical_or(b == 0, e != prev), b < nblk_ref[0])

    def fetch(expert, slot):
        return (pltpu.make_async_copy(wgu_hbm.at[expert], wgu_f32.at[slot], sems.at[0, slot]),
                pltpu.make_async_copy(wd_hbm.at[expert], wd_f32.at[slot], sems.at[1, slot]))

    @pl.when(b == 0)
    def _():
        for cp in fetch(e, slot_ref[e]):
            cp.start()

    @pl.when(fresh)
    def _():
        slot = slot_ref[e]
        for cp in fetch(e, slot):
            cp.wait()
        nxt = nexte_ref[e]

        @pl.when(nxt < N_EXPERTS)
        def _():
            for cp in fetch(nxt, 1 - slot):
                cp.start()

        wgu_bf[...] = wgu_f32[slot].astype(BF16)
        wd_bf[...] = wd_f32[slot].astype(BF16)

    @pl.when(b < nblk_ref[0])
    def _():
        x = _load_row_tiles(xb_ref, MOE_BM).astype(BF16)
        gu = jnp.dot(x, wgu_bf[...], preferred_element_type=F32) + bgu_ref[...]
        gate = jnp.minimum(gu[:, :D_FF], SWIGLU_LIMIT)
        up = jnp.clip(gu[:, D_FF:], -SWIGLU_LIMIT, SWIGLU_LIMIT)
        act = (up + 1.0) * (gate * jax.nn.sigmoid(gate * SWIGLU_ALPHA))
        y = jnp.dot(act.astype(BF16), wd_bf[...], preferred_element_type=F32) + bd_ref[...]
        _store_row_tiles(y_ref, y)

    @pl.when(b >= nblk_ref[0])
    def _():
        y_ref[...] = jnp.zeros_like(y_ref)


def _experts(block_exp, nblk, next_expert, expert_slot, order, xb, w_gate_up, b_gate_up, w_down,
             b_down):
    nb = xb.shape[0] // (MOE_BM * ROW_TILE)
    tiles = (MOE_BM * ROW_TILE, LANES)
    blk = lambda b, be, nk, ne, sl, od: (od[jnp.minimum(b, nk[0] - 1)], 0)
    wsel = lambda b, be, nk, *_: (be[jnp.minimum(b, nk[0] - 1)], 0, 0)
    return pl.pallas_call(
        _expert_kernel,
        grid_spec=pltpu.PrefetchScalarGridSpec(
            num_scalar_prefetch=5,
            grid=(nb,),
            in_specs=[
                pl.BlockSpec(tiles, blk),
                pl.BlockSpec(memory_space=pl.ANY),
                pl.BlockSpec((None, 1, 2 * D_FF), wsel),
                pl.BlockSpec(memory_space=pl.ANY),
                pl.BlockSpec((None, 1, D_MODEL), wsel),
            ],
            out_specs=pl.BlockSpec(tiles, lambda b, be, nk, ne, sl, od: (od[b], 0)),
            scratch_shapes=[pltpu.VMEM((2, D_MODEL, 2 * D_FF), F32),
                            pltpu.VMEM((2, D_FF, D_MODEL), F32),
                            pltpu.VMEM((D_MODEL, 2 * D_FF), BF16),
                            pltpu.VMEM((D_FF, D_MODEL), BF16),
                            pltpu.SemaphoreType.DMA((2, 2))],
        ),
        out_shape=jax.ShapeDtypeStruct(xb.shape, F32),
        compiler_params=pltpu.CompilerParams(
            dimension_semantics=("arbitrary",), vmem_limit_bytes=VMEM_LIMIT),
        name="moe_experts",
    )(block_exp, nblk, next_expert, expert_slot, order, xb, w_gate_up, b_gate_up, w_down, b_down)


def _combine_kernel(dest_ref, x1_ref, gate_ref, g_ref, y_hbm, o_ref, ybuf, sems):
    i = pl.program_id(0)
    n = pl.num_programs(0)
    tb = COMBINE_TB

    def row_copy(src_row, slot, k, j):
        src = y_hbm.at[pl.ds(pl.multiple_of(src_row * ROW_TILE, ROW_TILE), ROW_TILE)]
        dst = ybuf.at[slot, k, pl.ds(pl.multiple_of(j * ROW_TILE, ROW_TILE), ROW_TILE)]
        return pltpu.make_async_copy(src, dst, sems.at[slot])

    def gather(step, slot):
        def issue(jo, carry):
            for ji in range(ROW_UNROLL):
                j = jo * ROW_UNROLL + ji
                for k in range(TOP_K):
                    row_copy(dest_ref[(step * tb + j) * TOP_K + k], slot, k, j).start(
                        priority=k % 2)
            return carry

        lax.fori_loop(0, tb // ROW_UNROLL, issue, 0)

    slot = i % 2

    @pl.when(i == 0)
    def _():
        gather(0, 0)

    @pl.when(i + 1 < n)
    def _():
        gather(i + 1, 1 - slot)

    _drain_rows(lambda: row_copy(0, slot, 0, 0).wait(), tb * TOP_K)

    gates = gate_ref[...]
    moe = gates[:, 0:1] * _load_row_tiles(ybuf.at[slot, 0], tb)
    for k in range(1, TOP_K):
        moe = moe + gates[:, k:k + 1] * _load_row_tiles(ybuf.at[slot, k], tb)
    o_ref[...] = _rms(x1_ref[...] + moe, g_ref[...])


def _combine(dest_flat, x1, gates, final_g, y):
    N = x1.shape[0]
    tb = COMBINE_TB
    return pl.pallas_call(
        _combine_kernel,
        grid_spec=pltpu.PrefetchScalarGridSpec(
            num_scalar_prefetch=1,
            grid=(N // tb,),
            in_specs=[
                pl.BlockSpec((tb, D_MODEL), lambda i, *_: (i, 0)),
                pl.BlockSpec((tb, TOP_K), lambda i, *_: (i, 0)),
                pl.BlockSpec((1, D_MODEL), lambda i, *_: (0, 0)),
                pl.BlockSpec(memory_space=pl.ANY),
            ],
            out_specs=pl.BlockSpec((tb, D_MODEL), lambda i, *_: (i, 0)),
            scratch_shapes=[pltpu.VMEM((2, TOP_K, tb * ROW_TILE, LANES), F32),
                            pltpu.SemaphoreType.DMA((2,))],
        ),
        out_shape=jax.ShapeDtypeStruct((N, D_MODEL), F32),
        compiler_params=pltpu.CompilerParams(
            dimension_semantics=("arbitrary",), vmem_limit_bytes=VMEM_LIMIT),
        name="moe_combine",
    )(dest_flat, x1, gates, final_g, y)


def _layer(x2, norm1_g, w_in, pool_w, pool_scale, rel_bias, w_out, norm2_g,
           router_w, router_b, w_gate_up, b_gate_up, w_down, b_down, final_g):
    S = x2.shape[0]
    u, nat, p4, p16 = _inproj(x2, norm1_g.reshape(1, D_MODEL), w_in.astype(BF16), S)
    att = _attention(nat, p4, p16, _bias_tables(rel_bias), S)

    n_g = len(POOL_WINDOWS)
    pw_bd = (jnp.eye(n_g, dtype=F32)[:, None, :, None] * pool_w[:, :, None, :]).reshape(
        POOL_WIDTH, POOL_WIDTH).astype(BF16)
    x1, gates, dest, alloc, counts, xb = _post(
        x2, u, att, pw_bd, pool_scale.reshape(1, POOL_WIDTH), w_out.astype(BF16),
        norm2_g.reshape(1, D_MODEL), router_w, router_b.reshape(1, N_EXPERTS), S)

    counts = counts.reshape(N_EXPERTS)
    blocks_per = (counts + MOE_BM - 1) // MOE_BM
    block_end = jnp.cumsum(blocks_per)
    nb = (S * TOP_K) // MOE_BM + N_EXPERTS
    step = jnp.arange(nb, dtype=jnp.int32)
    block_exp = jnp.minimum(jnp.sum(block_end[None, :] <= step[:, None], axis=1),
                            N_EXPERTS - 1).astype(jnp.int32)
    nblk = block_end[-1:].astype(jnp.int32)
    handed = alloc.reshape(-1, N_EXPERTS).T.reshape(-1)
    used = handed >= 0
    pos = jnp.cumsum(used) - 1
    order = jnp.sum(jnp.where(used[None, :] & (pos[None, :] == step[:, None]), handed[None, :], 0),
                    axis=1)
    order = jnp.where(step < nblk[0], order, step).astype(jnp.int32)
    expert_ids = jnp.arange(N_EXPERTS, dtype=jnp.int32)
    nonempty = blocks_per > 0
    first_at_or_after = jnp.flip(lax.cummin(jnp.flip(jnp.where(nonempty, expert_ids, N_EXPERTS))))
    next_expert = jnp.concatenate(
        [first_at_or_after[1:], jnp.full((1,), N_EXPERTS, jnp.int32)]).astype(jnp.int32)
    expert_slot = ((jnp.cumsum(nonempty) - nonempty) % 2).astype(jnp.int32)

    y = _experts(block_exp, nblk, next_expert, expert_slot, order, xb, w_gate_up,
                 b_gate_up.reshape(N_EXPERTS, 1, 2 * D_FF), w_down,
                 b_down.reshape(N_EXPERTS, 1, D_MODEL))
    return _combine(dest.reshape(S * TOP_K), x1, gates, final_g.reshape(1, D_MODEL), y)


def kernel(x, norm1_g, w_in, pool_w, pool_scale, rel_bias, w_out, norm2_g, router_w, router_b,
           w_gate_up, b_gate_up, w_down, b_down, final_g):
    B, S, D = x.shape
    assert B == 1 and D == D_MODEL and S % ATT_TILE == 0, x.shape
    out = _layer(x.reshape(B * S, D), norm1_g[0], w_in[0], pool_w[0], pool_scale[0], rel_bias,
                 w_out[0], norm2_g[0], router_w[0], router_b[0], w_gate_up[0], b_gate_up[0],
                 w_down[0], b_down[0], final_g)
    return out.reshape(B, S, D)
```

```python
import functools
import math

import jax
import jax.numpy as jnp
from jax import lax
from jax.experimental import pallas as pl
from jax.experimental.pallas import tpu as pltpu

F32 = jnp.float32
BF16 = jnp.bfloat16

D_MODEL = 1024
POOL_WIDTH = 256
POOL_WINDOWS = (2, 4, 8, 16)
POOL_GROUP = 64
ATT_WIDTH = 768
HEAD_DIM = 64
N_HEADS = 12
N_PAIRS = N_HEADS // 2
DILATIONS = (1, 4, 16)
ATT_BLOCK = 128
N_BUCKETS = 32
MAX_DISTANCE = 2048
N_EXPERTS = 32
TOP_K = 4
D_FF = 1024
SWIGLU_LIMIT = 7.0
SWIGLU_ALPHA = 1.702
RMS_EPS = 1e-5
NEG_INF = -1e30
LOG2E = math.log2(math.e)

LANES = 128
ATT_TILE = 16 * ATT_BLOCK
BLOCKS_PER_TILE = ATT_TILE // ATT_BLOCK
N_UNITS = len(DILATIONS) * BLOCKS_PER_TILE
ATT_GROUP = 4
QKV_SLABS = 3 * N_PAIRS
IN_TM = 512
POST_TM = 256
MOE_BM = 256
COMBINE_TB = 256
ROW_UNROLL = 8
ROW_BUFS = 3
VMEM_LIMIT = 56 * 1024 * 1024

_CLASS_ROWS = tuple(ATT_TILE // d for d in DILATIONS)
_SEG_ROWS = tuple(ATT_BLOCK + r for r in _CLASS_ROWS)
_SEG_BASE = (0, _SEG_ROWS[0], _SEG_ROWS[0] + DILATIONS[1] * _SEG_ROWS[1])
KV_ROWS = _SEG_BASE[2] + DILATIONS[2] * _SEG_ROWS[2]


def _rms(x, g):
    return x * lax.rsqrt(jnp.mean(x * x, axis=-1, keepdims=True) + RMS_EPS) * g


ROW_TILE = D_MODEL // LANES


def _store_row_tiles(ref, x):
    n = x.shape[0]
    for c in range(ROW_TILE):
        ref[pl.ds(c, n, stride=ROW_TILE), :] = x[:, c * LANES:(c + 1) * LANES]


def _load_row_tiles(ref, n):
    return jnp.concatenate(
        [ref[pl.ds(c, n, stride=ROW_TILE), :] for c in range(ROW_TILE)], axis=-1)


def _inproj_kernel(x_ref, g_ref, w_ref, u_ref, nat_ref, p4_ref, p16_ref, h_ref, acc_ref):
    h_ref[...] = _rms(x_ref[...], g_ref[...]).astype(BF16)
    u_ref[...] = jnp.dot(h_ref[...], w_ref[:, :POOL_WIDTH], preferred_element_type=F32)
    for jj in range(QKV_SLABS // 2):
        c0 = POOL_WIDTH + 2 * jj * LANES
        res = jnp.dot(h_ref[...], w_ref[:, c0:c0 + 2 * LANES], preferred_element_type=F32)
        for s in (0, 1):
            j = 2 * jj + s
            slab = res[:, s * LANES:(s + 1) * LANES]
            if j < N_PAIRS:
                slab = slab * (HEAD_DIM ** -0.5 * LOG2E)
            acc_ref[j] = slab
            nat_ref[j] = slab.astype(BF16)
    for j in range(QKV_SLABS):
        for d, ref in ((DILATIONS[1], p4_ref), (DILATIONS[2], p16_ref)):
            for r in range(d):
                ref[j, r] = acc_ref[j, pl.ds(r, IN_TM // d, stride=d), :].astype(BF16)


def _inproj(x, g, w_bf, S):
    sub = ATT_TILE // IN_TM
    n_tiles = S // ATT_TILE

    def res_spec(d):
        return pl.BlockSpec((QKV_SLABS, None, d, IN_TM // d, LANES),
                            lambda i: (0, i // sub, 0, i % sub, 0))

    def res_shape(d):
        return jax.ShapeDtypeStruct((QKV_SLABS, n_tiles, d, ATT_TILE // d, LANES), BF16)

    return pl.pallas_call(
        _inproj_kernel,
        grid=(S // IN_TM,),
        in_specs=[
            pl.BlockSpec((IN_TM, D_MODEL), lambda i: (i, 0)),
            pl.BlockSpec((1, D_MODEL), lambda i: (0, 0)),
            pl.BlockSpec((D_MODEL, POOL_WIDTH + 3 * ATT_WIDTH), lambda i: (0, 0)),
        ],
        out_specs=[
            pl.BlockSpec((IN_TM, POOL_WIDTH), lambda i: (i, 0)),
            pl.BlockSpec((QKV_SLABS, IN_TM, LANES), lambda i: (0, i, 0)),
            res_spec(DILATIONS[1]),
            res_spec(DILATIONS[2]),
        ],
        out_shape=[
            jax.ShapeDtypeStruct((S, POOL_WIDTH), F32),
            jax.ShapeDtypeStruct((QKV_SLABS, S, LANES), BF16),
            res_shape(DILATIONS[1]),
            res_shape(DILATIONS[2]),
        ],
        scratch_shapes=[pltpu.VMEM((IN_TM, D_MODEL), BF16),
                        pltpu.VMEM((QKV_SLABS, IN_TM, LANES), F32)],
        compiler_params=pltpu.CompilerParams(
            dimension_semantics=("arbitrary",), vmem_limit_bytes=VMEM_LIMIT),
        name="inproj",
    )(x, g, w_bf)


def _t5_bucket(dist):
    max_exact = N_BUCKETS // 2
    is_small = dist < max_exact
    nf = jnp.maximum(dist, 1).astype(jnp.float32)
    large = max_exact + (jnp.log(nf / max_exact) / math.log(MAX_DISTANCE / max_exact)
                         * (N_BUCKETS - max_exact)).astype(jnp.int32)
    large = jnp.minimum(large, N_BUCKETS - 1)
    return jnp.where(is_small, dist, large)


def _bias_kernel(ids_ref, rb_ref, o_ref):
    h = pl.program_id(0)
    col = lax.broadcasted_iota(jnp.int32, (ATT_BLOCK, 2 * ATT_BLOCK), 1)
    for b in range(len(DILATIONS)):
        ids = ids_ref[b]
        tab = jnp.full(ids.shape, NEG_INF, F32)
        for k in range(N_BUCKETS):
            tab = jnp.where(ids == k, rb_ref[k, h] * LOG2E, tab)
        o_ref[0, b] = tab
        o_ref[1, b] = jnp.where(col < ATT_BLOCK, NEG_INF, tab)


def _bias_tables(rel_bias):
    qi = jnp.arange(ATT_BLOCK)[:, None]
    kj = jnp.arange(2 * ATT_BLOCK)[None, :]
    dist = qi + ATT_BLOCK - kj
    ok = (dist >= 0) & (dist <= ATT_BLOCK)
    ids = jnp.stack([jnp.where(ok, _t5_bucket(jnp.clip(dist, 0) * d), -1) for d in DILATIONS])
    return pl.pallas_call(
        _bias_kernel,
        grid=(N_HEADS,),
        in_specs=[
            pl.BlockSpec((len(DILATIONS), ATT_BLOCK, 2 * ATT_BLOCK), lambda h: (0, 0, 0)),
            pl.BlockSpec(memory_space=pltpu.SMEM),
        ],
        out_specs=pl.BlockSpec((2, len(DILATIONS), None, ATT_BLOCK, 2 * ATT_BLOCK),
                               lambda h: (0, 0, h, 0, 0)),
        out_shape=jax.ShapeDtypeStruct(
            (2, len(DILATIONS), N_HEADS, ATT_BLOCK, 2 * ATT_BLOCK), F32),
        compiler_params=pltpu.CompilerParams(dimension_semantics=("arbitrary",)),
        name="bias_tables",
    )(ids.astype(jnp.int32), rel_bias.astype(F32))


def _attn_kernel(qn, knp, kn, vnp, vn, q4, k4p, k4, v4p, v4, q16, k16p, k16, v16p, v16,
                 bias_ref, o_ref, qbuf, kbuf, vbuf0, vbuf1, sbuf0, sbuf1, pbuf0, pbuf1,
                 obuf, mbuf, lbuf, tbuf):
    i = pl.program_id(0)
    at_start = i == 0
    G = ATT_GROUP
    n_groups = N_UNITS // G
    sbufs, pbufs, vbufs = (sbuf0, sbuf1), (pbuf0, pbuf1), (vbuf0, vbuf1)
    lane = lax.broadcasted_iota(jnp.int32, (ATT_BLOCK, LANES), 1)
    lo = lane < HEAD_DIM
    head_mask = (jnp.where(lo, 1.0, 0.0).astype(BF16), jnp.where(lo, 0.0, 1.0).astype(BF16))

    @pl.when(jnp.logical_and(at_start, pl.program_id(1) == 0))
    def _():
        for buf in sbufs + pbufs:
            buf[...] = jnp.zeros_like(buf)

    for b, q in enumerate((qn, q4, q16)):
        qbuf[b * ATT_TILE:(b + 1) * ATT_TILE] = q[...]

    def fill_k(dst0, rows, val):
        kbuf[dst0:dst0 + rows] = val

    def fill_v(dst0, rows, val):
        for a in (0, 1):
            keep = head_mask[a][0:1]
            vbufs[a][dst0:dst0 + rows] = val * keep + (1 - keep)

    for fill, prevs, curs in ((fill_k, (knp, k4p, k16p), (kn, k4, k16)),
                              (fill_v, (vnp, v4p, v16p), (vn, v4, v16))):
        fill(0, ATT_BLOCK, prevs[0][...])
        fill(ATT_BLOCK, ATT_TILE, curs[0][...])
        for b in (1, 2):
            rows = _CLASS_ROWS[b]
            for c in range(DILATIONS[b]):
                base = _SEG_BASE[b] + c * _SEG_ROWS[b]
                fill(base, ATT_BLOCK, prevs[b][c])
                fill(base + ATT_BLOCK, rows, curs[b][c * rows:(c + 1) * rows])

    def geometry(u):
        b = u // BLOCKS_PER_TILE
        w = u % BLOCKS_PER_TILE
        sh = 4 - 2 * b
        c = lax.shift_right_logical(w, sh)
        a = w - lax.shift_left(c, sh)
        seg = ATT_BLOCK + lax.shift_right_logical(ATT_TILE, 2 * b)
        base = b * _SEG_BASE[1] + jnp.where(b == 2, _SEG_BASE[2] - 2 * _SEG_BASE[1], 0)
        krow = pl.multiple_of(base + c * seg + a * ATT_BLOCK, ATT_BLOCK)
        variant = jnp.logical_and(at_start, a == 0).astype(jnp.int32)
        return b, krow, variant

    def scores(g, sbuf):
        for j in range(G):
            u = g * G + j
            b, krow, variant = geometry(u)
            q = qbuf[pl.ds(pl.multiple_of(u * ATT_BLOCK, ATT_BLOCK), ATT_BLOCK), :]
            k = kbuf[pl.ds(krow, 2 * ATT_BLOCK), :]
            for a in (0, 1):
                s = lax.dot_general(q * head_mask[a], k, (((1,), (1,)), ((), ())),
                                    preferred_element_type=F32)
                sbuf[j, a] = s + bias_ref[variant, b, a]

    def softmax(g, sbuf, pbuf):
        for j in range(G):
            rows = pl.ds(pl.multiple_of((g * G + j) * ATT_BLOCK, ATT_BLOCK), ATT_BLOCK)
            maxes = []
            for a in (0, 1):
                s = sbuf[j, a]
                m = jnp.max(s, axis=-1, keepdims=True)
                pbuf[j, a] = jnp.exp2(s - m).astype(BF16)
                maxes.append(jnp.broadcast_to(m, (ATT_BLOCK, LANES)))
            mbuf[rows, :] = jnp.where(lo, maxes[0], maxes[1])

    def values(g, pbuf):
        for j in range(G):
            u = g * G + j
            _, krow, _ = geometry(u)
            rows = pl.ds(pl.multiple_of(u * ATT_BLOCK, ATT_BLOCK), ATT_BLOCK)
            o0 = jnp.dot(pbuf[j, 0], vbuf0[pl.ds(krow, 2 * ATT_BLOCK), :],
                         preferred_element_type=F32)
            o1 = jnp.dot(pbuf[j, 1], vbuf1[pl.ds(krow, 2 * ATT_BLOCK), :],
                         preferred_element_type=F32)
            obuf[rows, :] = jnp.where(lo, o0, o1)
            lbuf[rows, :] = pltpu.roll(jnp.where(lo, o1, o0), HEAD_DIM, axis=1)

    def trip_pair(tt, carry):
        clamp = lambda g: jnp.clip(g, 0, n_groups - 1)
        for par in (0, 1):
            t = 2 * tt + par
            values(clamp(t - 2), pbufs[par])
            scores(clamp(t), sbufs[par])
            softmax(clamp(t - 1), sbufs[1 - par], pbufs[1 - par])
        return carry

    assert n_groups % 2 == 0
    lax.fori_loop(0, (n_groups + 2) // 2, trip_pair, 0)

    for b in (1, 2):
        d, rows = DILATIONS[b], _CLASS_ROWS[b]
        for r in range(d):
            src = slice(b * ATT_TILE + r * rows, b * ATT_TILE + (r + 1) * rows)
            dst = pl.ds(r, rows, stride=d)
            for n, buf in enumerate((obuf, mbuf, lbuf)):
                tbuf[3 * (b - 1) + n, dst, :] = buf[src]

    def merge(c, carry):
        rows = pl.ds(pl.multiple_of(c * ATT_BLOCK, ATT_BLOCK), ATT_BLOCK)
        ms = (mbuf[rows, :], tbuf[1, rows, :], tbuf[4, rows, :])
        os_ = (obuf[rows, :], tbuf[0, rows, :], tbuf[3, rows, :])
        ls = (lbuf[rows, :], tbuf[2, rows, :], tbuf[5, rows, :])
        mm = jnp.maximum(jnp.maximum(ms[0], ms[1]), ms[2])
        ws = [jnp.exp2(m - mm) for m in ms]
        num = ws[0] * os_[0] + ws[1] * os_[1] + ws[2] * os_[2]
        den = ws[0] * ls[0] + ws[1] * ls[1] + ws[2] * ls[2]
        o_ref[rows, :] = (num / den).astype(o_ref.dtype)
        return carry

    lax.fori_loop(0, BLOCKS_PER_TILE, merge, 0)


def _attention(nat, p4, p16, bias, S):
    n_tiles = S // ATT_TILE
    d4, d16 = DILATIONS[1], DILATIONS[2]
    flat4 = p4.reshape(QKV_SLABS, S, LANES)
    flat16 = p16.reshape(QKV_SLABS, S, LANES)
    blk4 = p4.reshape(QKV_SLABS, n_tiles, d4, _CLASS_ROWS[1] // ATT_BLOCK, ATT_BLOCK, LANES)
    prev = lambda i: jnp.maximum(i - 1, 0)

    def cur(off):
        return pl.BlockSpec((None, ATT_TILE, LANES), lambda i, p: (off + p, i, 0))

    def prev_nat(off):
        return pl.BlockSpec((None, ATT_BLOCK, LANES),
                            lambda i, p: (off + p, jnp.maximum(i * BLOCKS_PER_TILE - 1, 0), 0))

    def prev4(off):
        last = _CLASS_ROWS[1] // ATT_BLOCK - 1
        return pl.BlockSpec((None, None, d4, None, ATT_BLOCK, LANES),
                            lambda i, p: (off + p, prev(i), 0, last, 0, 0))

    def prev16(off):
        return pl.BlockSpec((None, None, d16, ATT_BLOCK, LANES),
                            lambda i, p: (off + p, prev(i), 0, 0, 0))

    k_off, v_off = N_PAIRS, 2 * N_PAIRS
    return pl.pallas_call(
        _attn_kernel,
        grid=(n_tiles, N_PAIRS),
        in_specs=[
            cur(0), prev_nat(k_off), cur(k_off), prev_nat(v_off), cur(v_off),
            cur(0), prev4(k_off), cur(k_off), prev4(v_off), cur(v_off),
            cur(0), prev16(k_off), cur(k_off), prev16(v_off), cur(v_off),
            pl.BlockSpec((2, len(DILATIONS), 2, ATT_BLOCK, 2 * ATT_BLOCK),
                         lambda i, p: (0, 0, p, 0, 0)),
        ],
        out_specs=pl.BlockSpec((None, ATT_TILE, LANES), lambda i, p: (p, i, 0)),
        out_shape=jax.ShapeDtypeStruct((N_PAIRS, S, LANES), BF16),
        scratch_shapes=[
            pltpu.VMEM((N_UNITS * ATT_BLOCK, LANES), BF16),
            pltpu.VMEM((KV_ROWS, LANES), BF16),
            pltpu.VMEM((KV_ROWS, LANES), BF16),
            pltpu.VMEM((KV_ROWS, LANES), BF16),
            pltpu.VMEM((ATT_GROUP, 2, ATT_BLOCK, 2 * ATT_BLOCK), F32),
            pltpu.VMEM((ATT_GROUP, 2, ATT_BLOCK, 2 * ATT_BLOCK), F32),
            pltpu.VMEM((ATT_GROUP, 2, ATT_BLOCK, 2 * ATT_BLOCK), BF16),
            pltpu.VMEM((ATT_GROUP, 2, ATT_BLOCK, 2 * ATT_BLOCK), BF16),
            pltpu.VMEM((N_UNITS * ATT_BLOCK, LANES), F32),
            pltpu.VMEM((N_UNITS * ATT_BLOCK, LANES), F32),
            pltpu.VMEM((N_UNITS * ATT_BLOCK, LANES), F32),
            pltpu.VMEM((6, ATT_TILE, LANES), F32),
        ],
        compiler_params=pltpu.CompilerParams(
            dimension_semantics=("arbitrary", "arbitrary"), vmem_limit_bytes=VMEM_LIMIT),
        name="attention",
    )(nat, nat, nat, nat, nat, flat4, blk4, flat4, blk4, flat4,
      flat16, p16, flat16, p16, flat16, bias)


def _drain_rows(wait_one, n):
    def body(j, carry):
        for _ in range(ROW_UNROLL):
            wait_one()
        return carry

    lax.fori_loop(0, n // ROW_UNROLL, body, 0)


def _post_kernel(n_tiles, x_ref, u_ref, up_ref, att_ref, pw_ref, ps_ref, wo_ref, g2_ref,
                 rwh_ref, rwl_ref, rb_ref, x1_ref, gate_ref, dest_ref, alloc_ref, cnt_ref, xb_hbm,
                 ue, carry, curblk, nfree, hbuf, dbuf, dsm, fbuf, fsm, row_sems, idx_sem):
    i = pl.program_id(0)
    tm = POST_TM
    maxw = max(POOL_WINDOWS)
    inv_bm = 1.0 / MOE_BM
    blocks_of = lambda c: jnp.floor((c + (MOE_BM - 1)) * inv_bm)

    def row_copy(slot, j, dst_row):
        src = hbuf.at[slot, pl.ds(pl.multiple_of(j * ROW_TILE, ROW_TILE), ROW_TILE)]
        dst = xb_hbm.at[pl.ds(pl.multiple_of(dst_row * ROW_TILE, ROW_TILE), ROW_TILE)]
        return pltpu.make_async_copy(src, dst, row_sems.at[slot])

    @pl.when(i == 0)
    def _():
        carry[...] = jnp.zeros_like(carry)
        curblk[...] = jnp.zeros_like(curblk)
        nfree[...] = jnp.zeros_like(nfree)

    @pl.when(i < n_tiles)
    def _():
        hist = up_ref[...]
        ue[0:maxw] = jnp.where(i == 0, jnp.zeros_like(hist), hist)
        ue[maxw:] = u_ref[...]
        t_glob = i * tm + lax.broadcasted_iota(jnp.int32, (tm, LANES), 0)
        lane = lax.broadcasted_iota(jnp.int32, (tm, LANES), 1)
        halves = []
        for half, (w_lo, w_hi) in enumerate(((2, 4), (8, 16))):
            cols = slice(half * LANES, (half + 1) * LANES)
            s = ue[maxw:, cols]
            tok = s
            s_lo = None
            for j in range(1, w_hi):
                s = s + ue[maxw - j:maxw - j + tm, cols]
                if j == w_lo - 1:
                    s_lo = s
            cnt_lo = jnp.minimum(t_glob + 1, w_lo).astype(F32)
            cnt_hi = jnp.minimum(t_glob + 1, w_hi).astype(F32)
            halves.append(jnp.where(lane < POOL_GROUP, s_lo / cnt_lo, s / cnt_hi) - tok)
        pooled = jnp.concatenate(halves, axis=-1)
        mixed = jnp.dot(pooled.astype(BF16), pw_ref[...], preferred_element_type=F32) * ps_ref[...]

        mix = jnp.concatenate([mixed.astype(BF16)] + [att_ref[j] for j in range(N_PAIRS)],
                              axis=-1)
        x1 = x_ref[...] + jnp.dot(mix, wo_ref[...], preferred_element_type=F32)
        x1_ref[...] = x1
        h2 = _rms(x1, g2_ref[...])

        h2_hi = h2.astype(BF16)
        h2_lo = (h2 - h2_hi.astype(F32)).astype(BF16)
        logits = (jnp.dot(h2_hi, rwh_ref[...], preferred_element_type=F32)
                  + jnp.dot(h2_hi, rwl_ref[...], preferred_element_type=F32)
                  + jnp.dot(h2_lo, rwh_ref[...], preferred_element_type=F32)) + rb_ref[...]
        eidx = lax.broadcasted_iota(jnp.int32, (tm, N_EXPERTS), 1)
        vals = logits
        top_v, top_i = [], []
        for _ in range(TOP_K):
            m = jnp.max(vals, axis=-1, keepdims=True)
            idx = jnp.min(jnp.where(vals == m, eidx, N_EXPERTS), axis=-1, keepdims=True)
            top_v.append(m)
            top_i.append(idx)
            vals = jnp.where(eidx == idx, -jnp.inf, vals)
        ex = [jnp.exp(v - top_v[0]) for v in top_v]
        den = ex[0] + ex[1] + ex[2] + ex[3]

        hot = [(eidx == idx) for idx in top_i]
        cnt = sum(h.astype(F32) for h in hot)
        row = lax.broadcasted_iota(jnp.int32, (tm, tm), 0)
        col = lax.broadcasted_iota(jnp.int32, (tm, tm), 1)
        strict = jnp.where(col < row, 1.0, 0.0).astype(BF16)
        c0 = carry[...]
        before = jnp.dot(strict, cnt.astype(BF16), preferred_element_type=F32) + c0

        c1 = c0 + jnp.sum(cnt, axis=0, keepdims=True)
        nb0 = blocks_of(c0)
        opens = blocks_of(c1) - nb0
        er = lax.broadcasted_iota(jnp.int32, (N_EXPERTS, N_EXPERTS), 0)
        ec = lax.broadcasted_iota(jnp.int32, (N_EXPERTS, N_EXPERTS), 1)
        earlier = jnp.where(er < ec, 1.0, 0.0).astype(BF16)
        opened_before = jnp.dot(jnp.broadcast_to(opens, (8, N_EXPERTS)).astype(BF16), earlier,
                                preferred_element_type=F32)[0:1]
        new_blk = nfree[...] + opened_before
        cur = curblk[...]
        a_tok = jnp.floor(before * inv_bm)
        blk = jnp.where(a_tok == nb0 - 1.0, cur, new_blk)
        dest_full = blk * MOE_BM + (before - a_tok * MOE_BM)
        dests = [jnp.sum(jnp.where(h, dest_full, 0.0), axis=-1, keepdims=True) for h in hot]

        carry[...] = c1
        curblk[...] = jnp.where(opens > 0, new_blk, cur)
        nfree[...] = nfree[...] + jnp.sum(opens, axis=-1, keepdims=True)
        alloc_ref[...] = jnp.where(opens > 0, new_blk, -1.0).astype(jnp.int32)
        cnt_ref[...] = c1.astype(jnp.int32)

        k4 = lax.broadcasted_iota(jnp.int32, (tm, TOP_K), 1)

        def pack(cols):
            out = jnp.broadcast_to(cols[0], (tm, TOP_K))
            for k in range(1, TOP_K):
                out = jnp.where(k4 == k, jnp.broadcast_to(cols[k], (tm, TOP_K)), out)
            return out

        gate_ref[...] = pack([e / den for e in ex])
        dest = pack(dests).astype(jnp.int32)
        dest_ref[...] = dest

        @pl.when(i >= ROW_BUFS)
        def _():
            _drain_rows(lambda: row_copy(i % ROW_BUFS, 0, 0).wait(), tm * TOP_K)

        _store_row_tiles(hbuf.at[i % ROW_BUFS], h2)

    @pl.when(i > 0)
    def _():
        pltpu.make_async_copy(dbuf, dsm, idx_sem).wait()
        slot = (i - 1) % ROW_BUFS

        def issue(jo, c):
            for ji in range(ROW_UNROLL):
                j = jo * ROW_UNROLL + ji
                for k in range(TOP_K):
                    row_copy(slot, j, dsm[j, k]).start(priority=1)
            return c

        lax.fori_loop(0, tm // ROW_UNROLL, issue, 0)

    @pl.when(i < n_tiles)
    def _():
        dbuf[...] = dest_ref[...]
        pltpu.make_async_copy(dbuf, dsm, idx_sem).start()

    @pl.when(i == n_tiles)
    def _():
        last = (n_tiles - 1) % ROW_BUFS
        for back in range(min(ROW_BUFS, n_tiles)):
            s = (n_tiles - 1 - back) % ROW_BUFS
            _drain_rows(lambda: row_copy(s, 0, 0).wait(), tm * TOP_K)

        c = carry[...]
        used_in_last = c - (blocks_of(c) - 1.0) * MOE_BM
        fbuf[0:1] = jnp.where(c > 0, curblk[...] * MOE_BM + used_in_last, 0.0).astype(jnp.int32)
        fbuf[1:2] = jnp.where(c > 0, (curblk[...] + 1.0) * MOE_BM, 0.0).astype(jnp.int32)
        fbuf[2:8] = jnp.broadcast_to(nfree[...], (6, N_EXPERTS)).astype(jnp.int32)
        fill = pltpu.make_async_copy(fbuf, fsm, idx_sem)
        fill.start()
        fill.wait()

        def per_expert(e, carry_):
            lo, hi = fsm[0, e], fsm[1, e]

            def start_row(s, c_):
                row_copy(last, 0, s).start()
                return c_

            def wait_row(s, c_):
                row_copy(last, 0, 0).wait()
                return c_

            lax.fori_loop(lo, hi, start_row, 0)
            lax.fori_loop(lo, hi, wait_row, 0)
            return carry_

        lax.fori_loop(0, N_EXPERTS, per_expert, 0)

        def block_copy(b):
            rows = MOE_BM * ROW_TILE
            return pltpu.make_async_copy(
                hbuf.at[last], xb_hbm.at[pl.ds(pl.multiple_of(b * rows, rows), rows)],
                row_sems.at[last])

        def start_block(b, c_):
            block_copy(b).start()
            return c_

        def wait_block(b, c_):
            block_copy(b).wait()
            return c_

        n_blocks = xb_hbm.shape[0] // (MOE_BM * ROW_TILE)
        lax.fori_loop(fsm[2, 0], n_blocks, start_block, 0)
        lax.fori_loop(fsm[2, 0], n_blocks, wait_block, 0)


def _post(x, u, att, pw_bd, pool_scale, wo_bf, g2, router_w, router_b, S):
    tm = POST_TM
    assert tm == MOE_BM and tm % ROW_UNROLL == 0
    n_tiles = S // tm
    n_blocks = (S * TOP_K) // MOE_BM + N_EXPERTS
    maxw = max(POOL_WINDOWS)
    tile = lambda i: jnp.minimum(i, n_tiles - 1)
    row = lambda w: pl.BlockSpec((tm, w), lambda i: (tile(i), 0))
    const = lambda a, b: pl.BlockSpec((a, b), lambda i: (0, 0))
    rw_hi = router_w.astype(BF16)
    rw_hi_rest = (router_w - rw_hi.astype(F32)).astype(BF16)
    vec = lambda: pltpu.VMEM((1, N_EXPERTS), F32)
    return pl.pallas_call(
        functools.partial(_post_kernel, n_tiles),
        grid=(n_tiles + 1,),
        in_specs=[
            row(D_MODEL),
            row(POOL_WIDTH),
            pl.BlockSpec((maxw, POOL_WIDTH),
                         lambda i: (jnp.maximum(tile(i) * (tm // maxw) - 1, 0), 0)),
            pl.BlockSpec((N_PAIRS, tm, LANES), lambda i: (0, tile(i), 0)),
            const(POOL_WIDTH, POOL_WIDTH),
            const(1, POOL_WIDTH),
            const(D_MODEL, D_MODEL),
            const(1, D_MODEL),
            const(D_MODEL, N_EXPERTS),
            const(D_MODEL, N_EXPERTS),
            const(1, N_EXPERTS),
        ],
        out_specs=[row(D_MODEL), row(TOP_K), row(TOP_K),
                   pl.BlockSpec((None, 1, N_EXPERTS), lambda i: (tile(i), 0, 0)),
                   const(1, N_EXPERTS),
                   pl.BlockSpec(memory_space=pl.ANY)],
        out_shape=[
            jax.ShapeDtypeStruct((S, D_MODEL), F32),
            jax.ShapeDtypeStruct((S, TOP_K), F32),
            jax.ShapeDtypeStruct((S, TOP_K), jnp.int32),
            jax.ShapeDtypeStruct((n_tiles, 1, N_EXPERTS), jnp.int32),
            jax.ShapeDtypeStruct((1, N_EXPERTS), jnp.int32),
            jax.ShapeDtypeStruct((n_blocks * MOE_BM * ROW_TILE, LANES), F32),
        ],
        scratch_shapes=[pltpu.VMEM((tm + maxw, POOL_WIDTH), F32),
                        vec(), vec(), vec(),
                        pltpu.VMEM((ROW_BUFS, tm * ROW_TILE, LANES), F32),
                        pltpu.VMEM((tm, TOP_K), jnp.int32),
                        pltpu.SMEM((tm, TOP_K), jnp.int32),
                        pltpu.VMEM((8, N_EXPERTS), jnp.int32),
                        pltpu.SMEM((8, N_EXPERTS), jnp.int32),
                        pltpu.SemaphoreType.DMA((ROW_BUFS,)),
                        pltpu.SemaphoreType.DMA(())],
        compiler_params=pltpu.CompilerParams(
            dimension_semantics=("arbitrary",), has_side_effects=True,
            vmem_limit_bytes=VMEM_LIMIT),
        name="post_attention",
    )(x, u, u, att, pw_bd, pool_scale, wo_bf, g2, rw_hi, rw_hi_rest, router_b)


def _expert_kernel(bexp_ref, nblk_ref, nexte_ref, slot_ref, order_ref, xb_ref, wgu_hbm, bgu_ref, wd_hbm,
                   bd_ref, y_ref, wgu_f32, wd_f32, wgu_bf, wd_bf, sems):
    b = pl.program_id(0)
    e = bexp_ref[b]
    prev = bexp_ref[jnp.maximum(b - 1, 0)]
    fresh = jnp.logical_and(jnp.logical_or(b == 0, e != prev), b < nblk_ref[0])

    def fetch(expert, slot):
        return (pltpu.make_async_copy(wgu_hbm.at[expert], wgu_f32.at[slot], sems.at[0, slot]),
                pltpu.make_async_copy(wd_hbm.at[expert], wd_f32.at[slot], sems.at[1, slot]))

    @pl.when(b == 0)
    def _():
        for cp in fetch(e, slot_ref[e]):
            cp.start()

    @pl.when(fresh)
    def _():
        slot = slot_ref[e]
        for cp in fetch(e, slot):
            cp.wait()
        nxt = nexte_ref[e]

        @pl.when(nxt < N_EXPERTS)
        def _():
            for cp in fetch(nxt, 1 - slot):
                cp.start()

        wgu_bf[...] = wgu_f32[slot].astype(BF16)
        wd_bf[...] = wd_f32[slot].astype(BF16)

    @pl.when(b < nblk_ref[0])
    def _():
        x = _load_row_tiles(xb_ref, MOE_BM).astype(BF16)
        gu = jnp.dot(x, wgu_bf[...], preferred_element_type=F32) + bgu_ref[...]
        gate = jnp.minimum(gu[:, :D_FF], SWIGLU_LIMIT)
        up = jnp.clip(gu[:, D_FF:], -SWIGLU_LIMIT, SWIGLU_LIMIT)
        act = (up + 1.0) * (gate * jax.nn.sigmoid(gate * SWIGLU_ALPHA))
        y = jnp.dot(act.astype(BF16), wd_bf[...], preferred_element_type=F32) + bd_ref[...]
        _store_row_tiles(y_ref, y)

    @pl.when(b >= nblk_ref[0])
    def _():
        y_ref[...] = jnp.zeros_like(y_ref)


def _experts(block_exp, nblk, next_expert, expert_slot, order, xb, w_gate_up, b_gate_up, w_down,
             b_down):
    nb = xb.shape[0] // (MOE_BM * ROW_TILE)
    tiles = (MOE_BM * ROW_TILE, LANES)
    blk = lambda b, be, nk, ne, sl, od: (od[jnp.minimum(b, nk[0] - 1)], 0)
    wsel = lambda b, be, nk, *_: (be[jnp.minimum(b, nk[0] - 1)], 0, 0)
    return pl.pallas_call(
        _expert_kernel,
        grid_spec=pltpu.PrefetchScalarGridSpec(
            num_scalar_prefetch=5,
            grid=(nb,),
            in_specs=[
                pl.BlockSpec(tiles, blk),
                pl.BlockSpec(memory_space=pl.ANY),
                pl.BlockSpec((None, 1, 2 * D_FF), wsel),
                pl.BlockSpec(memory_space=pl.ANY),
                pl.BlockSpec((None, 1, D_MODEL), wsel),
            ],
            out_specs=pl.BlockSpec(tiles, lambda b, be, nk, ne, sl, od: (od[b], 0)),
            scratch_shapes=[pltpu.VMEM((2, D_MODEL, 2 * D_FF), F32),
                            pltpu.VMEM((2, D_FF, D_MODEL), F32),
                            pltpu.VMEM((D_MODEL, 2 * D_FF), BF16),
                            pltpu.VMEM((D_FF, D_MODEL), BF16),
                            pltpu.SemaphoreType.DMA((2, 2))],
        ),
        out_shape=jax.ShapeDtypeStruct(xb.shape, F32),
        compiler_params=pltpu.CompilerParams(
            dimension_semantics=("arbitrary",), vmem_limit_bytes=VMEM_LIMIT),
        name="moe_experts",
    )(block_exp, nblk, next_expert, expert_slot, order, xb, w_gate_up, b_gate_up, w_down, b_down)


def _combine_kernel(dest_ref, x1_ref, gate_ref, g_ref, y_hbm, o_ref, ybuf, sems):
    i = pl.program_id(0)
    n = pl.num_programs(0)
    tb = COMBINE_TB

    def row_copy(src_row, slot, k, j):
        src = y_hbm.at[pl.ds(pl.multiple_of(src_row * ROW_TILE, ROW_TILE), ROW_TILE)]
        dst = ybuf.at[slot, k, pl.ds(pl.multiple_of(j * ROW_TILE, ROW_TILE), ROW_TILE)]
        return pltpu.make_async_copy(src, dst, sems.at[slot])

    def gather(step, slot):
        def issue(jo, carry):
            for ji in range(ROW_UNROLL):
                j = jo * ROW_UNROLL + ji
                for k in range(TOP_K):
                    row_copy(dest_ref[(step * tb + j) * TOP_K + k], slot, k, j).start(
                        priority=k % 2)
            return carry

        lax.fori_loop(0, tb // ROW_UNROLL, issue, 0)

    slot = i % 2

    @pl.when(i == 0)
    def _():
        gather(0, 0)

    @pl.when(i + 1 < n)
    def _():
        gather(i + 1, 1 - slot)

    _drain_rows(lambda: row_copy(0, slot, 0, 0).wait(), tb * TOP_K)

    gates = gate_ref[...]
    moe = gates[:, 0:1] * _load_row_tiles(ybuf.at[slot, 0], tb)
    for k in range(1, TOP_K):
        moe = moe + gates[:, k:k + 1] * _load_row_tiles(ybuf.at[slot, k], tb)
    o_ref[...] = _rms(x1_ref[...] + moe, g_ref[...])


def _combine(dest_flat, x1, gates, final_g, y):
    N = x1.shape[0]
    tb = COMBINE_TB
    return pl.pallas_call(
        _combine_kernel,
        grid_spec=pltpu.PrefetchScalarGridSpec(
            num_scalar_prefetch=1,
            grid=(N // tb,),
            in_specs=[
                pl.BlockSpec((tb, D_MODEL), lambda i, *_: (i, 0)),
                pl.BlockSpec((tb, TOP_K), lambda i, *_: (i, 0)),
                pl.BlockSpec((1, D_MODEL), lambda i, *_: (0, 0)),
                pl.BlockSpec(memory_space=pl.ANY),
            ],
            out_specs=pl.BlockSpec((tb, D_MODEL), lambda i, *_: (i, 0)),
            scratch_shapes=[pltpu.VMEM((2, TOP_K, tb * ROW_TILE, LANES), F32),
                            pltpu.SemaphoreType.DMA((2,))],
        ),
        out_shape=jax.ShapeDtypeStruct((N, D_MODEL), F32),
        compiler_params=pltpu.CompilerParams(
            dimension_semantics=("arbitrary",), vmem_limit_bytes=VMEM_LIMIT),
        name="moe_combine",
    )(dest_flat, x1, gates, final_g, y)


def _layer(x2, norm1_g, w_in, pool_w, pool_scale, rel_bias, w_out, norm2_g,
           router_w, router_b, w_gate_up, b_gate_up, w_down, b_down, final_g):
    S = x2.shape[0]
    u, nat, p4, p16 = _inproj(x2, norm1_g.reshape(1, D_MODEL), w_in.astype(BF16), S)
    att = _attention(nat, p4, p16, _bias_tables(rel_bias), S)

    n_g = len(POOL_WINDOWS)
    pw_bd = (jnp.eye(n_g, dtype=F32)[:, None, :, None] * pool_w[:, :, None, :]).reshape(
        POOL_WIDTH, POOL_WIDTH).astype(BF16)
    x1, gates, dest, alloc, counts, xb = _post(
        x2, u, att, pw_bd, pool_scale.reshape(1, POOL_WIDTH), w_out.astype(BF16),
        norm2_g.reshape(1, D_MODEL), router_w, router_b.reshape(1, N_EXPERTS), S)

    counts = counts.reshape(N_EXPERTS)
    blocks_per = (counts + MOE_BM - 1) // MOE_BM
    block_end = jnp.cumsum(blocks_per)
    nb = (S * TOP_K) // MOE_BM + N_EXPERTS
    step = jnp.arange(nb, dtype=jnp.int32)
    block_exp = jnp.minimum(jnp.sum(block_end[None, :] <= step[:, None], axis=1),
                            N_EXPERTS - 1).astype(jnp.int32)
    nblk = block_end[-1:].astype(jnp.int32)
    handed = alloc.reshape(-1, N_EXPERTS).T.reshape(-1)
    used = handed >= 0
    pos = jnp.cumsum(used) - 1
    order = jnp.sum(jnp.where(used[None, :] & (pos[None, :] == step[:, None]), handed[None, :], 0),
                    axis=1)
    order = jnp.where(step < nblk[0], order, step).astype(jnp.int32)
    expert_ids = jnp.arange(N_EXPERTS, dtype=jnp.int32)
    nonempty = blocks_per > 0
    first_at_or_after = jnp.flip(lax.cummin(jnp.flip(jnp.where(nonempty, expert_ids, N_EXPERTS))))
    next_expert = jnp.concatenate(
        [first_at_or_after[1:], jnp.full((1,), N_EXPERTS, jnp.int32)]).astype(jnp.int32)
    expert_slot = ((jnp.cumsum(nonempty) - nonempty) % 2).astype(jnp.int32)

    y = _experts(block_exp, nblk, next_expert, expert_slot, order, xb, w_gate_up,
                 b_gate_up.reshape(N_EXPERTS, 1, 2 * D_FF), w_down,
                 b_down.reshape(N_EXPERTS, 1, D_MODEL))
    return _combine(dest.reshape(S * TOP_K), x1, gates, final_g.reshape(1, D_MODEL), y)


def kernel(x, norm1_g, w_in, pool_w, pool_scale, rel_bias, w_out, norm2_g, router_w, router_b,
           w_gate_up, b_gate_up, w_down, b_down, final_g):
    B, S, D = x.shape
    assert B == 1 and D == D_MODEL and S % ATT_TILE == 0, x.shape
    out = _layer(x.reshape(B * S, D), norm1_g[0], w_in[0], pool_w[0], pool_scale[0], rel_bias,
                 w_out[0], norm2_g[0], router_w[0], router_b[0], w_gate_up[0], b_gate_up[0],
                 w_down[0], b_down[0], final_g)
    return out.reshape(B, S, D)
```

```python
import math

import jax
import jax.numpy as jnp
from jax import lax
from jax.experimental import pallas as pl
from jax.experimental.pallas import tpu as pltpu

F32 = jnp.float32
BF16 = jnp.bfloat16

D_MODEL = 1024
POOL_WIDTH = 256
POOL_WINDOWS = (2, 4, 8, 16)
POOL_GROUP = 64
ATT_WIDTH = 768
HEAD_DIM = 64
N_HEADS = 12
N_PAIRS = N_HEADS // 2
DILATIONS = (1, 4, 16)
ATT_BLOCK = 128
N_BUCKETS = 32
MAX_DISTANCE = 2048
N_EXPERTS = 32
TOP_K = 4
D_FF = 1024
SWIGLU_LIMIT = 7.0
SWIGLU_ALPHA = 1.702
RMS_EPS = 1e-5
NEG_INF = -1e30
LOG2E = math.log2(math.e)

LANES = 128
ATT_TILE = 16 * ATT_BLOCK
BLOCKS_PER_TILE = ATT_TILE // ATT_BLOCK
N_UNITS = len(DILATIONS) * BLOCKS_PER_TILE
ATT_GROUP = 4
QKV_SLABS = 3 * N_PAIRS
IN_TM = 512
POST_TM = 256
MOE_BM = 256
DISPATCH_TB = 512
COMBINE_TB = 256
ROW_UNROLL = 8
VMEM_LIMIT = 56 * 1024 * 1024

_CLASS_ROWS = tuple(ATT_TILE // d for d in DILATIONS)
_SEG_ROWS = tuple(ATT_BLOCK + r for r in _CLASS_ROWS)
_SEG_BASE = (0, _SEG_ROWS[0], _SEG_ROWS[0] + DILATIONS[1] * _SEG_ROWS[1])
KV_ROWS = _SEG_BASE[2] + DILATIONS[2] * _SEG_ROWS[2]


def _rms(x, g):
    return x * lax.rsqrt(jnp.mean(x * x, axis=-1, keepdims=True) + RMS_EPS) * g


ROW_TILE = D_MODEL // LANES


def _store_row_tiles(ref, x):
    n = x.shape[0]
    for c in range(ROW_TILE):
        ref[pl.ds(c, n, stride=ROW_TILE), :] = x[:, c * LANES:(c + 1) * LANES]


def _load_row_tiles(ref, n):
    return jnp.concatenate(
        [ref[pl.ds(c, n, stride=ROW_TILE), :] for c in range(ROW_TILE)], axis=-1)


def _inproj_kernel(x_ref, g_ref, w_ref, u_ref, nat_ref, p4_ref, p16_ref, h_ref, acc_ref,
                   mod4_ref):
    h_ref[...] = _rms(x_ref[...], g_ref[...]).astype(BF16)
    u_ref[...] = jnp.dot(h_ref[...], w_ref[:, :POOL_WIDTH], preferred_element_type=F32)
    for jj in range(QKV_SLABS // 2):
        c0 = POOL_WIDTH + 2 * jj * LANES
        res = jnp.dot(h_ref[...], w_ref[:, c0:c0 + 2 * LANES], preferred_element_type=F32)
        for s in (0, 1):
            j = 2 * jj + s
            slab = res[:, s * LANES:(s + 1) * LANES]
            if j < N_PAIRS:
                slab = slab * (HEAD_DIM ** -0.5 * LOG2E)
            acc_ref[j] = slab
            nat_ref[j] = slab.astype(BF16)
    d4, d16 = DILATIONS[1], DILATIONS[2]
    for j in range(QKV_SLABS):
        for r4 in range(d4):
            rows = acc_ref[j, pl.ds(r4, IN_TM // d4, stride=d4), :]
            p4_ref[j, r4] = rows.astype(BF16)
            mod4_ref[j, r4] = rows
        for r4 in range(d4):
            for c in range(d16 // d4):
                p16_ref[j, r4 + d4 * c] = mod4_ref[
                    j, r4, pl.ds(c, IN_TM // d16, stride=d4), :].astype(BF16)


def _inproj(x, g, w_bf, S):
    sub = ATT_TILE // IN_TM
    n_tiles = S // ATT_TILE

    def res_spec(d):
        return pl.BlockSpec((QKV_SLABS, None, d, IN_TM // d, LANES),
                            lambda i: (0, i // sub, 0, i % sub, 0))

    def res_shape(d):
        return jax.ShapeDtypeStruct((QKV_SLABS, n_tiles, d, ATT_TILE // d, LANES), BF16)

    return pl.pallas_call(
        _inproj_kernel,
        grid=(S // IN_TM,),
        in_specs=[
            pl.BlockSpec((IN_TM, D_MODEL), lambda i: (i, 0)),
            pl.BlockSpec((1, D_MODEL), lambda i: (0, 0)),
            pl.BlockSpec((D_MODEL, POOL_WIDTH + 3 * ATT_WIDTH), lambda i: (0, 0)),
        ],
        out_specs=[
            pl.BlockSpec((IN_TM, POOL_WIDTH), lambda i: (i, 0)),
            pl.BlockSpec((QKV_SLABS, IN_TM, LANES), lambda i: (0, i, 0)),
            res_spec(DILATIONS[1]),
            res_spec(DILATIONS[2]),
        ],
        out_shape=[
            jax.ShapeDtypeStruct((S, POOL_WIDTH), F32),
            jax.ShapeDtypeStruct((QKV_SLABS, S, LANES), BF16),
            res_shape(DILATIONS[1]),
            res_shape(DILATIONS[2]),
        ],
        scratch_shapes=[pltpu.VMEM((IN_TM, D_MODEL), BF16),
                        pltpu.VMEM((QKV_SLABS, IN_TM, LANES), F32),
                        pltpu.VMEM((QKV_SLABS, DILATIONS[1], IN_TM // DILATIONS[1], LANES), F32)],
        compiler_params=pltpu.CompilerParams(
            dimension_semantics=("arbitrary",), vmem_limit_bytes=VMEM_LIMIT),
        name="inproj",
    )(x, g, w_bf)


def _t5_bucket(dist):
    max_exact = N_BUCKETS // 2
    is_small = dist < max_exact
    nf = jnp.maximum(dist, 1).astype(jnp.float32)
    large = max_exact + (jnp.log(nf / max_exact) / math.log(MAX_DISTANCE / max_exact)
                         * (N_BUCKETS - max_exact)).astype(jnp.int32)
    large = jnp.minimum(large, N_BUCKETS - 1)
    return jnp.where(is_small, dist, large)


def _bias_kernel(ids_ref, rb_ref, o_ref):
    h = pl.program_id(0)
    col = lax.broadcasted_iota(jnp.int32, (ATT_BLOCK, 2 * ATT_BLOCK), 1)
    for b in range(len(DILATIONS)):
        ids = ids_ref[b]
        tab = jnp.full(ids.shape, NEG_INF, F32)
        for k in range(N_BUCKETS):
            tab = jnp.where(ids == k, rb_ref[k, h] * LOG2E, tab)
        o_ref[0, b] = tab
        o_ref[1, b] = jnp.where(col < ATT_BLOCK, NEG_INF, tab)


def _bias_tables(rel_bias):
    qi = jnp.arange(ATT_BLOCK)[:, None]
    kj = jnp.arange(2 * ATT_BLOCK)[None, :]
    dist = qi + ATT_BLOCK - kj
    ok = (dist >= 0) & (dist <= ATT_BLOCK)
    ids = jnp.stack([jnp.where(ok, _t5_bucket(jnp.clip(dist, 0) * d), -1) for d in DILATIONS])
    return pl.pallas_call(
        _bias_kernel,
        grid=(N_HEADS,),
        in_specs=[
            pl.BlockSpec((len(DILATIONS), ATT_BLOCK, 2 * ATT_BLOCK), lambda h: (0, 0, 0)),
            pl.BlockSpec(memory_space=pltpu.SMEM),
        ],
        out_specs=pl.BlockSpec((2, len(DILATIONS), None, ATT_BLOCK, 2 * ATT_BLOCK),
                               lambda h: (0, 0, h, 0, 0)),
        out_shape=jax.ShapeDtypeStruct(
            (2, len(DILATIONS), N_HEADS, ATT_BLOCK, 2 * ATT_BLOCK), F32),
        compiler_params=pltpu.CompilerParams(dimension_semantics=("arbitrary",)),
        name="bias_tables",
    )(ids.astype(jnp.int32), rel_bias.astype(F32))


def _attn_kernel(qn, knp, kn, vnp, vn, q4, k4p, k4, v4p, v4, q16, k16p, k16, v16p, v16,
                 bias_ref, o_ref, qbuf, kbuf, vbuf0, vbuf1, sbuf0, sbuf1, pbuf0, pbuf1,
                 obuf, mbuf, lbuf, tbuf):
    i = pl.program_id(0)
    at_start = i == 0
    G = ATT_GROUP
    n_groups = N_UNITS // G
    sbufs, pbufs, vbufs = (sbuf0, sbuf1), (pbuf0, pbuf1), (vbuf0, vbuf1)
    lane = lax.broadcasted_iota(jnp.int32, (ATT_BLOCK, LANES), 1)
    lo = lane < HEAD_DIM
    head_mask = (jnp.where(lo, 1.0, 0.0).astype(BF16), jnp.where(lo, 0.0, 1.0).astype(BF16))

    @pl.when(jnp.logical_and(at_start, pl.program_id(1) == 0))
    def _():
        for buf in sbufs + pbufs:
            buf[...] = jnp.zeros_like(buf)

    for b, q in enumerate((qn, q4, q16)):
        qbuf[b * ATT_TILE:(b + 1) * ATT_TILE] = q[...]

    def fill_k(dst0, rows, val):
        kbuf[dst0:dst0 + rows] = val

    def fill_v(dst0, rows, val):
        for a in (0, 1):
            keep = head_mask[a][0:1]
            vbufs[a][dst0:dst0 + rows] = val * keep + (1 - keep)

    for fill, prevs, curs in ((fill_k, (knp, k4p, k16p), (kn, k4, k16)),
                              (fill_v, (vnp, v4p, v16p), (vn, v4, v16))):
        fill(0, ATT_BLOCK, prevs[0][...])
        fill(ATT_BLOCK, ATT_TILE, curs[0][...])
        for b in (1, 2):
            rows = _CLASS_ROWS[b]
            for c in range(DILATIONS[b]):
                base = _SEG_BASE[b] + c * _SEG_ROWS[b]
                fill(base, ATT_BLOCK, prevs[b][c])
                fill(base + ATT_BLOCK, rows, curs[b][c * rows:(c + 1) * rows])

    def geometry(u):
        b = u // BLOCKS_PER_TILE
        w = u % BLOCKS_PER_TILE
        sh = 4 - 2 * b
        c = lax.shift_right_logical(w, sh)
        a = w - lax.shift_left(c, sh)
        seg = ATT_BLOCK + lax.shift_right_logical(ATT_TILE, 2 * b)
        base = b * _SEG_BASE[1] + jnp.where(b == 2, _SEG_BASE[2] - 2 * _SEG_BASE[1], 0)
        krow = pl.multiple_of(base + c * seg + a * ATT_BLOCK, ATT_BLOCK)
        variant = jnp.logical_and(at_start, a == 0).astype(jnp.int32)
        return b, krow, variant

    def scores(g, sbuf):
        for j in range(G):
            u = g * G + j
            b, krow, variant = geometry(u)
            q = qbuf[pl.ds(pl.multiple_of(u * ATT_BLOCK, ATT_BLOCK), ATT_BLOCK), :]
            k = kbuf[pl.ds(krow, 2 * ATT_BLOCK), :]
            for a in (0, 1):
                s = lax.dot_general(q * head_mask[a], k, (((1,), (1,)), ((), ())),
                                    preferred_element_type=F32)
                sbuf[j, a] = s + bias_ref[variant, b, a]

    def softmax(g, sbuf, pbuf):
        for j in range(G):
            rows = pl.ds(pl.multiple_of((g * G + j) * ATT_BLOCK, ATT_BLOCK), ATT_BLOCK)
            maxes = []
            for a in (0, 1):
                s = sbuf[j, a]
                m = jnp.max(s, axis=-1, keepdims=True)
                pbuf[j, a] = jnp.exp2(s - m).astype(BF16)
                maxes.append(jnp.broadcast_to(m, (ATT_BLOCK, LANES)))
            mbuf[rows, :] = jnp.where(lo, maxes[0], maxes[1])

    def values(g, pbuf):
        for j in range(G):
            u = g * G + j
            _, krow, _ = geometry(u)
            rows = pl.ds(pl.multiple_of(u * ATT_BLOCK, ATT_BLOCK), ATT_BLOCK)
            o0 = jnp.dot(pbuf[j, 0], vbuf0[pl.ds(krow, 2 * ATT_BLOCK), :],
                         preferred_element_type=F32)
            o1 = jnp.dot(pbuf[j, 1], vbuf1[pl.ds(krow, 2 * ATT_BLOCK), :],
                         preferred_element_type=F32)
            obuf[rows, :] = jnp.where(lo, o0, o1)
            lbuf[rows, :] = pltpu.roll(jnp.where(lo, o1, o0), HEAD_DIM, axis=1)

    def trip_pair(tt, carry):
        clamp = lambda g: jnp.clip(g, 0, n_groups - 1)
        for par in (0, 1):
            t = 2 * tt + par
            values(clamp(t - 2), pbufs[par])
            scores(clamp(t), sbufs[par])
            softmax(clamp(t - 1), sbufs[1 - par], pbufs[1 - par])
        return carry

    assert n_groups % 2 == 0
    lax.fori_loop(0, (n_groups + 2) // 2, trip_pair, 0)

    d4, d16 = DILATIONS[1], DILATIONS[2]
    rows4, rows16 = _CLASS_ROWS[1], _CLASS_ROWS[2]
    for n, buf in enumerate((obuf, mbuf, lbuf)):
        for r4 in range(d4):
            for c in range(d16 // d4):
                src = 2 * ATT_TILE + (r4 + d4 * c) * rows16
                tbuf[6 + n, pl.ds(r4 * rows4 + c, rows16, stride=d4), :] = buf[src:src + rows16]
        for r4 in range(d4):
            dst = pl.ds(r4, rows4, stride=d4)
            tbuf[n, dst, :] = buf[ATT_TILE + r4 * rows4:ATT_TILE + (r4 + 1) * rows4]
            tbuf[3 + n, dst, :] = tbuf[6 + n, r4 * rows4:(r4 + 1) * rows4, :]

    def merge(c, carry):
        rows = pl.ds(pl.multiple_of(c * ATT_BLOCK, ATT_BLOCK), ATT_BLOCK)
        ms = (mbuf[rows, :], tbuf[1, rows, :], tbuf[4, rows, :])
        os_ = (obuf[rows, :], tbuf[0, rows, :], tbuf[3, rows, :])
        ls = (lbuf[rows, :], tbuf[2, rows, :], tbuf[5, rows, :])
        mm = jnp.maximum(jnp.maximum(ms[0], ms[1]), ms[2])
        ws = [jnp.exp2(m - mm) for m in ms]
        num = ws[0] * os_[0] + ws[1] * os_[1] + ws[2] * os_[2]
        den = ws[0] * ls[0] + ws[1] * ls[1] + ws[2] * ls[2]
        o_ref[rows, :] = (num / den).astype(o_ref.dtype)
        return carry

    lax.fori_loop(0, BLOCKS_PER_TILE, merge, 0)


def _attention(nat, p4, p16, bias, S):
    n_tiles = S // ATT_TILE
    d4, d16 = DILATIONS[1], DILATIONS[2]
    flat4 = p4.reshape(QKV_SLABS, S, LANES)
    flat16 = p16.reshape(QKV_SLABS, S, LANES)
    blk4 = p4.reshape(QKV_SLABS, n_tiles, d4, _CLASS_ROWS[1] // ATT_BLOCK, ATT_BLOCK, LANES)
    prev = lambda i: jnp.maximum(i - 1, 0)

    def cur(off):
        return pl.BlockSpec((None, ATT_TILE, LANES), lambda i, p: (off + p, i, 0))

    def prev_nat(off):
        return pl.BlockSpec((None, ATT_BLOCK, LANES),
                            lambda i, p: (off + p, jnp.maximum(i * BLOCKS_PER_TILE - 1, 0), 0))

    def prev4(off):
        last = _CLASS_ROWS[1] // ATT_BLOCK - 1
        return pl.BlockSpec((None, None, d4, None, ATT_BLOCK, LANES),
                            lambda i, p: (off + p, prev(i), 0, last, 0, 0))

    def prev16(off):
        return pl.BlockSpec((None, None, d16, ATT_BLOCK, LANES),
                            lambda i, p: (off + p, prev(i), 0, 0, 0))

    k_off, v_off = N_PAIRS, 2 * N_PAIRS
    return pl.pallas_call(
        _attn_kernel,
        grid=(n_tiles, N_PAIRS),
        in_specs=[
            cur(0), prev_nat(k_off), cur(k_off), prev_nat(v_off), cur(v_off),
            cur(0), prev4(k_off), cur(k_off), prev4(v_off), cur(v_off),
            cur(0), prev16(k_off), cur(k_off), prev16(v_off), cur(v_off),
            pl.BlockSpec((2, len(DILATIONS), 2, ATT_BLOCK, 2 * ATT_BLOCK),
                         lambda i, p: (0, 0, p, 0, 0)),
        ],
        out_specs=pl.BlockSpec((None, ATT_TILE, LANES), lambda i, p: (p, i, 0)),
        out_shape=jax.ShapeDtypeStruct((N_PAIRS, S, LANES), BF16),
        scratch_shapes=[
            pltpu.VMEM((N_UNITS * ATT_BLOCK, LANES), BF16),
            pltpu.VMEM((KV_ROWS, LANES), BF16),
            pltpu.VMEM((KV_ROWS, LANES), BF16),
            pltpu.VMEM((KV_ROWS, LANES), BF16),
            pltpu.VMEM((ATT_GROUP, 2, ATT_BLOCK, 2 * ATT_BLOCK), F32),
            pltpu.VMEM((ATT_GROUP, 2, ATT_BLOCK, 2 * ATT_BLOCK), F32),
            pltpu.VMEM((ATT_GROUP, 2, ATT_BLOCK, 2 * ATT_BLOCK), BF16),
            pltpu.VMEM((ATT_GROUP, 2, ATT_BLOCK, 2 * ATT_BLOCK), BF16),
            pltpu.VMEM((N_UNITS * ATT_BLOCK, LANES), F32),
            pltpu.VMEM((N_UNITS * ATT_BLOCK, LANES), F32),
            pltpu.VMEM((N_UNITS * ATT_BLOCK, LANES), F32),
            pltpu.VMEM((9, ATT_TILE, LANES), F32),
        ],
        compiler_params=pltpu.CompilerParams(
            dimension_semantics=("arbitrary", "arbitrary"), vmem_limit_bytes=VMEM_LIMIT),
        name="attention",
    )(nat, nat, nat, nat, nat, flat4, blk4, flat4, blk4, flat4,
      flat16, p16, flat16, p16, flat16, bias)


def _post_kernel(x_ref, u_ref, up_ref, att_ref, pw_ref, ps_ref, wo_ref, g2_ref, rwh_ref, rwl_ref, rb_ref,
                 x1_ref, h2_ref, topi_ref, gate_ref, rank_ref, cnt_ref, ue, carry):
    i = pl.program_id(0)
    tm = POST_TM
    maxw = max(POOL_WINDOWS)

    @pl.when(i == 0)
    def _():
        carry[...] = jnp.zeros_like(carry)

    hist = up_ref[...]
    ue[0:maxw] = jnp.where(i == 0, jnp.zeros_like(hist), hist)
    ue[maxw:] = u_ref[...]
    t_glob = i * tm + lax.broadcasted_iota(jnp.int32, (tm, LANES), 0)
    lane = lax.broadcasted_iota(jnp.int32, (tm, LANES), 1)
    halves = []
    for half, (w_lo, w_hi) in enumerate(((2, 4), (8, 16))):
        cols = slice(half * LANES, (half + 1) * LANES)
        s = ue[maxw:, cols]
        tok = s
        s_lo = None
        for j in range(1, w_hi):
            s = s + ue[maxw - j:maxw - j + tm, cols]
            if j == w_lo - 1:
                s_lo = s
        cnt_lo = jnp.minimum(t_glob + 1, w_lo).astype(F32)
        cnt_hi = jnp.minimum(t_glob + 1, w_hi).astype(F32)
        halves.append(jnp.where(lane < POOL_GROUP, s_lo / cnt_lo, s / cnt_hi) - tok)
    pooled = jnp.concatenate(halves, axis=-1)
    mixed = jnp.dot(pooled.astype(BF16), pw_ref[...], preferred_element_type=F32) * ps_ref[...]

    mix = jnp.concatenate([mixed.astype(BF16)] + [att_ref[j] for j in range(N_PAIRS)], axis=-1)
    x1 = x_ref[...] + jnp.dot(mix, wo_ref[...], preferred_element_type=F32)
    x1_ref[...] = x1
    h2 = _rms(x1, g2_ref[...])
    _store_row_tiles(h2_ref, h2)

    h2_hi = h2.astype(BF16)
    h2_lo = (h2 - h2_hi.astype(F32)).astype(BF16)
    logits = (jnp.dot(h2_hi, rwh_ref[...], preferred_element_type=F32)
              + jnp.dot(h2_hi, rwl_ref[...], preferred_element_type=F32)
              + jnp.dot(h2_lo, rwh_ref[...], preferred_element_type=F32)) + rb_ref[...]
    eidx = lax.broadcasted_iota(jnp.int32, (tm, N_EXPERTS), 1)
    vals = logits
    top_v, top_i = [], []
    for _ in range(TOP_K):
        m = jnp.max(vals, axis=-1, keepdims=True)
        idx = jnp.min(jnp.where(vals == m, eidx, N_EXPERTS), axis=-1, keepdims=True)
        top_v.append(m)
        top_i.append(idx)
        vals = jnp.where(eidx == idx, -jnp.inf, vals)
    ex = [jnp.exp(v - top_v[0]) for v in top_v]
    den = ex[0] + ex[1] + ex[2] + ex[3]

    hot = [(eidx == idx) for idx in top_i]
    cnt = sum(h.astype(F32) for h in hot)
    row = lax.broadcasted_iota(jnp.int32, (tm, tm), 0)
    col = lax.broadcasted_iota(jnp.int32, (tm, tm), 1)
    strict = jnp.where(col < row, 1.0, 0.0).astype(BF16)
    before = jnp.dot(strict, cnt.astype(BF16), preferred_element_type=F32) + carry[...]
    ranks = [jnp.sum(jnp.where(h, before, 0.0), axis=-1, keepdims=True) for h in hot]
    carry[...] = carry[...] + jnp.sum(cnt, axis=0, keepdims=True)
    cnt_ref[...] = carry[...].astype(jnp.int32)

    k4 = lax.broadcasted_iota(jnp.int32, (tm, TOP_K), 1)

    def pack(cols):
        out = jnp.broadcast_to(cols[0], (tm, TOP_K))
        for k in range(1, TOP_K):
            out = jnp.where(k4 == k, jnp.broadcast_to(cols[k], (tm, TOP_K)), out)
        return out

    topi_ref[...] = pack(top_i)
    gate_ref[...] = pack([e / den for e in ex])
    rank_ref[...] = pack(ranks).astype(jnp.int32)


def _post(x, u, att, pw_bd, pool_scale, wo_bf, g2, router_w, router_b, S):
    tm = POST_TM
    maxw = max(POOL_WINDOWS)
    row = lambda w: pl.BlockSpec((tm, w), lambda i: (i, 0))
    const = lambda a, b: pl.BlockSpec((a, b), lambda i: (0, 0))
    rw_hi = router_w.astype(BF16)
    rw_hi_rest = (router_w - rw_hi.astype(F32)).astype(BF16)
    return pl.pallas_call(
        _post_kernel,
        grid=(S // tm,),
        in_specs=[
            row(D_MODEL),
            row(POOL_WIDTH),
            pl.BlockSpec((maxw, POOL_WIDTH), lambda i: (jnp.maximum(i * (tm // maxw) - 1, 0), 0)),
            pl.BlockSpec((N_PAIRS, tm, LANES), lambda i: (0, i, 0)),
            const(POOL_WIDTH, POOL_WIDTH),
            const(1, POOL_WIDTH),
            const(D_MODEL, D_MODEL),
            const(1, D_MODEL),
            const(D_MODEL, N_EXPERTS),
            const(D_MODEL, N_EXPERTS),
            const(1, N_EXPERTS),
        ],
        out_specs=[row(D_MODEL), pl.BlockSpec((tm * ROW_TILE, LANES), lambda i: (i, 0)),
                   row(TOP_K), row(TOP_K), row(TOP_K), const(1, N_EXPERTS)],
        out_shape=[
            jax.ShapeDtypeStruct((S, D_MODEL), F32),
            jax.ShapeDtypeStruct((S * ROW_TILE, LANES), F32),
            jax.ShapeDtypeStruct((S, TOP_K), jnp.int32),
            jax.ShapeDtypeStruct((S, TOP_K), F32),
            jax.ShapeDtypeStruct((S, TOP_K), jnp.int32),
            jax.ShapeDtypeStruct((1, N_EXPERTS), jnp.int32),
        ],
        scratch_shapes=[pltpu.VMEM((tm + maxw, POOL_WIDTH), F32),
                        pltpu.VMEM((1, N_EXPERTS), F32)],
        compiler_params=pltpu.CompilerParams(
            dimension_semantics=("arbitrary",), vmem_limit_bytes=VMEM_LIMIT),
        name="post_attention",
    )(x, u, u, att, pw_bd, pool_scale, wo_bf, g2, rw_hi, rw_hi_rest, router_b)


def _expert_kernel(bexp_ref, nblk_ref, nexte_ref, slot_ref, xb_ref, wgu_hbm, bgu_ref, wd_hbm,
                   bd_ref, y_ref, wgu_f32, wd_f32, wgu_bf, wd_bf, sems):
    b = pl.program_id(0)
    e = bexp_ref[b]
    prev = bexp_ref[jnp.maximum(b - 1, 0)]
    fresh = jnp.logical_and(jnp.logical_or(b == 0, e != prev), b < nblk_ref[0])

    def fetch(expert, slot):
        return (pltpu.make_async_copy(wgu_hbm.at[expert], wgu_f32.at[slot], sems.at[0, slot]),
                pltpu.make_async_copy(wd_hbm.at[expert], wd_f32.at[slot], sems.at[1, slot]))

    @pl.when(b == 0)
    def _():
        for cp in fetch(e, slot_ref[e]):
            cp.start()

    @pl.when(fresh)
    def _():
        slot = slot_ref[e]
        for cp in fetch(e, slot):
            cp.wait()
        nxt = nexte_ref[e]

        @pl.when(nxt < N_EXPERTS)
        def _():
            for cp in fetch(nxt, 1 - slot):
                cp.start()

        wgu_bf[...] = wgu_f32[slot].astype(BF16)
        wd_bf[...] = wd_f32[slot].astype(BF16)

    @pl.when(b < nblk_ref[0])
    def _():
        x = _load_row_tiles(xb_ref, MOE_BM).astype(BF16)
        gu = jnp.dot(x, wgu_bf[...], preferred_element_type=F32) + bgu_ref[...]
        gate = jnp.minimum(gu[:, :D_FF], SWIGLU_LIMIT)
        up = jnp.clip(gu[:, D_FF:], -SWIGLU_LIMIT, SWIGLU_LIMIT)
        act = (up + 1.0) * (gate * jax.nn.sigmoid(gate * SWIGLU_ALPHA))
        y = jnp.dot(act.astype(BF16), wd_bf[...], preferred_element_type=F32) + bd_ref[...]
        _store_row_tiles(y_ref, y)

    @pl.when(b >= nblk_ref[0])
    def _():
        y_ref[...] = jnp.zeros_like(y_ref)


def _experts(block_exp, nblk, next_expert, expert_slot, xb, w_gate_up, b_gate_up, w_down, b_down):
    nb = xb.shape[0] // (MOE_BM * ROW_TILE)
    tiles = (MOE_BM * ROW_TILE, LANES)
    blk = lambda b, be, nk, *_: (jnp.minimum(b, nk[0] - 1), 0)
    wsel = lambda b, be, nk, *_: (be[jnp.minimum(b, nk[0] - 1)], 0, 0)
    return pl.pallas_call(
        _expert_kernel,
        grid_spec=pltpu.PrefetchScalarGridSpec(
            num_scalar_prefetch=4,
            grid=(nb,),
            in_specs=[
                pl.BlockSpec(tiles, blk),
                pl.BlockSpec(memory_space=pl.ANY),
                pl.BlockSpec((None, 1, 2 * D_FF), wsel),
                pl.BlockSpec(memory_space=pl.ANY),
                pl.BlockSpec((None, 1, D_MODEL), wsel),
            ],
            out_specs=pl.BlockSpec(tiles, lambda b, *_: (b, 0)),
            scratch_shapes=[pltpu.VMEM((2, D_MODEL, 2 * D_FF), F32),
                            pltpu.VMEM((2, D_FF, D_MODEL), F32),
                            pltpu.VMEM((D_MODEL, 2 * D_FF), BF16),
                            pltpu.VMEM((D_FF, D_MODEL), BF16),
                            pltpu.SemaphoreType.DMA((2, 2))],
        ),
        out_shape=jax.ShapeDtypeStruct(xb.shape, F32),
        compiler_params=pltpu.CompilerParams(
            dimension_semantics=("arbitrary",), vmem_limit_bytes=VMEM_LIMIT),
        name="moe_experts",
    )(block_exp, nblk, next_expert, expert_slot, xb, w_gate_up, b_gate_up, w_down, b_down)


def _drain_rows(wait_one, n):
    def body(j, carry):
        for _ in range(ROW_UNROLL):
            wait_one()
        return carry

    lax.fori_loop(0, n // ROW_UNROLL, body, 0)


def _dispatch_kernel(dest_ref, padlo_ref, padhi_ref, nblk_ref, h2_ref, xb_hbm, sem):
    i = pl.program_id(0)
    tb = DISPATCH_TB

    def row_copy(j, dst_row):
        src = h2_ref.at[pl.ds(pl.multiple_of(j * ROW_TILE, ROW_TILE), ROW_TILE)]
        dst = xb_hbm.at[pl.ds(pl.multiple_of(dst_row * ROW_TILE, ROW_TILE), ROW_TILE)]
        return pltpu.make_async_copy(src, dst, sem)

    def issue(jo, carry):
        for ji in range(ROW_UNROLL):
            j = jo * ROW_UNROLL + ji
            for k in range(TOP_K):
                row_copy(j, dest_ref[(i * tb + j) * TOP_K + k]).start(priority=k % 2)
        return carry

    lax.fori_loop(0, tb // ROW_UNROLL, issue, 0)
    _drain_rows(lambda: row_copy(0, 0).wait(), tb * TOP_K)

    @pl.when(i == pl.num_programs(0) - 1)
    def _():
        def per_expert(e, carry):
            lo, hi = padlo_ref[e], padhi_ref[e]

            def fill(s, c):
                row_copy(0, s).start()
                return c

            def drain(s, c):
                row_copy(0, 0).wait()
                return c

            lax.fori_loop(lo, hi, fill, 0)
            lax.fori_loop(lo, hi, drain, 0)
            return carry

        lax.fori_loop(0, N_EXPERTS, per_expert, 0)

        def block_copy(b):
            rows = MOE_BM * ROW_TILE
            return pltpu.make_async_copy(
                h2_ref.at[pl.ds(0, rows)],
                xb_hbm.at[pl.ds(pl.multiple_of(b * rows, rows), rows)], sem)

        def fill_block(b, c):
            block_copy(b).start()
            return c

        def drain_block(b, c):
            block_copy(b).wait()
            return c

        n_blocks = xb_hbm.shape[0] // (MOE_BM * ROW_TILE)
        lax.fori_loop(nblk_ref[0], n_blocks, fill_block, 0)
        lax.fori_loop(nblk_ref[0], n_blocks, drain_block, 0)


def _dispatch(dest_flat, padlo, padhi, nblk, h2, n_rows):
    N = h2.shape[0] // ROW_TILE
    assert DISPATCH_TB >= MOE_BM and DISPATCH_TB % ROW_UNROLL == 0
    return pl.pallas_call(
        _dispatch_kernel,
        grid_spec=pltpu.PrefetchScalarGridSpec(
            num_scalar_prefetch=4,
            grid=(N // DISPATCH_TB,),
            in_specs=[pl.BlockSpec((DISPATCH_TB * ROW_TILE, LANES), lambda i, *_: (i, 0))],
            out_specs=pl.BlockSpec(memory_space=pl.ANY),
            scratch_shapes=[pltpu.SemaphoreType.DMA(())],
        ),
        out_shape=jax.ShapeDtypeStruct((n_rows * ROW_TILE, LANES), F32),
        compiler_params=pltpu.CompilerParams(
            dimension_semantics=("arbitrary",), has_side_effects=True,
            vmem_limit_bytes=VMEM_LIMIT),
        name="moe_dispatch",
    )(dest_flat, padlo, padhi, nblk, h2)


def _combine_kernel(dest_ref, x1_ref, gate_ref, g_ref, y_hbm, o_ref, ybuf, sems):
    i = pl.program_id(0)
    n = pl.num_programs(0)
    tb = COMBINE_TB

    def row_copy(src_row, slot, k, j):
        src = y_hbm.at[pl.ds(pl.multiple_of(src_row * ROW_TILE, ROW_TILE), ROW_TILE)]
        dst = ybuf.at[slot, k, pl.ds(pl.multiple_of(j * ROW_TILE, ROW_TILE), ROW_TILE)]
        return pltpu.make_async_copy(src, dst, sems.at[slot])

    def gather(step, slot):
        def issue(jo, carry):
            for ji in range(ROW_UNROLL):
                j = jo * ROW_UNROLL + ji
                for k in range(TOP_K):
                    row_copy(dest_ref[(step * tb + j) * TOP_K + k], slot, k, j).start(
                        priority=k % 2)
            return carry

        lax.fori_loop(0, tb // ROW_UNROLL, issue, 0)

    slot = i % 2

    @pl.when(i == 0)
    def _():
        gather(0, 0)

    @pl.when(i + 1 < n)
    def _():
        gather(i + 1, 1 - slot)

    _drain_rows(lambda: row_copy(0, slot, 0, 0).wait(), tb * TOP_K)

    gates = gate_ref[...]
    moe = gates[:, 0:1] * _load_row_tiles(ybuf.at[slot, 0], tb)
    for k in range(1, TOP_K):
        moe = moe + gates[:, k:k + 1] * _load_row_tiles(ybuf.at[slot, k], tb)
    o_ref[...] = _rms(x1_ref[...] + moe, g_ref[...])


def _combine(dest_flat, x1, gates, final_g, y):
    N = x1.shape[0]
    tb = COMBINE_TB
    return pl.pallas_call(
        _combine_kernel,
        grid_spec=pltpu.PrefetchScalarGridSpec(
            num_scalar_prefetch=1,
            grid=(N // tb,),
            in_specs=[
                pl.BlockSpec((tb, D_MODEL), lambda i, *_: (i, 0)),
                pl.BlockSpec((tb, TOP_K), lambda i, *_: (i, 0)),
                pl.BlockSpec((1, D_MODEL), lambda i, *_: (0, 0)),
                pl.BlockSpec(memory_space=pl.ANY),
            ],
            out_specs=pl.BlockSpec((tb, D_MODEL), lambda i, *_: (i, 0)),
            scratch_shapes=[pltpu.VMEM((2, TOP_K, tb * ROW_TILE, LANES), F32),
                            pltpu.SemaphoreType.DMA((2,))],
        ),
        out_shape=jax.ShapeDtypeStruct((N, D_MODEL), F32),
        compiler_params=pltpu.CompilerParams(
            dimension_semantics=("arbitrary",), vmem_limit_bytes=VMEM_LIMIT),
        name="moe_combine",
    )(dest_flat, x1, gates, final_g, y)


def _layer(x2, norm1_g, w_in, pool_w, pool_scale, rel_bias, w_out, norm2_g,
           router_w, router_b, w_gate_up, b_gate_up, w_down, b_down, final_g):
    S = x2.shape[0]
    u, nat, p4, p16 = _inproj(x2, norm1_g.reshape(1, D_MODEL), w_in.astype(BF16), S)
    att = _attention(nat, p4, p16, _bias_tables(rel_bias), S)

    n_g = len(POOL_WINDOWS)
    pw_bd = (jnp.eye(n_g, dtype=F32)[:, None, :, None] * pool_w[:, :, None, :]).reshape(
        POOL_WIDTH, POOL_WIDTH).astype(BF16)
    x1, h2, topi, gates, rank, counts = _post(
        x2, u, att, pw_bd, pool_scale.reshape(1, POOL_WIDTH), w_out.astype(BF16),
        norm2_g.reshape(1, D_MODEL), router_w, router_b.reshape(1, N_EXPERTS), S)

    counts = counts.reshape(N_EXPERTS)
    blocks_per = (counts + MOE_BM - 1) // MOE_BM
    block_end = jnp.cumsum(blocks_per)
    pstart = ((block_end - blocks_per) * MOE_BM).astype(jnp.int32)
    nb = (S * TOP_K) // MOE_BM + N_EXPERTS
    block_exp = jnp.minimum(
        jnp.sum(block_end[None, :] <= jnp.arange(nb, dtype=jnp.int32)[:, None], axis=1),
        N_EXPERTS - 1).astype(jnp.int32)
    nblk = block_end[-1:].astype(jnp.int32)
    padlo = (pstart + counts).astype(jnp.int32)
    padhi = (pstart + blocks_per * MOE_BM).astype(jnp.int32)
    expert_ids = jnp.arange(N_EXPERTS, dtype=jnp.int32)
    dest = rank + jnp.sum(jnp.where(topi[:, :, None] == expert_ids, pstart, 0), axis=-1)
    dest_flat = dest.reshape(S * TOP_K).astype(jnp.int32)

    xb = _dispatch(dest_flat, padlo, padhi, nblk, h2, nb * MOE_BM)
    nonempty = blocks_per > 0
    first_at_or_after = jnp.flip(lax.cummin(jnp.flip(jnp.where(nonempty, expert_ids, N_EXPERTS))))
    next_expert = jnp.concatenate(
        [first_at_or_after[1:], jnp.full((1,), N_EXPERTS, jnp.int32)]).astype(jnp.int32)
    expert_slot = ((jnp.cumsum(nonempty) - nonempty) % 2).astype(jnp.int32)
    y = _experts(block_exp, nblk, next_expert, expert_slot, xb, w_gate_up,
                 b_gate_up.reshape(N_EXPERTS, 1, 2 * D_FF), w_down,
                 b_down.reshape(N_EXPERTS, 1, D_MODEL))
    return _combine(dest_flat, x1, gates, final_g.reshape(1, D_MODEL), y)


def kernel(x, norm1_g, w_in, pool_w, pool_scale, rel_bias, w_out, norm2_g, router_w, router_b,
           w_gate_up, b_gate_up, w_down, b_down, final_g):
    B, S, D = x.shape
    assert B == 1 and D == D_MODEL and S % ATT_TILE == 0, x.shape
    out = _layer(x.reshape(B * S, D), norm1_g[0], w_in[0], pool_w[0], pool_scale[0], rel_bias,
                 w_out[0], norm2_g[0], router_w[0], router_b[0], w_gate_up[0], b_gate_up[0],
                 w_down[0], b_down[0], final_g)
    return out.reshape(B, S, D)
```

```python
import math

import jax
import jax.numpy as jnp
from jax import lax
from jax.experimental import pallas as pl
from jax.experimental.pallas import tpu as pltpu

F32 = jnp.float32
BF16 = jnp.bfloat16

D_MODEL = 1024
POOL_WIDTH = 256
POOL_WINDOWS = (2, 4, 8, 16)
POOL_GROUP = 64
ATT_WIDTH = 768
HEAD_DIM = 64
N_HEADS = 12
N_PAIRS = N_HEADS // 2
DILATIONS = (1, 4, 16)
ATT_BLOCK = 128
N_BUCKETS = 32
MAX_DISTANCE = 2048
N_EXPERTS = 32
TOP_K = 4
D_FF = 1024
SWIGLU_LIMIT = 7.0
SWIGLU_ALPHA = 1.702
RMS_EPS = 1e-5
NEG_INF = -1e30
LOG2E = math.log2(math.e)

LANES = 128
ATT_TILE = 16 * ATT_BLOCK
BLOCKS_PER_TILE = ATT_TILE // ATT_BLOCK
N_UNITS = len(DILATIONS) * BLOCKS_PER_TILE
ATT_GROUP = 4
QKV_SLABS = 3 * N_PAIRS
IN_TM = 512
POST_TM = 256
MOE_BM = 256
DISPATCH_TB = 512
COMBINE_TB = 256
ROW_UNROLL = 8
VMEM_LIMIT = 56 * 1024 * 1024

_CLASS_ROWS = tuple(ATT_TILE // d for d in DILATIONS)
_SEG_ROWS = tuple(ATT_BLOCK + r for r in _CLASS_ROWS)
_SEG_BASE = (0, _SEG_ROWS[0], _SEG_ROWS[0] + DILATIONS[1] * _SEG_ROWS[1])
KV_ROWS = _SEG_BASE[2] + DILATIONS[2] * _SEG_ROWS[2]


def _rms(x, g):
    return x * lax.rsqrt(jnp.mean(x * x, axis=-1, keepdims=True) + RMS_EPS) * g


ROW_TILE = D_MODEL // LANES


def _store_row_tiles(ref, x):
    n = x.shape[0]
    for c in range(ROW_TILE):
        ref[pl.ds(c, n, stride=ROW_TILE), :] = x[:, c * LANES:(c + 1) * LANES]


def _load_row_tiles(ref, n):
    return jnp.concatenate(
        [ref[pl.ds(c, n, stride=ROW_TILE), :] for c in range(ROW_TILE)], axis=-1)


def _inproj_kernel(x_ref, g_ref, w_ref, u_ref, nat_ref, p4_ref, p16_ref, h_ref, acc_ref,
                   mod4_ref):
    h_ref[...] = _rms(x_ref[...], g_ref[...]).astype(BF16)
    u_ref[...] = jnp.dot(h_ref[...], w_ref[:, :POOL_WIDTH], preferred_element_type=F32)
    for jj in range(QKV_SLABS // 2):
        c0 = POOL_WIDTH + 2 * jj * LANES
        res = jnp.dot(h_ref[...], w_ref[:, c0:c0 + 2 * LANES], preferred_element_type=F32)
        for s in (0, 1):
            j = 2 * jj + s
            slab = res[:, s * LANES:(s + 1) * LANES]
            if j < N_PAIRS:
                slab = slab * (HEAD_DIM ** -0.5 * LOG2E)
            acc_ref[j] = slab
            nat_ref[j] = slab.astype(BF16)
    d4, d16 = DILATIONS[1], DILATIONS[2]
    for j in range(QKV_SLABS):
        for r4 in range(d4):
            rows = acc_ref[j, pl.ds(r4, IN_TM // d4, stride=d4), :]
            p4_ref[j, r4] = rows.astype(BF16)
            mod4_ref[j, r4] = rows
        for r4 in range(d4):
            for c in range(d16 // d4):
                p16_ref[j, r4 + d4 * c] = mod4_ref[
                    j, r4, pl.ds(c, IN_TM // d16, stride=d4), :].astype(BF16)


def _inproj(x, g, w_bf, S):
    sub = ATT_TILE // IN_TM
    n_tiles = S // ATT_TILE

    def res_spec(d):
        return pl.BlockSpec((QKV_SLABS, None, d, IN_TM // d, LANES),
                            lambda i: (0, i // sub, 0, i % sub, 0))

    def res_shape(d):
        return jax.ShapeDtypeStruct((QKV_SLABS, n_tiles, d, ATT_TILE // d, LANES), BF16)

    return pl.pallas_call(
        _inproj_kernel,
        grid=(S // IN_TM,),
        in_specs=[
            pl.BlockSpec((IN_TM, D_MODEL), lambda i: (i, 0)),
            pl.BlockSpec((1, D_MODEL), lambda i: (0, 0)),
            pl.BlockSpec((D_MODEL, POOL_WIDTH + 3 * ATT_WIDTH), lambda i: (0, 0)),
        ],
        out_specs=[
            pl.BlockSpec((IN_TM, POOL_WIDTH), lambda i: (i, 0)),
            pl.BlockSpec((QKV_SLABS, IN_TM, LANES), lambda i: (0, i, 0)),
            res_spec(DILATIONS[1]),
            res_spec(DILATIONS[2]),
        ],
        out_shape=[
            jax.ShapeDtypeStruct((S, POOL_WIDTH), F32),
            jax.ShapeDtypeStruct((QKV_SLABS, S, LANES), BF16),
            res_shape(DILATIONS[1]),
            res_shape(DILATIONS[2]),
        ],
        scratch_shapes=[pltpu.VMEM((IN_TM, D_MODEL), BF16),
                        pltpu.VMEM((QKV_SLABS, IN_TM, LANES), F32),
                        pltpu.VMEM((QKV_SLABS, DILATIONS[1], IN_TM // DILATIONS[1], LANES), F32)],
        compiler_params=pltpu.CompilerParams(
            dimension_semantics=("arbitrary",), vmem_limit_bytes=VMEM_LIMIT),
        name="inproj",
    )(x, g, w_bf)


def _t5_bucket(dist):
    max_exact = N_BUCKETS // 2
    is_small = dist < max_exact
    nf = jnp.maximum(dist, 1).astype(jnp.float32)
    large = max_exact + (jnp.log(nf / max_exact) / math.log(MAX_DISTANCE / max_exact)
                         * (N_BUCKETS - max_exact)).astype(jnp.int32)
    large = jnp.minimum(large, N_BUCKETS - 1)
    return jnp.where(is_small, dist, large)


def _bias_kernel(ids_ref, rb_ref, o_ref):
    h = pl.program_id(0)
    col = lax.broadcasted_iota(jnp.int32, (ATT_BLOCK, 2 * ATT_BLOCK), 1)
    for b in range(len(DILATIONS)):
        ids = ids_ref[b]
        tab = jnp.full(ids.shape, NEG_INF, F32)
        for k in range(N_BUCKETS):
            tab = jnp.where(ids == k, rb_ref[k, h] * LOG2E, tab)
        o_ref[0, b] = tab
        o_ref[1, b] = jnp.where(col < ATT_BLOCK, NEG_INF, tab)


def _bias_tables(rel_bias):
    qi = jnp.arange(ATT_BLOCK)[:, None]
    kj = jnp.arange(2 * ATT_BLOCK)[None, :]
    dist = qi + ATT_BLOCK - kj
    ok = (dist >= 0) & (dist <= ATT_BLOCK)
    ids = jnp.stack([jnp.where(ok, _t5_bucket(jnp.clip(dist, 0) * d), -1) for d in DILATIONS])
    return pl.pallas_call(
        _bias_kernel,
        grid=(N_HEADS,),
        in_specs=[
            pl.BlockSpec((len(DILATIONS), ATT_BLOCK, 2 * ATT_BLOCK), lambda h: (0, 0, 0)),
            pl.BlockSpec(memory_space=pltpu.SMEM),
        ],
        out_specs=pl.BlockSpec((2, len(DILATIONS), None, ATT_BLOCK, 2 * ATT_BLOCK),
                               lambda h: (0, 0, h, 0, 0)),
        out_shape=jax.ShapeDtypeStruct(
            (2, len(DILATIONS), N_HEADS, ATT_BLOCK, 2 * ATT_BLOCK), F32),
        compiler_params=pltpu.CompilerParams(dimension_semantics=("arbitrary",)),
        name="bias_tables",
    )(ids.astype(jnp.int32), rel_bias.astype(F32))


def _attn_kernel(qn, knp, kn, vnp, vn, q4, k4p, k4, v4p, v4, q16, k16p, k16, v16p, v16,
                 bias_ref, o_ref, qbuf, kbuf, vbuf0, vbuf1, sbuf0, sbuf1, pbuf0, pbuf1,
                 obuf, mbuf, lbuf, tbuf):
    i = pl.program_id(0)
    at_start = i == 0
    G = ATT_GROUP
    n_groups = N_UNITS // G
    sbufs, pbufs, vbufs = (sbuf0, sbuf1), (pbuf0, pbuf1), (vbuf0, vbuf1)
    lane = lax.broadcasted_iota(jnp.int32, (ATT_BLOCK, LANES), 1)
    lo = lane < HEAD_DIM
    head_mask = (jnp.where(lo, 1.0, 0.0).astype(BF16), jnp.where(lo, 0.0, 1.0).astype(BF16))

    @pl.when(jnp.logical_and(at_start, pl.program_id(1) == 0))
    def _():
        for buf in sbufs + pbufs:
            buf[...] = jnp.zeros_like(buf)

    for b, q in enumerate((qn, q4, q16)):
        qbuf[b * ATT_TILE:(b + 1) * ATT_TILE] = q[...]

    def fill_k(dst0, rows, val):
        kbuf[dst0:dst0 + rows] = val

    def fill_v(dst0, rows, val):
        for a in (0, 1):
            keep = jnp.broadcast_to(head_mask[a][0:1], val.shape) > 0
            vbufs[a][dst0:dst0 + rows] = jnp.where(keep, val, jnp.ones_like(val))

    for fill, prevs, curs in ((fill_k, (knp, k4p, k16p), (kn, k4, k16)),
                              (fill_v, (vnp, v4p, v16p), (vn, v4, v16))):
        fill(0, ATT_BLOCK, prevs[0][...])
        fill(ATT_BLOCK, ATT_TILE, curs[0][...])
        for b in (1, 2):
            rows = _CLASS_ROWS[b]
            for c in range(DILATIONS[b]):
                base = _SEG_BASE[b] + c * _SEG_ROWS[b]
                fill(base, ATT_BLOCK, prevs[b][c])
                fill(base + ATT_BLOCK, rows, curs[b][c * rows:(c + 1) * rows])

    def geometry(u):
        b = u // BLOCKS_PER_TILE
        w = u % BLOCKS_PER_TILE
        sh = 4 - 2 * b
        c = lax.shift_right_logical(w, sh)
        a = w - lax.shift_left(c, sh)
        seg = ATT_BLOCK + lax.shift_right_logical(ATT_TILE, 2 * b)
        base = b * _SEG_BASE[1] + jnp.where(b == 2, _SEG_BASE[2] - 2 * _SEG_BASE[1], 0)
        krow = pl.multiple_of(base + c * seg + a * ATT_BLOCK, ATT_BLOCK)
        variant = jnp.logical_and(at_start, a == 0).astype(jnp.int32)
        return b, krow, variant

    def scores(g, sbuf):
        for j in range(G):
            u = g * G + j
            b, krow, variant = geometry(u)
            q = qbuf[pl.ds(pl.multiple_of(u * ATT_BLOCK, ATT_BLOCK), ATT_BLOCK), :]
            k = kbuf[pl.ds(krow, 2 * ATT_BLOCK), :]
            for a in (0, 1):
                s = lax.dot_general(q * head_mask[a], k, (((1,), (1,)), ((), ())),
                                    preferred_element_type=F32)
                sbuf[j, a] = s + bias_ref[variant, b, a]

    def softmax(g, sbuf, pbuf):
        for j in range(G):
            rows = pl.ds(pl.multiple_of((g * G + j) * ATT_BLOCK, ATT_BLOCK), ATT_BLOCK)
            maxes = []
            for a in (0, 1):
                s = sbuf[j, a]
                m = jnp.max(s, axis=-1, keepdims=True)
                pbuf[j, a] = jnp.exp2(s - m).astype(BF16)
                maxes.append(jnp.broadcast_to(m, (ATT_BLOCK, LANES)))
            mbuf[rows, :] = jnp.where(lo, maxes[0], maxes[1])

    def values(g, pbuf):
        for j in range(G):
            u = g * G + j
            _, krow, _ = geometry(u)
            rows = pl.ds(pl.multiple_of(u * ATT_BLOCK, ATT_BLOCK), ATT_BLOCK)
            o0 = jnp.dot(pbuf[j, 0], vbuf0[pl.ds(krow, 2 * ATT_BLOCK), :],
                         preferred_element_type=F32)
            o1 = jnp.dot(pbuf[j, 1], vbuf1[pl.ds(krow, 2 * ATT_BLOCK), :],
                         preferred_element_type=F32)
            obuf[rows, :] = jnp.where(lo, o0, o1)
            lbuf[rows, :] = pltpu.roll(jnp.where(lo, o1, o0), HEAD_DIM, axis=1)

    def trip_pair(tt, carry):
        clamp = lambda g: jnp.clip(g, 0, n_groups - 1)
        for par in (0, 1):
            t = 2 * tt + par
            values(clamp(t - 2), pbufs[par])
            scores(clamp(t), sbufs[par])
            softmax(clamp(t - 1), sbufs[1 - par], pbufs[1 - par])
        return carry

    assert n_groups % 2 == 0
    lax.fori_loop(0, (n_groups + 2) // 2, trip_pair, 0)

    d4, d16 = DILATIONS[1], DILATIONS[2]
    rows4, rows16 = _CLASS_ROWS[1], _CLASS_ROWS[2]
    for n, buf in enumerate((obuf, mbuf, lbuf)):
        for r4 in range(d4):
            for c in range(d16 // d4):
                src = 2 * ATT_TILE + (r4 + d4 * c) * rows16
                tbuf[6 + n, pl.ds(r4 * rows4 + c, rows16, stride=d4), :] = buf[src:src + rows16]
        for r4 in range(d4):
            dst = pl.ds(r4, rows4, stride=d4)
            tbuf[n, dst, :] = buf[ATT_TILE + r4 * rows4:ATT_TILE + (r4 + 1) * rows4]
            tbuf[3 + n, dst, :] = tbuf[6 + n, r4 * rows4:(r4 + 1) * rows4, :]

    def merge(c, carry):
        rows = pl.ds(pl.multiple_of(c * ATT_BLOCK, ATT_BLOCK), ATT_BLOCK)
        ms = (mbuf[rows, :], tbuf[1, rows, :], tbuf[4, rows, :])
        os_ = (obuf[rows, :], tbuf[0, rows, :], tbuf[3, rows, :])
        ls = (lbuf[rows, :], tbuf[2, rows, :], tbuf[5, rows, :])
        mm = jnp.maximum(jnp.maximum(ms[0], ms[1]), ms[2])
        ws = [jnp.exp2(m - mm) for m in ms]
        num = ws[0] * os_[0] + ws[1] * os_[1] + ws[2] * os_[2]
        den = ws[0] * ls[0] + ws[1] * ls[1] + ws[2] * ls[2]
        o_ref[rows, :] = (num / den).astype(o_ref.dtype)
        return carry

    lax.fori_loop(0, BLOCKS_PER_TILE, merge, 0)


def _attention(nat, p4, p16, bias, S):
    n_tiles = S // ATT_TILE
    d4, d16 = DILATIONS[1], DILATIONS[2]
    flat4 = p4.reshape(QKV_SLABS, S, LANES)
    flat16 = p16.reshape(QKV_SLABS, S, LANES)
    blk4 = p4.reshape(QKV_SLABS, n_tiles, d4, _CLASS_ROWS[1] // ATT_BLOCK, ATT_BLOCK, LANES)
    prev = lambda i: jnp.maximum(i - 1, 0)

    def cur(off):
        return pl.BlockSpec((None, ATT_TILE, LANES), lambda i, p: (off + p, i, 0))

    def prev_nat(off):
        return pl.BlockSpec((None, ATT_BLOCK, LANES),
                            lambda i, p: (off + p, jnp.maximum(i * BLOCKS_PER_TILE - 1, 0), 0))

    def prev4(off):
        last = _CLASS_ROWS[1] // ATT_BLOCK - 1
        return pl.BlockSpec((None, None, d4, None, ATT_BLOCK, LANES),
                            lambda i, p: (off + p, prev(i), 0, last, 0, 0))

    def prev16(off):
        return pl.BlockSpec((None, None, d16, ATT_BLOCK, LANES),
                            lambda i, p: (off + p, prev(i), 0, 0, 0))

    k_off, v_off = N_PAIRS, 2 * N_PAIRS
    return pl.pallas_call(
        _attn_kernel,
        grid=(n_tiles, N_PAIRS),
        in_specs=[
            cur(0), prev_nat(k_off), cur(k_off), prev_nat(v_off), cur(v_off),
            cur(0), prev4(k_off), cur(k_off), prev4(v_off), cur(v_off),
            cur(0), prev16(k_off), cur(k_off), prev16(v_off), cur(v_off),
            pl.BlockSpec((2, len(DILATIONS), 2, ATT_BLOCK, 2 * ATT_BLOCK),
                         lambda i, p: (0, 0, p, 0, 0)),
        ],
        out_specs=pl.BlockSpec((None, ATT_TILE, LANES), lambda i, p: (p, i, 0)),
        out_shape=jax.ShapeDtypeStruct((N_PAIRS, S, LANES), BF16),
        scratch_shapes=[
            pltpu.VMEM((N_UNITS * ATT_BLOCK, LANES), BF16),
            pltpu.VMEM((KV_ROWS, LANES), BF16),
            pltpu.VMEM((KV_ROWS, LANES), BF16),
            pltpu.VMEM((KV_ROWS, LANES), BF16),
            pltpu.VMEM((ATT_GROUP, 2, ATT_BLOCK, 2 * ATT_BLOCK), F32),
            pltpu.VMEM((ATT_GROUP, 2, ATT_BLOCK, 2 * ATT_BLOCK), F32),
            pltpu.VMEM((ATT_GROUP, 2, ATT_BLOCK, 2 * ATT_BLOCK), BF16),
            pltpu.VMEM((ATT_GROUP, 2, ATT_BLOCK, 2 * ATT_BLOCK), BF16),
            pltpu.VMEM((N_UNITS * ATT_BLOCK, LANES), F32),
            pltpu.VMEM((N_UNITS * ATT_BLOCK, LANES), F32),
            pltpu.VMEM((N_UNITS * ATT_BLOCK, LANES), F32),
            pltpu.VMEM((9, ATT_TILE, LANES), F32),
        ],
        compiler_params=pltpu.CompilerParams(
            dimension_semantics=("arbitrary", "arbitrary"), vmem_limit_bytes=VMEM_LIMIT),
        name="attention",
    )(nat, nat, nat, nat, nat, flat4, blk4, flat4, blk4, flat4,
      flat16, p16, flat16, p16, flat16, bias)


def _post_kernel(x_ref, u_ref, up_ref, att_ref, pw_ref, ps_ref, wo_ref, g2_ref, rwh_ref, rwl_ref, rb_ref,
                 x1_ref, h2_ref, topi_ref, gate_ref, rank_ref, cnt_ref, ue, carry):
    i = pl.program_id(0)
    tm = POST_TM
    maxw = max(POOL_WINDOWS)

    @pl.when(i == 0)
    def _():
        carry[...] = jnp.zeros_like(carry)

    hist = up_ref[...]
    ue[0:maxw] = jnp.where(i == 0, jnp.zeros_like(hist), hist)
    ue[maxw:] = u_ref[...]
    t_glob = i * tm + lax.broadcasted_iota(jnp.int32, (tm, LANES), 0)
    lane = lax.broadcasted_iota(jnp.int32, (tm, LANES), 1)
    halves = []
    for half, (w_lo, w_hi) in enumerate(((2, 4), (8, 16))):
        cols = slice(half * LANES, (half + 1) * LANES)
        s = ue[maxw:, cols]
        tok = s
        s_lo = None
        for j in range(1, w_hi):
            s = s + ue[maxw - j:maxw - j + tm, cols]
            if j == w_lo - 1:
                s_lo = s
        cnt_lo = jnp.minimum(t_glob + 1, w_lo).astype(F32)
        cnt_hi = jnp.minimum(t_glob + 1, w_hi).astype(F32)
        halves.append(jnp.where(lane < POOL_GROUP, s_lo / cnt_lo, s / cnt_hi) - tok)
    pooled = jnp.concatenate(halves, axis=-1)
    mixed = jnp.dot(pooled.astype(BF16), pw_ref[...], preferred_element_type=F32) * ps_ref[...]

    mix = jnp.concatenate([mixed.astype(BF16)] + [att_ref[j] for j in range(N_PAIRS)], axis=-1)
    x1 = x_ref[...] + jnp.dot(mix, wo_ref[...], preferred_element_type=F32)
    x1_ref[...] = x1
    h2 = _rms(x1, g2_ref[...])
    _store_row_tiles(h2_ref, h2)

    h2_hi = h2.astype(BF16)
    h2_lo = (h2 - h2_hi.astype(F32)).astype(BF16)
    logits = (jnp.dot(h2_hi, rwh_ref[...], preferred_element_type=F32)
              + jnp.dot(h2_hi, rwl_ref[...], preferred_element_type=F32)
              + jnp.dot(h2_lo, rwh_ref[...], preferred_element_type=F32)) + rb_ref[...]
    eidx = lax.broadcasted_iota(jnp.int32, (tm, N_EXPERTS), 1)
    vals = logits
    top_v, top_i = [], []
    for _ in range(TOP_K):
        m = jnp.max(vals, axis=-1, keepdims=True)
        idx = jnp.min(jnp.where(vals == m, eidx, N_EXPERTS), axis=-1, keepdims=True)
        top_v.append(m)
        top_i.append(idx)
        vals = jnp.where(eidx == idx, -jnp.inf, vals)
    ex = [jnp.exp(v - top_v[0]) for v in top_v]
    den = ex[0] + ex[1] + ex[2] + ex[3]

    hot = [(eidx == idx) for idx in top_i]
    cnt = sum(h.astype(F32) for h in hot)
    row = lax.broadcasted_iota(jnp.int32, (tm, tm), 0)
    col = lax.broadcasted_iota(jnp.int32, (tm, tm), 1)
    strict = jnp.where(col < row, 1.0, 0.0).astype(BF16)
    before = jnp.dot(strict, cnt.astype(BF16), preferred_element_type=F32) + carry[...]
    ranks = [jnp.sum(jnp.where(h, before, 0.0), axis=-1, keepdims=True) for h in hot]
    carry[...] = carry[...] + jnp.sum(cnt, axis=0, keepdims=True)
    cnt_ref[...] = carry[...].astype(jnp.int32)

    k4 = lax.broadcasted_iota(jnp.int32, (tm, TOP_K), 1)

    def pack(cols):
        out = jnp.broadcast_to(cols[0], (tm, TOP_K))
        for k in range(1, TOP_K):
            out = jnp.where(k4 == k, jnp.broadcast_to(cols[k], (tm, TOP_K)), out)
        return out

    topi_ref[...] = pack(top_i)
    gate_ref[...] = pack([e / den for e in ex])
    rank_ref[...] = pack(ranks).astype(jnp.int32)


def _post(x, u, att, pw_bd, pool_scale, wo_bf, g2, router_w, router_b, S):
    tm = POST_TM
    maxw = max(POOL_WINDOWS)
    row = lambda w: pl.BlockSpec((tm, w), lambda i: (i, 0))
    const = lambda a, b: pl.BlockSpec((a, b), lambda i: (0, 0))
    rw_hi = router_w.astype(BF16)
    rw_hi_rest = (router_w - rw_hi.astype(F32)).astype(BF16)
    return pl.pallas_call(
        _post_kernel,
        grid=(S // tm,),
        in_specs=[
            row(D_MODEL),
            row(POOL_WIDTH),
            pl.BlockSpec((maxw, POOL_WIDTH), lambda i: (jnp.maximum(i * (tm // maxw) - 1, 0), 0)),
            pl.BlockSpec((N_PAIRS, tm, LANES), lambda i: (0, i, 0)),
            const(POOL_WIDTH, POOL_WIDTH),
            const(1, POOL_WIDTH),
            const(D_MODEL, D_MODEL),
            const(1, D_MODEL),
            const(D_MODEL, N_EXPERTS),
            const(D_MODEL, N_EXPERTS),
            const(1, N_EXPERTS),
        ],
        out_specs=[row(D_MODEL), pl.BlockSpec((tm * ROW_TILE, LANES), lambda i: (i, 0)),
                   row(TOP_K), row(TOP_K), row(TOP_K), const(1, N_EXPERTS)],
        out_shape=[
            jax.ShapeDtypeStruct((S, D_MODEL), F32),
            jax.ShapeDtypeStruct((S * ROW_TILE, LANES), F32),
            jax.ShapeDtypeStruct((S, TOP_K), jnp.int32),
            jax.ShapeDtypeStruct((S, TOP_K), F32),
            jax.ShapeDtypeStruct((S, TOP_K), jnp.int32),
            jax.ShapeDtypeStruct((1, N_EXPERTS), jnp.int32),
        ],
        scratch_shapes=[pltpu.VMEM((tm + maxw, POOL_WIDTH), F32),
                        pltpu.VMEM((1, N_EXPERTS), F32)],
        compiler_params=pltpu.CompilerParams(
            dimension_semantics=("arbitrary",), vmem_limit_bytes=VMEM_LIMIT),
        name="post_attention",
    )(x, u, u, att, pw_bd, pool_scale, wo_bf, g2, rw_hi, rw_hi_rest, router_b)


def _expert_kernel(bexp_ref, nblk_ref, nexte_ref, slot_ref, xb_ref, wgu_hbm, bgu_ref, wd_hbm,
                   bd_ref, y_ref, wgu_f32, wd_f32, wgu_bf, wd_bf, sems):
    b = pl.program_id(0)
    e = bexp_ref[b]
    prev = bexp_ref[jnp.maximum(b - 1, 0)]
    fresh = jnp.logical_and(jnp.logical_or(b == 0, e != prev), b < nblk_ref[0])

    def fetch(expert, slot):
        return (pltpu.make_async_copy(wgu_hbm.at[expert], wgu_f32.at[slot], sems.at[0, slot]),
                pltpu.make_async_copy(wd_hbm.at[expert], wd_f32.at[slot], sems.at[1, slot]))

    @pl.when(b == 0)
    def _():
        for cp in fetch(e, slot_ref[e]):
            cp.start()

    @pl.when(fresh)
    def _():
        slot = slot_ref[e]
        for cp in fetch(e, slot):
            cp.wait()
        nxt = nexte_ref[e]

        @pl.when(nxt < N_EXPERTS)
        def _():
            for cp in fetch(nxt, 1 - slot):
                cp.start()

        wgu_bf[...] = wgu_f32[slot].astype(BF16)
        wd_bf[...] = wd_f32[slot].astype(BF16)

    @pl.when(b < nblk_ref[0])
    def _():
        x = _load_row_tiles(xb_ref, MOE_BM).astype(BF16)
        gu = jnp.dot(x, wgu_bf[...], preferred_element_type=F32) + bgu_ref[...]
        gate = jnp.minimum(gu[:, :D_FF], SWIGLU_LIMIT)
        up = jnp.clip(gu[:, D_FF:], -SWIGLU_LIMIT, SWIGLU_LIMIT)
        act = (up + 1.0) * (gate * jax.nn.sigmoid(gate * SWIGLU_ALPHA))
        y = jnp.dot(act.astype(BF16), wd_bf[...], preferred_element_type=F32) + bd_ref[...]
        _store_row_tiles(y_ref, y)

    @pl.when(b >= nblk_ref[0])
    def _():
        y_ref[...] = jnp.zeros_like(y_ref)


def _experts(block_exp, nblk, next_expert, expert_slot, xb, w_gate_up, b_gate_up, w_down, b_down):
    nb = xb.shape[0] // (MOE_BM * ROW_TILE)
    tiles = (MOE_BM * ROW_TILE, LANES)
    blk = lambda b, be, nk, *_: (jnp.minimum(b, nk[0] - 1), 0)
    wsel = lambda b, be, nk, *_: (be[jnp.minimum(b, nk[0] - 1)], 0, 0)
    return pl.pallas_call(
        _expert_kernel,
        grid_spec=pltpu.PrefetchScalarGridSpec(
            num_scalar_prefetch=4,
            grid=(nb,),
            in_specs=[
                pl.BlockSpec(tiles, blk),
                pl.BlockSpec(memory_space=pl.ANY),
                pl.BlockSpec((None, 1, 2 * D_FF), wsel),
                pl.BlockSpec(memory_space=pl.ANY),
                pl.BlockSpec((None, 1, D_MODEL), wsel),
            ],
            out_specs=pl.BlockSpec(tiles, lambda b, *_: (b, 0)),
            scratch_shapes=[pltpu.VMEM((2, D_MODEL, 2 * D_FF), F32),
                            pltpu.VMEM((2, D_FF, D_MODEL), F32),
                            pltpu.VMEM((D_MODEL, 2 * D_FF), BF16),
                            pltpu.VMEM((D_FF, D_MODEL), BF16),
                            pltpu.SemaphoreType.DMA((2, 2))],
        ),
        out_shape=jax.ShapeDtypeStruct(xb.shape, F32),
        compiler_params=pltpu.CompilerParams(
            dimension_semantics=("arbitrary",), vmem_limit_bytes=VMEM_LIMIT),
        name="moe_experts",
    )(block_exp, nblk, next_expert, expert_slot, xb, w_gate_up, b_gate_up, w_down, b_down)


def _drain_rows(wait_one, n):
    def body(j, carry):
        for _ in range(ROW_UNROLL):
            wait_one()
        return carry

    lax.fori_loop(0, n // ROW_UNROLL, body, 0)


def _dispatch_kernel(dest_ref, padlo_ref, padhi_ref, nblk_ref, h2_ref, xb_hbm, sem):
    i = pl.program_id(0)
    tb = DISPATCH_TB

    def row_copy(j, dst_row):
        src = h2_ref.at[pl.ds(pl.multiple_of(j * ROW_TILE, ROW_TILE), ROW_TILE)]
        dst = xb_hbm.at[pl.ds(pl.multiple_of(dst_row * ROW_TILE, ROW_TILE), ROW_TILE)]
        return pltpu.make_async_copy(src, dst, sem)

    def issue(jo, carry):
        for ji in range(ROW_UNROLL):
            j = jo * ROW_UNROLL + ji
            for k in range(TOP_K):
                row_copy(j, dest_ref[(i * tb + j) * TOP_K + k]).start(priority=k % 2)
        return carry

    lax.fori_loop(0, tb // ROW_UNROLL, issue, 0)
    _drain_rows(lambda: row_copy(0, 0).wait(), tb * TOP_K)

    @pl.when(i == pl.num_programs(0) - 1)
    def _():
        def per_expert(e, carry):
            lo, hi = padlo_ref[e], padhi_ref[e]

            def fill(s, c):
                row_copy(0, s).start()
                return c

            def drain(s, c):
                row_copy(0, 0).wait()
                return c

            lax.fori_loop(lo, hi, fill, 0)
            lax.fori_loop(lo, hi, drain, 0)
            return carry

        lax.fori_loop(0, N_EXPERTS, per_expert, 0)

        def block_copy(b):
            rows = MOE_BM * ROW_TILE
            return pltpu.make_async_copy(
                h2_ref.at[pl.ds(0, rows)],
                xb_hbm.at[pl.ds(pl.multiple_of(b * rows, rows), rows)], sem)

        def fill_block(b, c):
            block_copy(b).start()
            return c

        def drain_block(b, c):
            block_copy(b).wait()
            return c

        n_blocks = xb_hbm.shape[0] // (MOE_BM * ROW_TILE)
        lax.fori_loop(nblk_ref[0], n_blocks, fill_block, 0)
        lax.fori_loop(nblk_ref[0], n_blocks, drain_block, 0)


def _dispatch(dest_flat, padlo, padhi, nblk, h2, n_rows):
    N = h2.shape[0] // ROW_TILE
    assert DISPATCH_TB >= MOE_BM and DISPATCH_TB % ROW_UNROLL == 0
    return pl.pallas_call(
        _dispatch_kernel,
        grid_spec=pltpu.PrefetchScalarGridSpec(
            num_scalar_prefetch=4,
            grid=(N // DISPATCH_TB,),
            in_specs=[pl.BlockSpec((DISPATCH_TB * ROW_TILE, LANES), lambda i, *_: (i, 0))],
            out_specs=pl.BlockSpec(memory_space=pl.ANY),
            scratch_shapes=[pltpu.SemaphoreType.DMA(())],
        ),
        out_shape=jax.ShapeDtypeStruct((n_rows * ROW_TILE, LANES), F32),
        compiler_params=pltpu.CompilerParams(
            dimension_semantics=("arbitrary",), has_side_effects=True,
            vmem_limit_bytes=VMEM_LIMIT),
        name="moe_dispatch",
    )(dest_flat, padlo, padhi, nblk, h2)


def _combine_kernel(dest_ref, x1_ref, gate_ref, g_ref, y_hbm, o_ref, ybuf, sems):
    i = pl.program_id(0)
    n = pl.num_programs(0)
    tb = COMBINE_TB

    def row_copy(src_row, slot, k, j):
        src = y_hbm.at[pl.ds(pl.multiple_of(src_row * ROW_TILE, ROW_TILE), ROW_TILE)]
        dst = ybuf.at[slot, k, pl.ds(pl.multiple_of(j * ROW_TILE, ROW_TILE), ROW_TILE)]
        return pltpu.make_async_copy(src, dst, sems.at[slot])

    def gather(step, slot):
        def issue(jo, carry):
            for ji in range(ROW_UNROLL):
                j = jo * ROW_UNROLL + ji
                for k in range(TOP_K):
                    row_copy(dest_ref[(step * tb + j) * TOP_K + k], slot, k, j).start(
                        priority=k % 2)
            return carry

        lax.fori_loop(0, tb // ROW_UNROLL, issue, 0)

    slot = i % 2

    @pl.when(i == 0)
    def _():
        gather(0, 0)

    @pl.when(i + 1 < n)
    def _():
        gather(i + 1, 1 - slot)

    _drain_rows(lambda: row_copy(0, slot, 0, 0).wait(), tb * TOP_K)

    gates = gate_ref[...]
    moe = gates[:, 0:1] * _load_row_tiles(ybuf.at[slot, 0], tb)
    for k in range(1, TOP_K):
        moe = moe + gates[:, k:k + 1] * _load_row_tiles(ybuf.at[slot, k], tb)
    o_ref[...] = _rms(x1_ref[...] + moe, g_ref[...])


def _combine(dest_flat, x1, gates, final_g, y):
    N = x1.shape[0]
    tb = COMBINE_TB
    return pl.pallas_call(
        _combine_kernel,
        grid_spec=pltpu.PrefetchScalarGridSpec(
            num_scalar_prefetch=1,
            grid=(N // tb,),
            in_specs=[
                pl.BlockSpec((tb, D_MODEL), lambda i, *_: (i, 0)),
                pl.BlockSpec((tb, TOP_K), lambda i, *_: (i, 0)),
                pl.BlockSpec((1, D_MODEL), lambda i, *_: (0, 0)),
                pl.BlockSpec(memory_space=pl.ANY),
            ],
            out_specs=pl.BlockSpec((tb, D_MODEL), lambda i, *_: (i, 0)),
            scratch_shapes=[pltpu.VMEM((2, TOP_K, tb * ROW_TILE, LANES), F32),
                            pltpu.SemaphoreType.DMA((2,))],
        ),
        out_shape=jax.ShapeDtypeStruct((N, D_MODEL), F32),
        compiler_params=pltpu.CompilerParams(
            dimension_semantics=("arbitrary",), vmem_limit_bytes=VMEM_LIMIT),
        name="moe_combine",
    )(dest_flat, x1, gates, final_g, y)


def _layer(x2, norm1_g, w_in, pool_w, pool_scale, rel_bias, w_out, norm2_g,
           router_w, router_b, w_gate_up, b_gate_up, w_down, b_down, final_g):
    S = x2.shape[0]
    u, nat, p4, p16 = _inproj(x2, norm1_g.reshape(1, D_MODEL), w_in.astype(BF16), S)
    att = _attention(nat, p4, p16, _bias_tables(rel_bias), S)

    n_g = len(POOL_WINDOWS)
    pw_bd = (jnp.eye(n_g, dtype=F32)[:, None, :, None] * pool_w[:, :, None, :]).reshape(
        POOL_WIDTH, POOL_WIDTH).astype(BF16)
    x1, h2, topi, gates, rank, counts = _post(
        x2, u, att, pw_bd, pool_scale.reshape(1, POOL_WIDTH), w_out.astype(BF16),
        norm2_g.reshape(1, D_MODEL), router_w, router_b.reshape(1, N_EXPERTS), S)

    counts = counts.reshape(N_EXPERTS)
    blocks_per = (counts + MOE_BM - 1) // MOE_BM
    block_end = jnp.cumsum(blocks_per)
    pstart = ((block_end - blocks_per) * MOE_BM).astype(jnp.int32)
    nb = (S * TOP_K) // MOE_BM + N_EXPERTS
    block_exp = jnp.minimum(
        jnp.sum(block_end[None, :] <= jnp.arange(nb, dtype=jnp.int32)[:, None], axis=1),
        N_EXPERTS - 1).astype(jnp.int32)
    nblk = block_end[-1:].astype(jnp.int32)
    padlo = (pstart + counts).astype(jnp.int32)
    padhi = (pstart + blocks_per * MOE_BM).astype(jnp.int32)
    expert_ids = jnp.arange(N_EXPERTS, dtype=jnp.int32)
    dest = rank + jnp.sum(jnp.where(topi[:, :, None] == expert_ids, pstart, 0), axis=-1)
    dest_flat = dest.reshape(S * TOP_K).astype(jnp.int32)

    xb = _dispatch(dest_flat, padlo, padhi, nblk, h2, nb * MOE_BM)
    nonempty = blocks_per > 0
    first_at_or_after = jnp.flip(lax.cummin(jnp.flip(jnp.where(nonempty, expert_ids, N_EXPERTS))))
    next_expert = jnp.concatenate(
        [first_at_or_after[1:], jnp.full((1,), N_EXPERTS, jnp.int32)]).astype(jnp.int32)
    expert_slot = ((jnp.cumsum(nonempty) - nonempty) % 2).astype(jnp.int32)
    y = _experts(block_exp, nblk, next_expert, expert_slot, xb, w_gate_up,
                 b_gate_up.reshape(N_EXPERTS, 1, 2 * D_FF), w_down,
                 b_down.reshape(N_EXPERTS, 1, D_MODEL))
    return _combine(dest_flat, x1, gates, final_g.reshape(1, D_MODEL), y)


def kernel(x, norm1_g, w_in, pool_w, pool_scale, rel_bias, w_out, norm2_g, router_w, router_b,
           w_gate_up, b_gate_up, w_down, b_down, final_g):
    B, S, D = x.shape
    assert B == 1 and D == D_MODEL and S % ATT_TILE == 0, x.shape
    out = _layer(x.reshape(B * S, D), norm1_g[0], w_in[0], pool_w[0], pool_scale[0], rel_bias,
                 w_out[0], norm2_g[0], router_w[0], router_b[0], w_gate_up[0], b_gate_up[0],
                 w_down[0], b_down[0], final_g)
    return out.reshape(B, S, D)
```

```python
import math

import jax
import jax.numpy as jnp
from jax import lax
from jax.experimental import pallas as pl
from jax.experimental.pallas import tpu as pltpu

F32 = jnp.float32
BF16 = jnp.bfloat16

D_MODEL = 1024
POOL_WIDTH = 256
POOL_WINDOWS = (2, 4, 8, 16)
POOL_GROUP = 64
ATT_WIDTH = 768
HEAD_DIM = 64
N_HEADS = 12
N_PAIRS = N_HEADS // 2
DILATIONS = (1, 4, 16)
ATT_BLOCK = 128
N_BUCKETS = 32
MAX_DISTANCE = 2048
N_EXPERTS = 32
TOP_K = 4
D_FF = 1024
SWIGLU_LIMIT = 7.0
SWIGLU_ALPHA = 1.702
RMS_EPS = 1e-5
NEG_INF = -1e30
LOG2E = math.log2(math.e)

LANES = 128
ATT_TILE = 16 * ATT_BLOCK
BLOCKS_PER_TILE = ATT_TILE // ATT_BLOCK
N_UNITS = len(DILATIONS) * BLOCKS_PER_TILE
ATT_GROUP = 4
QKV_SLABS = 3 * N_PAIRS
IN_TM = 512
POST_TM = 512
MOE_BM = 256
DISPATCH_TB = 512
COMBINE_TB = 256
ROW_UNROLL = 8
VMEM_LIMIT = 56 * 1024 * 1024

_CLASS_ROWS = tuple(ATT_TILE // d for d in DILATIONS)
_SEG_ROWS = tuple(ATT_BLOCK + r for r in _CLASS_ROWS)
_SEG_BASE = (0, _SEG_ROWS[0], _SEG_ROWS[0] + DILATIONS[1] * _SEG_ROWS[1])
KV_ROWS = _SEG_BASE[2] + DILATIONS[2] * _SEG_ROWS[2]


def _rms(x, g):
    return x * lax.rsqrt(jnp.mean(x * x, axis=-1, keepdims=True) + RMS_EPS) * g


ROW_TILE = D_MODEL // LANES


def _store_row_tiles(ref, x):
    n = x.shape[0]
    for c in range(ROW_TILE):
        ref[pl.ds(c, n, stride=ROW_TILE), :] = x[:, c * LANES:(c + 1) * LANES]


def _load_row_tiles(ref, n):
    return jnp.concatenate(
        [ref[pl.ds(c, n, stride=ROW_TILE), :] for c in range(ROW_TILE)], axis=-1)


def _inproj_kernel(x_ref, g_ref, w_ref, u_ref, nat_ref, p4_ref, p16_ref, h_ref, acc_ref,
                   mod4_ref):
    h_ref[...] = _rms(x_ref[...], g_ref[...]).astype(BF16)
    u_ref[...] = jnp.dot(h_ref[...], w_ref[:, :POOL_WIDTH], preferred_element_type=F32)
    for jj in range(QKV_SLABS // 2):
        c0 = POOL_WIDTH + 2 * jj * LANES
        res = jnp.dot(h_ref[...], w_ref[:, c0:c0 + 2 * LANES], preferred_element_type=F32)
        for s in (0, 1):
            j = 2 * jj + s
            slab = res[:, s * LANES:(s + 1) * LANES]
            if j < N_PAIRS:
                slab = slab * (HEAD_DIM ** -0.5 * LOG2E)
            acc_ref[j] = slab
            nat_ref[j] = slab.astype(BF16)
    d4, d16 = DILATIONS[1], DILATIONS[2]
    for j in range(QKV_SLABS):
        for r4 in range(d4):
            rows = acc_ref[j, pl.ds(r4, IN_TM // d4, stride=d4), :]
            p4_ref[j, r4] = rows.astype(BF16)
            mod4_ref[j, r4] = rows
        for r4 in range(d4):
            for c in range(d16 // d4):
                p16_ref[j, r4 + d4 * c] = mod4_ref[
                    j, r4, pl.ds(c, IN_TM // d16, stride=d4), :].astype(BF16)


def _inproj(x, g, w_bf, S):
    sub = ATT_TILE // IN_TM
    n_tiles = S // ATT_TILE

    def res_spec(d):
        return pl.BlockSpec((QKV_SLABS, None, d, IN_TM // d, LANES),
                            lambda i: (0, i // sub, 0, i % sub, 0))

    def res_shape(d):
        return jax.ShapeDtypeStruct((QKV_SLABS, n_tiles, d, ATT_TILE // d, LANES), BF16)

    return pl.pallas_call(
        _inproj_kernel,
        grid=(S // IN_TM,),
        in_specs=[
            pl.BlockSpec((IN_TM, D_MODEL), lambda i: (i, 0)),
            pl.BlockSpec((1, D_MODEL), lambda i: (0, 0)),
            pl.BlockSpec((D_MODEL, POOL_WIDTH + 3 * ATT_WIDTH), lambda i: (0, 0)),
        ],
        out_specs=[
            pl.BlockSpec((IN_TM, POOL_WIDTH), lambda i: (i, 0)),
            pl.BlockSpec((QKV_SLABS, IN_TM, LANES), lambda i: (0, i, 0)),
            res_spec(DILATIONS[1]),
            res_spec(DILATIONS[2]),
        ],
        out_shape=[
            jax.ShapeDtypeStruct((S, POOL_WIDTH), F32),
            jax.ShapeDtypeStruct((QKV_SLABS, S, LANES), BF16),
            res_shape(DILATIONS[1]),
            res_shape(DILATIONS[2]),
        ],
        scratch_shapes=[pltpu.VMEM((IN_TM, D_MODEL), BF16),
                        pltpu.VMEM((QKV_SLABS, IN_TM, LANES), F32),
                        pltpu.VMEM((QKV_SLABS, DILATIONS[1], IN_TM // DILATIONS[1], LANES), F32)],
        compiler_params=pltpu.CompilerParams(
            dimension_semantics=("arbitrary",), vmem_limit_bytes=VMEM_LIMIT),
        name="inproj",
    )(x, g, w_bf)


def _t5_bucket(dist):
    max_exact = N_BUCKETS // 2
    is_small = dist < max_exact
    nf = jnp.maximum(dist, 1).astype(jnp.float32)
    large = max_exact + (jnp.log(nf / max_exact) / math.log(MAX_DISTANCE / max_exact)
                         * (N_BUCKETS - max_exact)).astype(jnp.int32)
    large = jnp.minimum(large, N_BUCKETS - 1)
    return jnp.where(is_small, dist, large)


def _bias_kernel(ids_ref, rb_ref, o_ref):
    h = pl.program_id(0)
    col = lax.broadcasted_iota(jnp.int32, (ATT_BLOCK, 2 * ATT_BLOCK), 1)
    for b in range(len(DILATIONS)):
        ids = ids_ref[b]
        tab = jnp.full(ids.shape, NEG_INF, F32)
        for k in range(N_BUCKETS):
            tab = jnp.where(ids == k, rb_ref[k, h] * LOG2E, tab)
        o_ref[0, b] = tab
        o_ref[1, b] = jnp.where(col < ATT_BLOCK, NEG_INF, tab)


def _bias_tables(rel_bias):
    qi = jnp.arange(ATT_BLOCK)[:, None]
    kj = jnp.arange(2 * ATT_BLOCK)[None, :]
    dist = qi + ATT_BLOCK - kj
    ok = (dist >= 0) & (dist <= ATT_BLOCK)
    ids = jnp.stack([jnp.where(ok, _t5_bucket(jnp.clip(dist, 0) * d), -1) for d in DILATIONS])
    return pl.pallas_call(
        _bias_kernel,
        grid=(N_HEADS,),
        in_specs=[
            pl.BlockSpec((len(DILATIONS), ATT_BLOCK, 2 * ATT_BLOCK), lambda h: (0, 0, 0)),
            pl.BlockSpec(memory_space=pltpu.SMEM),
        ],
        out_specs=pl.BlockSpec((2, len(DILATIONS), None, ATT_BLOCK, 2 * ATT_BLOCK),
                               lambda h: (0, 0, h, 0, 0)),
        out_shape=jax.ShapeDtypeStruct(
            (2, len(DILATIONS), N_HEADS, ATT_BLOCK, 2 * ATT_BLOCK), F32),
        compiler_params=pltpu.CompilerParams(dimension_semantics=("arbitrary",)),
        name="bias_tables",
    )(ids.astype(jnp.int32), rel_bias.astype(F32))


def _attn_kernel(qn, knp, kn, vnp, vn, q4, k4p, k4, v4p, v4, q16, k16p, k16, v16p, v16,
                 bias_ref, o_ref, qbuf, kbuf, vbuf0, vbuf1, sbuf0, sbuf1, pbuf0, pbuf1,
                 obuf, mbuf, lbuf, tbuf):
    i = pl.program_id(0)
    at_start = i == 0
    G = ATT_GROUP
    n_groups = N_UNITS // G
    sbufs, pbufs, vbufs = (sbuf0, sbuf1), (pbuf0, pbuf1), (vbuf0, vbuf1)
    lane = lax.broadcasted_iota(jnp.int32, (ATT_BLOCK, LANES), 1)
    lo = lane < HEAD_DIM
    head_mask = (jnp.where(lo, 1.0, 0.0).astype(BF16), jnp.where(lo, 0.0, 1.0).astype(BF16))

    @pl.when(jnp.logical_and(at_start, pl.program_id(1) == 0))
    def _():
        for buf in sbufs + pbufs:
            buf[...] = jnp.zeros_like(buf)

    for b, q in enumerate((qn, q4, q16)):
        qbuf[b * ATT_TILE:(b + 1) * ATT_TILE] = q[...]

    def fill_k(dst0, rows, val):
        kbuf[dst0:dst0 + rows] = val

    def fill_v(dst0, rows, val):
        for a in (0, 1):
            keep = jnp.broadcast_to(head_mask[a][0:1], val.shape) > 0
            vbufs[a][dst0:dst0 + rows] = jnp.where(keep, val, jnp.ones_like(val))

    for fill, prevs, curs in ((fill_k, (knp, k4p, k16p), (kn, k4, k16)),
                              (fill_v, (vnp, v4p, v16p), (vn, v4, v16))):
        fill(0, ATT_BLOCK, prevs[0][...])
        fill(ATT_BLOCK, ATT_TILE, curs[0][...])
        for b in (1, 2):
            rows = _CLASS_ROWS[b]
            for c in range(DILATIONS[b]):
                base = _SEG_BASE[b] + c * _SEG_ROWS[b]
                fill(base, ATT_BLOCK, prevs[b][c])
                fill(base + ATT_BLOCK, rows, curs[b][c * rows:(c + 1) * rows])

    def geometry(u):
        b = u // BLOCKS_PER_TILE
        w = u % BLOCKS_PER_TILE
        sh = 4 - 2 * b
        c = lax.shift_right_logical(w, sh)
        a = w - lax.shift_left(c, sh)
        seg = ATT_BLOCK + lax.shift_right_logical(ATT_TILE, 2 * b)
        base = b * _SEG_BASE[1] + jnp.where(b == 2, _SEG_BASE[2] - 2 * _SEG_BASE[1], 0)
        krow = pl.multiple_of(base + c * seg + a * ATT_BLOCK, ATT_BLOCK)
        variant = jnp.logical_and(at_start, a == 0).astype(jnp.int32)
        return b, krow, variant

    def scores(g, sbuf):
        for j in range(G):
            u = g * G + j
            b, krow, variant = geometry(u)
            q = qbuf[pl.ds(pl.multiple_of(u * ATT_BLOCK, ATT_BLOCK), ATT_BLOCK), :]
            k = kbuf[pl.ds(krow, 2 * ATT_BLOCK), :]
            for a in (0, 1):
                s = lax.dot_general(q * head_mask[a], k, (((1,), (1,)), ((), ())),
                                    preferred_element_type=F32)
                sbuf[j, a] = s + bias_ref[variant, b, a]

    def softmax(g, sbuf, pbuf):
        for j in range(G):
            rows = pl.ds(pl.multiple_of((g * G + j) * ATT_BLOCK, ATT_BLOCK), ATT_BLOCK)
            maxes = []
            for a in (0, 1):
                s = sbuf[j, a]
                m = jnp.max(s, axis=-1, keepdims=True)
                pbuf[j, a] = jnp.exp2(s - m).astype(BF16)
                maxes.append(jnp.broadcast_to(m, (ATT_BLOCK, LANES)))
            mbuf[rows, :] = jnp.where(lo, maxes[0], maxes[1])

    def values(g, pbuf):
        for j in range(G):
            u = g * G + j
            _, krow, _ = geometry(u)
            rows = pl.ds(pl.multiple_of(u * ATT_BLOCK, ATT_BLOCK), ATT_BLOCK)
            o0 = jnp.dot(pbuf[j, 0], vbuf0[pl.ds(krow, 2 * ATT_BLOCK), :],
                         preferred_element_type=F32)
            o1 = jnp.dot(pbuf[j, 1], vbuf1[pl.ds(krow, 2 * ATT_BLOCK), :],
                         preferred_element_type=F32)
            obuf[rows, :] = jnp.where(lo, o0, o1)
            lbuf[rows, :] = pltpu.roll(jnp.where(lo, o1, o0), HEAD_DIM, axis=1)

    def trip_pair(tt, carry):
        clamp = lambda g: jnp.clip(g, 0, n_groups - 1)
        for par in (0, 1):
            t = 2 * tt + par
            values(clamp(t - 2), pbufs[par])
            scores(clamp(t), sbufs[par])
            softmax(clamp(t - 1), sbufs[1 - par], pbufs[1 - par])
        return carry

    assert n_groups % 2 == 0
    lax.fori_loop(0, (n_groups + 2) // 2, trip_pair, 0)

    d4, d16 = DILATIONS[1], DILATIONS[2]
    rows4, rows16 = _CLASS_ROWS[1], _CLASS_ROWS[2]
    for n, buf in enumerate((obuf, mbuf, lbuf)):
        for r4 in range(d4):
            for c in range(d16 // d4):
                src = 2 * ATT_TILE + (r4 + d4 * c) * rows16
                tbuf[6 + n, pl.ds(r4 * rows4 + c, rows16, stride=d4), :] = buf[src:src + rows16]
        for r4 in range(d4):
            dst = pl.ds(r4, rows4, stride=d4)
            tbuf[n, dst, :] = buf[ATT_TILE + r4 * rows4:ATT_TILE + (r4 + 1) * rows4]
            tbuf[3 + n, dst, :] = tbuf[6 + n, r4 * rows4:(r4 + 1) * rows4, :]

    def merge(c, carry):
        rows = pl.ds(pl.multiple_of(c * ATT_BLOCK, ATT_BLOCK), ATT_BLOCK)
        ms = (mbuf[rows, :], tbuf[1, rows, :], tbuf[4, rows, :])
        os_ = (obuf[rows, :], tbuf[0, rows, :], tbuf[3, rows, :])
        ls = (lbuf[rows, :], tbuf[2, rows, :], tbuf[5, rows, :])
        mm = jnp.maximum(jnp.maximum(ms[0], ms[1]), ms[2])
        ws = [jnp.exp2(m - mm) for m in ms]
        num = ws[0] * os_[0] + ws[1] * os_[1] + ws[2] * os_[2]
        den = ws[0] * ls[0] + ws[1] * ls[1] + ws[2] * ls[2]
        o_ref[rows, :] = (num / den).astype(o_ref.dtype)
        return carry

    lax.fori_loop(0, BLOCKS_PER_TILE, merge, 0)


def _attention(nat, p4, p16, bias, S):
    n_tiles = S // ATT_TILE
    d4, d16 = DILATIONS[1], DILATIONS[2]
    flat4 = p4.reshape(QKV_SLABS, S, LANES)
    flat16 = p16.reshape(QKV_SLABS, S, LANES)
    blk4 = p4.reshape(QKV_SLABS, n_tiles, d4, _CLASS_ROWS[1] // ATT_BLOCK, ATT_BLOCK, LANES)
    prev = lambda i: jnp.maximum(i - 1, 0)

    def cur(off):
        return pl.BlockSpec((None, ATT_TILE, LANES), lambda i, p: (off + p, i, 0))

    def prev_nat(off):
        return pl.BlockSpec((None, ATT_BLOCK, LANES),
                            lambda i, p: (off + p, jnp.maximum(i * BLOCKS_PER_TILE - 1, 0), 0))

    def prev4(off):
        last = _CLASS_ROWS[1] // ATT_BLOCK - 1
        return pl.BlockSpec((None, None, d4, None, ATT_BLOCK, LANES),
                            lambda i, p: (off + p, prev(i), 0, last, 0, 0))

    def prev16(off):
        return pl.BlockSpec((None, None, d16, ATT_BLOCK, LANES),
                            lambda i, p: (off + p, prev(i), 0, 0, 0))

    k_off, v_off = N_PAIRS, 2 * N_PAIRS
    return pl.pallas_call(
        _attn_kernel,
        grid=(n_tiles, N_PAIRS),
        in_specs=[
            cur(0), prev_nat(k_off), cur(k_off), prev_nat(v_off), cur(v_off),
            cur(0), prev4(k_off), cur(k_off), prev4(v_off), cur(v_off),
            cur(0), prev16(k_off), cur(k_off), prev16(v_off), cur(v_off),
            pl.BlockSpec((2, len(DILATIONS), 2, ATT_BLOCK, 2 * ATT_BLOCK),
                         lambda i, p: (0, 0, p, 0, 0)),
        ],
        out_specs=pl.BlockSpec((None, ATT_TILE, LANES), lambda i, p: (p, i, 0)),
        out_shape=jax.ShapeDtypeStruct((N_PAIRS, S, LANES), BF16),
        scratch_shapes=[
            pltpu.VMEM((N_UNITS * ATT_BLOCK, LANES), BF16),
            pltpu.VMEM((KV_ROWS, LANES), BF16),
            pltpu.VMEM((KV_ROWS, LANES), BF16),
            pltpu.VMEM((KV_ROWS, LANES), BF16),
            pltpu.VMEM((ATT_GROUP, 2, ATT_BLOCK, 2 * ATT_BLOCK), F32),
            pltpu.VMEM((ATT_GROUP, 2, ATT_BLOCK, 2 * ATT_BLOCK), F32),
            pltpu.VMEM((ATT_GROUP, 2, ATT_BLOCK, 2 * ATT_BLOCK), BF16),
            pltpu.VMEM((ATT_GROUP, 2, ATT_BLOCK, 2 * ATT_BLOCK), BF16),
            pltpu.VMEM((N_UNITS * ATT_BLOCK, LANES), F32),
            pltpu.VMEM((N_UNITS * ATT_BLOCK, LANES), F32),
            pltpu.VMEM((N_UNITS * ATT_BLOCK, LANES), F32),
            pltpu.VMEM((9, ATT_TILE, LANES), F32),
        ],
        compiler_params=pltpu.CompilerParams(
            dimension_semantics=("arbitrary", "arbitrary"), vmem_limit_bytes=VMEM_LIMIT),
        name="attention",
    )(nat, nat, nat, nat, nat, flat4, blk4, flat4, blk4, flat4,
      flat16, p16, flat16, p16, flat16, bias)


def _post_kernel(x_ref, u_ref, up_ref, att_ref, pw_ref, ps_ref, wo_ref, g2_ref, rwh_ref, rwl_ref, rb_ref,
                 x1_ref, h2_ref, topi_ref, gate_ref, rank_ref, cnt_ref, ue, carry):
    i = pl.program_id(0)
    tm = POST_TM
    maxw = max(POOL_WINDOWS)

    @pl.when(i == 0)
    def _():
        carry[...] = jnp.zeros_like(carry)

    hist = up_ref[...]
    ue[0:maxw] = jnp.where(i == 0, jnp.zeros_like(hist), hist)
    ue[maxw:] = u_ref[...]
    t_glob = i * tm + lax.broadcasted_iota(jnp.int32, (tm, LANES), 0)
    lane = lax.broadcasted_iota(jnp.int32, (tm, LANES), 1)
    halves = []
    for half, (w_lo, w_hi) in enumerate(((2, 4), (8, 16))):
        cols = slice(half * LANES, (half + 1) * LANES)
        s = ue[maxw:, cols]
        tok = s
        s_lo = None
        for j in range(1, w_hi):
            s = s + ue[maxw - j:maxw - j + tm, cols]
            if j == w_lo - 1:
                s_lo = s
        cnt_lo = jnp.minimum(t_glob + 1, w_lo).astype(F32)
        cnt_hi = jnp.minimum(t_glob + 1, w_hi).astype(F32)
        halves.append(jnp.where(lane < POOL_GROUP, s_lo / cnt_lo, s / cnt_hi) - tok)
    pooled = jnp.concatenate(halves, axis=-1)
    mixed = jnp.dot(pooled.astype(BF16), pw_ref[...], preferred_element_type=F32) * ps_ref[...]

    mix = jnp.concatenate([mixed.astype(BF16)] + [att_ref[j] for j in range(N_PAIRS)], axis=-1)
    x1 = x_ref[...] + jnp.dot(mix, wo_ref[...], preferred_element_type=F32)
    x1_ref[...] = x1
    h2 = _rms(x1, g2_ref[...])
    _store_row_tiles(h2_ref, h2)

    h2_hi = h2.astype(BF16)
    h2_lo = (h2 - h2_hi.astype(F32)).astype(BF16)
    logits = (jnp.dot(h2_hi, rwh_ref[...], preferred_element_type=F32)
              + jnp.dot(h2_hi, rwl_ref[...], preferred_element_type=F32)
              + jnp.dot(h2_lo, rwh_ref[...], preferred_element_type=F32)) + rb_ref[...]
    eidx = lax.broadcasted_iota(jnp.int32, (tm, N_EXPERTS), 1)
    vals = logits
    top_v, top_i = [], []
    for _ in range(TOP_K):
        m = jnp.max(vals, axis=-1, keepdims=True)
        idx = jnp.min(jnp.where(vals == m, eidx, N_EXPERTS), axis=-1, keepdims=True)
        top_v.append(m)
        top_i.append(idx)
        vals = jnp.where(eidx == idx, -jnp.inf, vals)
    ex = [jnp.exp(v - top_v[0]) for v in top_v]
    den = ex[0] + ex[1] + ex[2] + ex[3]

    hot = [(eidx == idx) for idx in top_i]
    cnt = sum(h.astype(F32) for h in hot)
    row = lax.broadcasted_iota(jnp.int32, (tm, tm), 0)
    col = lax.broadcasted_iota(jnp.int32, (tm, tm), 1)
    strict = jnp.where(col < row, 1.0, 0.0).astype(BF16)
    before = jnp.dot(strict, cnt.astype(BF16), preferred_element_type=F32) + carry[...]
    ranks = [jnp.sum(jnp.where(h, before, 0.0), axis=-1, keepdims=True) for h in hot]
    carry[...] = carry[...] + jnp.sum(cnt, axis=0, keepdims=True)
    cnt_ref[...] = carry[...].astype(jnp.int32)

    k4 = lax.broadcasted_iota(jnp.int32, (tm, TOP_K), 1)

    def pack(cols):
        out = jnp.broadcast_to(cols[0], (tm, TOP_K))
        for k in range(1, TOP_K):
            out = jnp.where(k4 == k, jnp.broadcast_to(cols[k], (tm, TOP_K)), out)
        return out

    topi_ref[...] = pack(top_i)
    gate_ref[...] = pack([e / den for e in ex])
    rank_ref[...] = pack(ranks).astype(jnp.int32)


def _post(x, u, att, pw_bd, pool_scale, wo_bf, g2, router_w, router_b, S):
    tm = POST_TM
    maxw = max(POOL_WINDOWS)
    row = lambda w: pl.BlockSpec((tm, w), lambda i: (i, 0))
    const = lambda a, b: pl.BlockSpec((a, b), lambda i: (0, 0))
    rw_hi = router_w.astype(BF16)
    rw_hi_rest = (router_w - rw_hi.astype(F32)).astype(BF16)
    return pl.pallas_call(
        _post_kernel,
        grid=(S // tm,),
        in_specs=[
            row(D_MODEL),
            row(POOL_WIDTH),
            pl.BlockSpec((maxw, POOL_WIDTH), lambda i: (jnp.maximum(i * (tm // maxw) - 1, 0), 0)),
            pl.BlockSpec((N_PAIRS, tm, LANES), lambda i: (0, i, 0)),
            const(POOL_WIDTH, POOL_WIDTH),
            const(1, POOL_WIDTH),
            const(D_MODEL, D_MODEL),
            const(1, D_MODEL),
            const(D_MODEL, N_EXPERTS),
            const(D_MODEL, N_EXPERTS),
            const(1, N_EXPERTS),
        ],
        out_specs=[row(D_MODEL), pl.BlockSpec((tm * ROW_TILE, LANES), lambda i: (i, 0)),
                   row(TOP_K), row(TOP_K), row(TOP_K), const(1, N_EXPERTS)],
        out_shape=[
            jax.ShapeDtypeStruct((S, D_MODEL), F32),
            jax.ShapeDtypeStruct((S * ROW_TILE, LANES), F32),
            jax.ShapeDtypeStruct((S, TOP_K), jnp.int32),
            jax.ShapeDtypeStruct((S, TOP_K), F32),
            jax.ShapeDtypeStruct((S, TOP_K), jnp.int32),
            jax.ShapeDtypeStruct((1, N_EXPERTS), jnp.int32),
        ],
        scratch_shapes=[pltpu.VMEM((tm + maxw, POOL_WIDTH), F32),
                        pltpu.VMEM((1, N_EXPERTS), F32)],
        compiler_params=pltpu.CompilerParams(
            dimension_semantics=("arbitrary",), vmem_limit_bytes=VMEM_LIMIT),
        name="post_attention",
    )(x, u, u, att, pw_bd, pool_scale, wo_bf, g2, rw_hi, rw_hi_rest, router_b)


def _expert_kernel(bexp_ref, nblk_ref, nexte_ref, slot_ref, xb_ref, wgu_hbm, bgu_ref, wd_hbm,
                   bd_ref, y_ref, wgu_f32, wd_f32, wgu_bf, wd_bf, sems):
    b = pl.program_id(0)
    e = bexp_ref[b]
    prev = bexp_ref[jnp.maximum(b - 1, 0)]
    fresh = jnp.logical_and(jnp.logical_or(b == 0, e != prev), b < nblk_ref[0])

    def fetch(expert, slot):
        return (pltpu.make_async_copy(wgu_hbm.at[expert], wgu_f32.at[slot], sems.at[0, slot]),
                pltpu.make_async_copy(wd_hbm.at[expert], wd_f32.at[slot], sems.at[1, slot]))

    @pl.when(b == 0)
    def _():
        for cp in fetch(e, slot_ref[e]):
            cp.start()

    @pl.when(fresh)
    def _():
        slot = slot_ref[e]
        for cp in fetch(e, slot):
            cp.wait()
        nxt = nexte_ref[e]

        @pl.when(nxt < N_EXPERTS)
        def _():
            for cp in fetch(nxt, 1 - slot):
                cp.start()

        wgu_bf[...] = wgu_f32[slot].astype(BF16)
        wd_bf[...] = wd_f32[slot].astype(BF16)

    @pl.when(b < nblk_ref[0])
    def _():
        x = _load_row_tiles(xb_ref, MOE_BM).astype(BF16)
        gu = jnp.dot(x, wgu_bf[...], preferred_element_type=F32) + bgu_ref[...]
        gate = jnp.minimum(gu[:, :D_FF], SWIGLU_LIMIT)
        up = jnp.clip(gu[:, D_FF:], -SWIGLU_LIMIT, SWIGLU_LIMIT)
        act = (up + 1.0) * (gate * jax.nn.sigmoid(gate * SWIGLU_ALPHA))
        y = jnp.dot(act.astype(BF16), wd_bf[...], preferred_element_type=F32) + bd_ref[...]
        _store_row_tiles(y_ref, y)

    @pl.when(b >= nblk_ref[0])
    def _():
        y_ref[...] = jnp.zeros_like(y_ref)


def _experts(block_exp, nblk, next_expert, expert_slot, xb, w_gate_up, b_gate_up, w_down, b_down):
    nb = xb.shape[0] // (MOE_BM * ROW_TILE)
    tiles = (MOE_BM * ROW_TILE, LANES)
    blk = lambda b, be, nk, *_: (jnp.minimum(b, nk[0] - 1), 0)
    wsel = lambda b, be, nk, *_: (be[jnp.minimum(b, nk[0] - 1)], 0, 0)
    return pl.pallas_call(
        _expert_kernel,
        grid_spec=pltpu.PrefetchScalarGridSpec(
            num_scalar_prefetch=4,
            grid=(nb,),
            in_specs=[
                pl.BlockSpec(tiles, blk),
                pl.BlockSpec(memory_space=pl.ANY),
                pl.BlockSpec((None, 1, 2 * D_FF), wsel),
                pl.BlockSpec(memory_space=pl.ANY),
                pl.BlockSpec((None, 1, D_MODEL), wsel),
            ],
            out_specs=pl.BlockSpec(tiles, lambda b, *_: (b, 0)),
            scratch_shapes=[pltpu.VMEM((2, D_MODEL, 2 * D_FF), F32),
                            pltpu.VMEM((2, D_FF, D_MODEL), F32),
                            pltpu.VMEM((D_MODEL, 2 * D_FF), BF16),
                            pltpu.VMEM((D_FF, D_MODEL), BF16),
                            pltpu.SemaphoreType.DMA((2, 2))],
        ),
        out_shape=jax.ShapeDtypeStruct(xb.shape, F32),
        compiler_params=pltpu.CompilerParams(
            dimension_semantics=("arbitrary",), vmem_limit_bytes=VMEM_LIMIT),
        name="moe_experts",
    )(block_exp, nblk, next_expert, expert_slot, xb, w_gate_up, b_gate_up, w_down, b_down)


def _drain_rows(wait_one, n):
    def body(j, carry):
        for _ in range(ROW_UNROLL):
            wait_one()
        return carry

    lax.fori_loop(0, n // ROW_UNROLL, body, 0)


def _dispatch_kernel(dest_ref, padlo_ref, padhi_ref, nblk_ref, h2_ref, xb_hbm, sem):
    i = pl.program_id(0)
    tb = DISPATCH_TB

    def row_copy(j, dst_row):
        src = h2_ref.at[pl.ds(pl.multiple_of(j * ROW_TILE, ROW_TILE), ROW_TILE)]
        dst = xb_hbm.at[pl.ds(pl.multiple_of(dst_row * ROW_TILE, ROW_TILE), ROW_TILE)]
        return pltpu.make_async_copy(src, dst, sem)

    def issue(jo, carry):
        for ji in range(ROW_UNROLL):
            j = jo * ROW_UNROLL + ji
            for k in range(TOP_K):
                row_copy(j, dest_ref[(i * tb + j) * TOP_K + k]).start(priority=k % 2)
        return carry

    lax.fori_loop(0, tb // ROW_UNROLL, issue, 0)
    _drain_rows(lambda: row_copy(0, 0).wait(), tb * TOP_K)

    @pl.when(i == pl.num_programs(0) - 1)
    def _():
        def per_expert(e, carry):
            lo, hi = padlo_ref[e], padhi_ref[e]

            def fill(s, c):
                row_copy(0, s).start()
                return c

            def drain(s, c):
                row_copy(0, 0).wait()
                return c

            lax.fori_loop(lo, hi, fill, 0)
            lax.fori_loop(lo, hi, drain, 0)
            return carry

        lax.fori_loop(0, N_EXPERTS, per_expert, 0)

        def block_copy(b):
            rows = MOE_BM * ROW_TILE
            return pltpu.make_async_copy(
                h2_ref.at[pl.ds(0, rows)],
                xb_hbm.at[pl.ds(pl.multiple_of(b * rows, rows), rows)], sem)

        def fill_block(b, c):
            block_copy(b).start()
            return c

        def drain_block(b, c):
            block_copy(b).wait()
            return c

        n_blocks = xb_hbm.shape[0] // (MOE_BM * ROW_TILE)
        lax.fori_loop(nblk_ref[0], n_blocks, fill_block, 0)
        lax.fori_loop(nblk_ref[0], n_blocks, drain_block, 0)


def _dispatch(dest_flat, padlo, padhi, nblk, h2, n_rows):
    N = h2.shape[0] // ROW_TILE
    assert DISPATCH_TB >= MOE_BM and DISPATCH_TB % ROW_UNROLL == 0
    return pl.pallas_call(
        _dispatch_kernel,
        grid_spec=pltpu.PrefetchScalarGridSpec(
            num_scalar_prefetch=4,
            grid=(N // DISPATCH_TB,),
            in_specs=[pl.BlockSpec((DISPATCH_TB * ROW_TILE, LANES), lambda i, *_: (i, 0))],
            out_specs=pl.BlockSpec(memory_space=pl.ANY),
            scratch_shapes=[pltpu.SemaphoreType.DMA(())],
        ),
        out_shape=jax.ShapeDtypeStruct((n_rows * ROW_TILE, LANES), F32),
        compiler_params=pltpu.CompilerParams(
            dimension_semantics=("arbitrary",), has_side_effects=True,
            vmem_limit_bytes=VMEM_LIMIT),
        name="moe_dispatch",
    )(dest_flat, padlo, padhi, nblk, h2)


def _combine_kernel(dest_ref, x1_ref, gate_ref, g_ref, y_hbm, o_ref, ybuf, sems):
    i = pl.program_id(0)
    n = pl.num_programs(0)
    tb = COMBINE_TB

    def row_copy(src_row, slot, k, j):
        src = y_hbm.at[pl.ds(pl.multiple_of(src_row * ROW_TILE, ROW_TILE), ROW_TILE)]
        dst = ybuf.at[slot, k, pl.ds(pl.multiple_of(j * ROW_TILE, ROW_TILE), ROW_TILE)]
        return pltpu.make_async_copy(src, dst, sems.at[slot])

    def gather(step, slot):
        def issue(jo, carry):
            for ji in range(ROW_UNROLL):
                j = jo * ROW_UNROLL + ji
                for k in range(TOP_K):
                    row_copy(dest_ref[(step * tb + j) * TOP_K + k], slot, k, j).start(
                        priority=k % 2)
            return carry

        lax.fori_loop(0, tb // ROW_UNROLL, issue, 0)

    slot = i % 2

    @pl.when(i == 0)
    def _():
        gather(0, 0)

    @pl.when(i + 1 < n)
    def _():
        gather(i + 1, 1 - slot)

    _drain_rows(lambda: row_copy(0, slot, 0, 0).wait(), tb * TOP_K)

    gates = gate_ref[...]
    moe = gates[:, 0:1] * _load_row_tiles(ybuf.at[slot, 0], tb)
    for k in range(1, TOP_K):
        moe = moe + gates[:, k:k + 1] * _load_row_tiles(ybuf.at[slot, k], tb)
    o_ref[...] = _rms(x1_ref[...] + moe, g_ref[...])


def _combine(dest_flat, x1, gates, final_g, y):
    N = x1.shape[0]
    tb = COMBINE_TB
    return pl.pallas_call(
        _combine_kernel,
        grid_spec=pltpu.PrefetchScalarGridSpec(
            num_scalar_prefetch=1,
            grid=(N // tb,),
            in_specs=[
                pl.BlockSpec((tb, D_MODEL), lambda i, *_: (i, 0)),
                pl.BlockSpec((tb, TOP_K), lambda i, *_: (i, 0)),
                pl.BlockSpec((1, D_MODEL), lambda i, *_: (0, 0)),
                pl.BlockSpec(memory_space=pl.ANY),
            ],
            out_specs=pl.BlockSpec((tb, D_MODEL), lambda i, *_: (i, 0)),
            scratch_shapes=[pltpu.VMEM((2, TOP_K, tb * ROW_TILE, LANES), F32),
                            pltpu.SemaphoreType.DMA((2,))],
        ),
        out_shape=jax.ShapeDtypeStruct((N, D_MODEL), F32),
        compiler_params=pltpu.CompilerParams(
            dimension_semantics=("arbitrary",), vmem_limit_bytes=VMEM_LIMIT),
        name="moe_combine",
    )(dest_flat, x1, gates, final_g, y)


def _layer(x2, norm1_g, w_in, pool_w, pool_scale, rel_bias, w_out, norm2_g,
           router_w, router_b, w_gate_up, b_gate_up, w_down, b_down, final_g):
    S = x2.shape[0]
    u, nat, p4, p16 = _inproj(x2, norm1_g.reshape(1, D_MODEL), w_in.astype(BF16), S)
    att = _attention(nat, p4, p16, _bias_tables(rel_bias), S)

    n_g = len(POOL_WINDOWS)
    pw_bd = (jnp.eye(n_g, dtype=F32)[:, None, :, None] * pool_w[:, :, None, :]).reshape(
        POOL_WIDTH, POOL_WIDTH).astype(BF16)
    x1, h2, topi, gates, rank, counts = _post(
        x2, u, att, pw_bd, pool_scale.reshape(1, POOL_WIDTH), w_out.astype(BF16),
        norm2_g.reshape(1, D_MODEL), router_w, router_b.reshape(1, N_EXPERTS), S)

    counts = counts.reshape(N_EXPERTS)
    blocks_per = (counts + MOE_BM - 1) // MOE_BM
    block_end = jnp.cumsum(blocks_per)
    pstart = ((block_end - blocks_per) * MOE_BM).astype(jnp.int32)
    nb = (S * TOP_K) // MOE_BM + N_EXPERTS
    block_exp = jnp.minimum(
        jnp.sum(block_end[None, :] <= jnp.arange(nb, dtype=jnp.int32)[:, None], axis=1),
        N_EXPERTS - 1).astype(jnp.int32)
    nblk = block_end[-1:].astype(jnp.int32)
    padlo = (pstart + counts).astype(jnp.int32)
    padhi = (pstart + blocks_per * MOE_BM).astype(jnp.int32)
    expert_ids = jnp.arange(N_EXPERTS, dtype=jnp.int32)
    dest = rank + jnp.sum(jnp.where(topi[:, :, None] == expert_ids, pstart, 0), axis=-1)
    dest_flat = dest.reshape(S * TOP_K).astype(jnp.int32)

    xb = _dispatch(dest_flat, padlo, padhi, nblk, h2, nb * MOE_BM)
    nonempty = blocks_per > 0
    first_at_or_after = jnp.flip(lax.cummin(jnp.flip(jnp.where(nonempty, expert_ids, N_EXPERTS))))
    next_expert = jnp.concatenate(
        [first_at_or_after[1:], jnp.full((1,), N_EXPERTS, jnp.int32)]).astype(jnp.int32)
    expert_slot = ((jnp.cumsum(nonempty) - nonempty) % 2).astype(jnp.int32)
    y = _experts(block_exp, nblk, next_expert, expert_slot, xb, w_gate_up,
                 b_gate_up.reshape(N_EXPERTS, 1, 2 * D_FF), w_down,
                 b_down.reshape(N_EXPERTS, 1, D_MODEL))
    return _combine(dest_flat, x1, gates, final_g.reshape(1, D_MODEL), y)


def kernel(x, norm1_g, w_in, pool_w, pool_scale, rel_bias, w_out, norm2_g, router_w, router_b,
           w_gate_up, b_gate_up, w_down, b_down, final_g):
    B, S, D = x.shape
    assert B == 1 and D == D_MODEL and S % ATT_TILE == 0, x.shape
    out = _layer(x.reshape(B * S, D), norm1_g[0], w_in[0], pool_w[0], pool_scale[0], rel_bias,
                 w_out[0], norm2_g[0], router_w[0], router_b[0], w_gate_up[0], b_gate_up[0],
                 w_down[0], b_down[0], final_g)
    return out.reshape(B, S, D)
```
